```python
import jax, jax.numpy as jnp
from jax import lax
import numpy as np

D_MODEL = 1024
BATCH = 16
SEQ = 2048
DEPTH = 2
DEC_BATCH = 32
DEC_SEQ = 8
PAST_LEN = 16384
PAGE_SIZE = 128

N_A = DEPTH // 2
N_B = DEPTH - N_A
N_DENSE = (DEPTH + 1) // 2
N_MOE = DEPTH // 2
ALPHA = (2.0 * DEPTH) ** 0.25
BETA = (8.0 * DEPTH) ** -0.25
LN_EPS = 1e-5
RMS_EPS = 1e-6
NEG_BIG = -1e30
FORCE = 1e6

A_DK = 128
A_HEADS = D_MODEL // A_DK
A_DV = D_MODEL // A_HEADS
A_WIDTH = A_HEADS * A_DK
A_CHUNK = 64

B_HEADS = 16
B_KV = 4
B_GROUP = B_HEADS // B_KV
HEAD_DIM = D_MODEL // B_HEADS
SCALE = HEAD_DIM ** -0.5
CMP_STRIDE = 16
CMP_BLOCK = 2 * CMP_STRIDE
CMP_HID = HEAD_DIM
SEL_BLOCK = 64
TOP_N = 8
N_LOCAL = 2
WINDOW = 512
Q_BLOCK = 32

D_FF = 256 * ((8 * D_MODEL // 3 + 255) // 256)
N_EXPERTS = 8
MOE_TOP_K = 2
D_FF_E = 7 * D_MODEL // 2

kernel_name = 'yoco_hgrn2_nsa_decoder_step'


def layer_norm(x, g, b):
    xf = x.astype(jnp.float32)
    xc = xf - jnp.mean(xf, -1, keepdims=True)
    var = jnp.mean(xc * xc, -1, keepdims=True)
    return (xc * lax.rsqrt(var + LN_EPS) * g.astype(jnp.float32) + b.astype(jnp.float32)).astype(x.dtype)


def masked_softmax(s, mask):
    p = jax.nn.softmax(jnp.where(mask, s, NEG_BIG), axis=-1)
    return jnp.where(mask, p, 0.0)


def gla_chunked(q, k, v, logf, s0):
    bsz, t, nh, _ = q.shape
    dv = v.shape[-1]
    c = min(A_CHUNK, t)
    n = -(-t // c)
    pad = n * c - t

    def chunks(a):
        a = jnp.pad(a, ((0, 0), (0, pad), (0, 0), (0, 0)))
        return a.reshape(bsz, n, c, nh, a.shape[-1]).transpose(1, 0, 3, 2, 4)

    qc, kc, vc = chunks(q), chunks(k), chunks(v)
    gc = jnp.cumsum(chunks(logf), axis=3)
    causal = jnp.tril(jnp.ones((c, c), dtype=bool))

    def step(s, inp):
        qj, kj, vj, gj = inp
        glast = gj[:, :, -1:, :]
        qg = qj * jnp.exp(gj)
        kg = kj * jnp.exp(-gj)
        att = jnp.where(causal, jnp.einsum('bhtd,bhsd->bhts', qg, kg), 0.0)
        o = jnp.einsum('bhtd,bhde->bhte', qg, s) + jnp.einsum('bhts,bhse->bhte', att, vj)
        s = jnp.exp(glast[:, :, 0, :, None]) * s + jnp.einsum('bhsd,bhse->bhde', kj * jnp.exp(glast - gj), vj)
        return s, o

    s, o = lax.scan(step, s0, (qc, kc, vc, gc))
    o = o.transpose(1, 0, 3, 2, 4).reshape(bsz, n * c, nh, dv)[:, :t]
    return o, s


def hgrn2_mixer(x, s0, w_in, lb, norm_g, w_out):
    bsz, t, _ = x.shape
    f32 = jnp.float32
    zq, zf, zi, zg = jnp.split(x @ w_in, 4, axis=-1)
    q = jax.nn.silu(zq.astype(f32)).reshape(bsz, t, A_HEADS, A_DK)
    f = lb + (1.0 - lb) * jax.nn.sigmoid(zf.astype(f32))
    logf = jnp.log(f).reshape(bsz, t, A_HEADS, A_DK)
    k = (1.0 - f).reshape(bsz, t, A_HEADS, A_DK)
    v = zi.astype(f32).reshape(bsz, t, A_HEADS, A_DV)
    o, s = gla_chunked(q, k, v, logf, s0.astype(f32))
    o = o * lax.rsqrt(jnp.mean(o * o, -1, keepdims=True) + RMS_EPS) * norm_g.astype(f32).reshape(A_HEADS, A_DV)
    o = o * jax.nn.silu(zg.astype(f32)).reshape(bsz, t, A_HEADS, A_DV)
    y = o.reshape(bsz, t, A_HEADS * A_DV).astype(x.dtype) @ w_out
    return y, s.astype(x.dtype)


def compress_kv(kv_cmp, cmp_pe, cmp_w1, cmp_b1, cmp_w2):
    bsz, t = kv_cmp.shape[:2]
    n_ch = t // CMP_STRIDE
    ch = kv_cmp[:, :n_ch * CMP_STRIDE].reshape(bsz, n_ch, CMP_STRIDE, 2, B_KV, HEAD_DIM)
    first = jnp.einsum('bcskgd,skdh->bckgh', ch, cmp_w1[:CMP_STRIDE])
    second = jnp.einsum('bcskgd,skdh->bckgh', ch, cmp_w1[CMP_STRIDE:])
    pos_bias = jnp.einsum('skd,skdh->kh', cmp_pe, cmp_w1) + cmp_b1
    hid = jax.nn.gelu(first[:, :-1] + second[:, 1:] + pos_bias[:, None, :])
    out = jnp.einsum('bckgh,khd->bckgd', hid, cmp_w2)
    ends = jnp.arange(n_ch - 1) * CMP_STRIDE + CMP_BLOCK - 1
    return out[:, :, 0], out[:, :, 1], ends


def nsa_attend(q, gates, qpos, kc, vc, cend, fetch, n_sel, win, wpos):
    f32 = jnp.float32
    s = jnp.einsum('bqgrd,bcgd->bgrqc', q, kc).astype(f32) * SCALE
    p_cmp = masked_softmax(s, cend[None, :] <= qpos[:, None])
    o_cmp = jnp.einsum('bgrqc,bcgd->bqgrd', p_cmp, vc.astype(f32))
    nc = kc.shape[1]
    c0 = jnp.arange(nc) * CMP_STRIDE
    j0 = jnp.arange(n_sel) * SEL_BLOCK
    overlap = ((c0[:, None] <= j0[None, :] + SEL_BLOCK - 1) & (c0[:, None] + CMP_BLOCK - 1 >= j0[None, :])).astype(f32)
    imp = jnp.einsum('bgrqc,cj->bgqj', p_cmp, overlap)
    cur = qpos // SEL_BLOCK
    jj = jnp.arange(n_sel)[None, :]
    valid = jj <= cur[:, None]
    forced = (jj == 0) | (valid & (jj > cur[:, None] - N_LOCAL))
    score = jnp.where(forced, FORCE, jnp.where(valid, imp, -FORCE))
    _, idx = lax.top_k(score, min(TOP_N, n_sel))
    pos = idx[..., None] * SEL_BLOCK + jnp.arange(SEL_BLOCK)
    ok = (idx <= cur[None, None, :, None])[..., None] & (pos <= qpos[None, None, :, None, None])
    kv = fetch(pos)
    s = jnp.einsum('bqgrd,bgqnld->bgrqnl', q, kv[..., 0, :]).astype(f32) * SCALE
    sh = s.shape
    ok2 = ok.reshape(ok.shape[0], ok.shape[1], 1, ok.shape[2], -1)
    p_sel = masked_softmax(s.reshape(sh[0], sh[1], sh[2], sh[3], -1), ok2).reshape(sh)
    o_sel = jnp.einsum('bgrqnl,bgqnld->bqgrd', p_sel, kv[..., 1, :].astype(f32))
    s = jnp.einsum('bqgrd,bkgd->bgrqk', q, win[:, :, 0]).astype(f32) * SCALE
    mw = (wpos[None, :] <= qpos[:, None]) & (wpos[None, :] >= qpos[:, None] - WINDOW) & (wpos[None, :] >= 0)
    p_win = masked_softmax(s, mw)
    o_win = jnp.einsum('bgrqk,bkgd->bqgrd', p_win, win[:, :, 1].astype(f32))
    o = gates[..., 0:1] * o_cmp + gates[..., 1:2] * o_sel + gates[..., 2:3] * o_win
    return o.astype(q.dtype)


def nsa_prompt_context(h, kv_w, cmp_pe, cmp_w1, cmp_b1, cmp_w2):
    bsz, t, _ = h.shape
    kv = (h @ kv_w).reshape(bsz, t, 3, 2, B_KV, HEAD_DIM)
    kv_cmp, kv_sel, kv_win = kv[:, :, 0], kv[:, :, 1], kv[:, :, 2]
    kc, vc, ends = compress_kv(kv_cmp, cmp_pe, cmp_w1, cmp_b1, cmp_w2)
    bi = jnp.arange(bsz)[:, None, None, None, None]
    gi = jnp.arange(B_KV)[None, :, None, None, None]

    def fetch(pos):
        return kv_sel[bi, jnp.minimum(pos, t - 1), :, gi, :]

    win_pad = jnp.pad(kv_win, ((0, 0), (WINDOW, 0), (0, 0), (0, 0), (0, 0)))
    n_sel = -(-t // SEL_BLOCK)
    nb = t // Q_BLOCK

    def attend(q, gates):
        qb = q.reshape(bsz, nb, Q_BLOCK, B_KV, B_GROUP, HEAD_DIM).swapaxes(0, 1)
        gb = gates.reshape(bsz, nb, Q_BLOCK, B_KV, B_GROUP, 3).swapaxes(0, 1)

        def one(args):
            qi, gti, s0 = args
            qpos = s0 + jnp.arange(Q_BLOCK)
            win = lax.dynamic_slice_in_dim(win_pad, s0, WINDOW + Q_BLOCK, axis=1)
            wpos = s0 - WINDOW + jnp.arange(WINDOW + Q_BLOCK)
            return nsa_attend(qi, gti, qpos, kc, vc, ends, fetch, n_sel, win, wpos)

        o = lax.map(one, (qb, gb, jnp.arange(nb) * Q_BLOCK))
        return o.swapaxes(0, 1).reshape(bsz, t, B_KV, B_GROUP, HEAD_DIM)

    return attend, (kv_cmp, kv_sel, kv_win[:, t - min(WINDOW, t):])


def nsa_sample_context(h, cache_kv_cmp, cache_kv_sel, state_kv_win, page_table, kv_w, cmp_pe, cmp_w1, cmp_b1, cmp_w2):
    bsz, t, _ = h.shape
    past = page_table.shape[1] * PAGE_SIZE
    kv = (h @ kv_w).reshape(bsz, t, 3, 2, B_KV, HEAD_DIM)
    kv_cmp, kv_sel, kv_win = kv[:, :, 0], kv[:, :, 1], kv[:, :, 2]
    past_cmp = cache_kv_cmp[page_table].reshape(bsz, past, 2, B_KV, HEAD_DIM)
    kc, vc, ends = compress_kv(jnp.concatenate([past_cmp, kv_cmp], axis=1), cmp_pe, cmp_w1, cmp_b1, cmp_w2)
    bi = jnp.arange(bsz)[:, None, None, None, None]
    gi = jnp.arange(B_KV)[None, :, None, None, None]

    def fetch(pos):
        pp = jnp.minimum(pos, past - 1)
        phys = page_table[bi, pp // PAGE_SIZE]
        old = cache_kv_sel[phys, pp % PAGE_SIZE, :, gi, :]
        new = kv_sel[bi, jnp.clip(pos - past, 0, t - 1), :, gi, :]
        return jnp.where((pos < past)[..., None, None], old, new)

    n_keep = state_kv_win.shape[1]
    win = jnp.concatenate([state_kv_win, kv_win], axis=1)
    wpos = past - n_keep + jnp.arange(n_keep + t)
    qpos = past + jnp.arange(t)
    n_sel = -(-(past + t) // SEL_BLOCK)

    def attend(q, gates):
        return nsa_attend(q, gates, qpos, kc, vc, ends, fetch, n_sel, win, wpos)

    return attend, (kv_cmp, kv_sel, win[:, -n_keep:])


def nsa_mixer(x, attend, w_qg, w_out):
    bsz, t, _ = x.shape
    z = x @ w_qg
    q = z[..., :B_HEADS * HEAD_DIM].reshape(bsz, t, B_KV, B_GROUP, HEAD_DIM)
    gates = jax.nn.sigmoid(z[..., B_HEADS * HEAD_DIM:].astype(jnp.float32)).reshape(bsz, t, B_KV, B_GROUP, 3)
    o = attend(q, gates)
    return o.reshape(bsz, t, B_HEADS * HEAD_DIM) @ w_out


def swiglu(x, w_in, w_out):
    a, u = jnp.split(x @ w_in, 2, axis=-1)
    return (jax.nn.silu(a) * u) @ w_out


def moe_swiglu(x, w_router, b_router, w_in, w_out):
    f32 = jnp.float32
    logits = (x @ w_router).astype(f32) + b_router.astype(f32)
    top_logit, top_idx = lax.top_k(logits, MOE_TOP_K)
    top_w = jax.nn.softmax(top_logit, axis=-1)
    gate = jnp.sum(jax.nn.one_hot(top_idx, N_EXPERTS, dtype=f32) * top_w[..., None], axis=-2)
    y = jnp.zeros(x.shape, f32)
    for e in range(N_EXPERTS):
        y = y + gate[..., e:e + 1] * swiglu(x, w_in[e], w_out[e]).astype(f32)
    return y.astype(x.dtype)


def setup_inputs(seed: int = 0) -> dict:
    key = jax.random.key(seed)
    ks = jax.random.split(key, 26)
    f32 = jnp.float32

    def nrm(k, shape, scale):
        return jax.random.normal(k, shape, f32) * scale

    n_pages = PAST_LEN // PAGE_SIZE
    used = DEC_BATCH * n_pages
    n_pool = used + max(1, used // 4)
    win_past = min(WINDOW, PAST_LEN)
    kv_row = (2, B_KV, HEAD_DIM)
    page_table = jax.random.permutation(ks[6], n_pool)[:used].reshape(DEC_BATCH, n_pages).astype(jnp.int32)
    return {
        'x_prompt': nrm(ks[0], (BATCH, SEQ, D_MODEL), 1.0),
        'x_sample': nrm(ks[1], (DEC_BATCH, DEC_SEQ, D_MODEL), 1.0),
        'state_hgrn': nrm(ks[2], (N_A, DEC_BATCH, A_HEADS, A_DK, A_DV), 0.3),
        'cache_kv_cmp': nrm(ks[3], (n_pool, PAGE_SIZE) + kv_row, 1.0),
        'cache_kv_sel': nrm(ks[4], (n_pool, PAGE_SIZE) + kv_row, 1.0),
        'state_kv_win': nrm(ks[5], (DEC_BATCH, win_past) + kv_row, 1.0),
        'page_table': page_table,
        'a_w_in': nrm(ks[7], (N_A, D_MODEL, 4 * A_WIDTH), D_MODEL ** -0.5),
        'a_lb_logits': nrm(ks[8], (N_A + 1, A_WIDTH), 0.5),
        'a_norm_g': 1.0 + nrm(ks[9], (N_A, A_WIDTH), 0.02),
        'a_w_out': nrm(ks[10], (N_A, A_WIDTH, D_MODEL), BETA * A_WIDTH ** -0.5),
        'b_w_qg': nrm(ks[11], (N_B, D_MODEL, B_HEADS * HEAD_DIM + 3 * B_HEADS), D_MODEL ** -0.5),
        'b_w_out': nrm(ks[12], (N_B, B_HEADS * HEAD_DIM, D_MODEL), BETA * (B_HEADS * HEAD_DIM) ** -0.5),
        'kv_w': nrm(ks[13], (D_MODEL, 6 * B_KV * HEAD_DIM), D_MODEL ** -0.5),
        'cmp_pe': nrm(ks[14], (CMP_BLOCK, 2, HEAD_DIM), 0.5),
        'cmp_w1': nrm(ks[15], (CMP_BLOCK, 2, HEAD_DIM, CMP_HID), (CMP_BLOCK * HEAD_DIM) ** -0.5),
        'cmp_b1': nrm(ks[16], (2, CMP_HID), 0.02),
        'cmp_w2': nrm(ks[17], (2, CMP_HID, HEAD_DIM), (2.0 / CMP_HID) ** 0.5),
        'ffn_w_in': nrm(ks[18], (N_DENSE, D_MODEL, 2 * D_FF), D_MODEL ** -0.5),
        'ffn_w_out': nrm(ks[19], (N_DENSE, D_FF, D_MODEL), BETA * D_FF ** -0.5),
        'moe_w_router': nrm(ks[20], (N_MOE, D_MODEL, N_EXPERTS), D_MODEL ** -0.5),
        'moe_b_router': nrm(ks[21], (N_MOE, N_EXPERTS), 0.01),
        'moe_w_in': nrm(ks[22], (N_MOE, N_EXPERTS, D_MODEL, 2 * D_FF_E), D_MODEL ** -0.5),
        'moe_w_out': nrm(ks[23], (N_MOE, N_EXPERTS, D_FF_E, D_MODEL), BETA * D_FF_E ** -0.5),
        'ln_g': 1.0 + nrm(ks[24], (DEPTH, 2, D_MODEL), 0.02),
        'ln_b': nrm(ks[25], (DEPTH, 2, D_MODEL), 0.02),
    }


def reference(x_prompt, x_sample, state_hgrn, cache_kv_cmp, cache_kv_sel, state_kv_win, page_table,
              a_w_in, a_lb_logits, a_norm_g, a_w_out, b_w_qg, b_w_out, kv_w,
              cmp_pe, cmp_w1, cmp_b1, cmp_w2, ffn_w_in, ffn_w_out,
              moe_w_router, moe_b_router, moe_w_in, moe_w_out, ln_g, ln_b):
    lower_bounds = jnp.cumsum(jax.nn.softmax(a_lb_logits.astype(jnp.float32), axis=0), axis=0)

    def run(x, hgrn_init, make_ctx):
        h = x
        states = []
        attend, kv_new = None, None
        for l in range(DEPTH):
            if l < N_A:
                mix, s = hgrn2_mixer(h, hgrn_init[l], a_w_in[l], lower_bounds[l], a_norm_g[l], a_w_out[l])
                states.append(s)
            else:
                if attend is None:
                    attend, kv_new = make_ctx(h)
                mix = nsa_mixer(h, attend, b_w_qg[l - N_A], b_w_out[l - N_A])
            h = layer_norm(ALPHA * h + mix, ln_g[l, 0], ln_b[l, 0])
            if l % 2 == 0:
                ff = swiglu(h, ffn_w_in[l // 2], ffn_w_out[l // 2])
            else:
                ff = moe_swiglu(h, moe_w_router[l // 2], moe_b_router[l // 2], moe_w_in[l // 2], moe_w_out[l // 2])
            h = layer_norm(ALPHA * h + ff, ln_g[l, 1], ln_b[l, 1])
        return h, jnp.stack(states), kv_new

    def prompt_ctx(h):
        return nsa_prompt_context(h, kv_w, cmp_pe, cmp_w1, cmp_b1, cmp_w2)

    def sample_ctx(h):
        return nsa_sample_context(h, cache_kv_cmp, cache_kv_sel, state_kv_win, page_table,
                                  kv_w, cmp_pe, cmp_w1, cmp_b1, cmp_w2)

    hgrn0 = jnp.zeros((N_A, x_prompt.shape[0], A_HEADS, A_DK, A_DV), x_prompt.dtype)
    y_prompt, hgrn_p, kv_p = run(x_prompt, hgrn0, prompt_ctx)
    y_sample, hgrn_s, kv_s = run(x_sample, state_hgrn, sample_ctx)
    cmp_p, sel_p, win_p = kv_p
    cmp_s, sel_s, win_s = kv_s
    return (y_prompt, y_sample, hgrn_p, cmp_p, sel_p, win_p, hgrn_s, cmp_s, sel_s, win_s)
```

```python
import functools

import jax
import jax.numpy as jnp
from jax import lax
from jax.experimental import pallas as pl
from jax.experimental.pallas import tpu as pltpu

F32 = jnp.float32
BF16 = jnp.bfloat16

D_MODEL = 1024
DEPTH = 2
ALPHA = (2.0 * DEPTH) ** 0.25
LN_EPS = 1e-5
RMS_EPS = 1e-6
NEG_BIG = -1e30
FORCE = 1e6
PAGE_SIZE = 128

A_DK = 128
A_HEADS = D_MODEL // A_DK
A_DV = D_MODEL // A_HEADS
A_WIDTH = A_HEADS * A_DK
A_CHUNK = 64

B_HEADS = 16
B_KV = 4
B_GROUP = B_HEADS // B_KV
HEAD_DIM = D_MODEL // B_HEADS
SCALE = HEAD_DIM ** -0.5
CMP_STRIDE = 16
CMP_BLOCK = 2 * CMP_STRIDE
SEL_BLOCK = 64
TOP_N = 8
N_LOCAL = 2
WINDOW = 512

D_FF = 256 * ((8 * D_MODEL // 3 + 255) // 256)
N_EXPERTS = 8
MOE_TOP_K = 2
D_FF_E = 7 * D_MODEL // 2

VMEM_LIMIT_BYTES = 56 * 1024 * 1024


def _params(*sem):
    return pltpu.CompilerParams(dimension_semantics=sem, vmem_limit_bytes=VMEM_LIMIT_BYTES)


def _silu(x):
    return x * (1.0 / (1.0 + jnp.exp(-x)))


def _sigmoid(x):
    return 1.0 / (1.0 + jnp.exp(-x))


def _layer_norm(x, g, b):
    xc = x - jnp.mean(x, -1, keepdims=True)
    var = jnp.mean(xc * xc, -1, keepdims=True)
    return xc * lax.rsqrt(var + LN_EPS) * g + b


def _dot(a, b):
    return jnp.dot(a.astype(BF16), b.astype(BF16), preferred_element_type=F32)


def _dot_nt(a, b):
    return lax.dot_general(a.astype(BF16), b.astype(BF16), (((1,), (1,)), ((), ())),
                           preferred_element_type=F32)


def _dot_tn(a, b):
    return lax.dot_general(a.astype(BF16), b.astype(BF16), (((0,), (0,)), ((), ())),
                           preferred_element_type=F32)


def _hgrn_kernel(x_ref, s0_ref, win_ref, lbl_ref, ng_ref, wout_ref, lng_ref, lnb_ref,
                 h_ref, sout_ref, st_ref, *, layer, seg, n_seg, carry):
    rows = seg * n_seg
    x = x_ref[...].reshape(rows, D_MODEL)
    z = jnp.dot(x.astype(BF16), win_ref[...], preferred_element_type=F32)
    zq = z[:, 0 * A_WIDTH:1 * A_WIDTH]
    zf = z[:, 1 * A_WIDTH:2 * A_WIDTH]
    v = z[:, 2 * A_WIDTH:3 * A_WIDTH]
    zg = z[:, 3 * A_WIDTH:4 * A_WIDTH]

    lbl = lbl_ref[...]
    e = jnp.exp(lbl - jnp.max(lbl, axis=0, keepdims=True))
    lb = jnp.sum(e[:layer + 1], axis=0, keepdims=True) / jnp.sum(e, axis=0, keepdims=True)

    q = _silu(zq)
    f = lb + (1.0 - lb) * _sigmoid(zf)
    logf = jnp.log(f)
    k = 1.0 - f

    r_i = lax.broadcasted_iota(jnp.int32, (seg, seg), 0)
    c_i = lax.broadcasted_iota(jnp.int32, (seg, seg), 1)
    causal = c_i <= r_i
    tri = causal.astype(F32)

    if carry:
        @pl.when(pl.program_id(1) == 0)
        def _():
            for h in range(A_HEADS):
                st_ref[h] = s0_ref[0, h].T

    o_parts = []
    for s in range(n_seg):
        sl = slice(s * seg, (s + 1) * seg)
        g = jnp.dot(tri, logf[sl], preferred_element_type=F32, precision=lax.Precision.HIGHEST)
        glast = g[seg - 1:seg, :]
        qg = q[sl] * jnp.exp(g)
        kg = k[sl] * jnp.exp(-g)
        kd = k[sl] * jnp.exp(glast - g)
        eg = jnp.exp(glast)
        vs = v[sl]
        heads = []
        for h in range(A_HEADS):
            cl = slice(h * A_DK, (h + 1) * A_DK)
            if carry:
                st = st_ref[h]
            else:
                st = s0_ref[s, h].T
            att = jnp.where(causal, _dot_nt(qg[:, cl], kg[:, cl]), 0.0)
            o = _dot_nt(qg[:, cl], st) + _dot(att, vs[:, cl])
            st_new = eg[:, cl] * st + _dot_tn(vs[:, cl], kd[:, cl])
            if carry:
                st_ref[h] = st_new
            else:
                sout_ref[s, h] = st_new.T
            o = o * lax.rsqrt(jnp.mean(o * o, -1, keepdims=True) + RMS_EPS)
            heads.append(o)
        o_parts.append(jnp.concatenate(heads, axis=1))
    o = o_parts[0] if n_seg == 1 else jnp.concatenate(o_parts, axis=0)
    o = o * ng_ref[...] * _silu(zg)
    y = jnp.dot(o.astype(BF16), wout_ref[...], preferred_element_type=F32)
    hh = _layer_norm(ALPHA * x + y, lng_ref[...], lnb_ref[...])
    h_ref[...] = hh.reshape(h_ref.shape)

    if carry:
        @pl.when(pl.program_id(1) == pl.num_programs(1) - 1)
        def _():
            for h in range(A_HEADS):
                sout_ref[0, h] = st_ref[h].T


def hgrn_layer(x, s0, w_in, lb_logits, norm_g, w_out, ln_g, ln_b, *, layer, seg, n_seg, carry):
    bsz, t, _ = x.shape
    row2 = lambda a: a.reshape(1, -1).astype(F32)
    w_in = w_in.astype(BF16)
    w_out = w_out.astype(BF16)
    const = lambda *_: (0, 0)
    if carry:
        tile = seg * n_seg
        grid = (bsz, t // tile)
        x_spec = pl.BlockSpec((1, tile, D_MODEL), lambda b, c: (b, c, 0))
        s_spec = pl.BlockSpec((1, A_HEADS, A_DK, A_DV), lambda b, c: (b, 0, 0, 0))
        sem = ("parallel", "arbitrary")
    else:
        assert t == seg
        grid = (bsz // n_seg, 1)
        x_spec = pl.BlockSpec((n_seg, seg, D_MODEL), lambda b, c: (b, 0, 0))
        s_spec = pl.BlockSpec((n_seg, A_HEADS, A_DK, A_DV), lambda b, c: (b, 0, 0, 0))
        sem = ("parallel", "arbitrary")
    kern = functools.partial(_hgrn_kernel, layer=layer, seg=seg, n_seg=n_seg, carry=carry)
    return pl.pallas_call(
        kern,
        grid=grid,
        in_specs=[
            x_spec, s_spec,
            pl.BlockSpec((D_MODEL, 4 * A_WIDTH), const),
            pl.BlockSpec(lb_logits.shape, const),
            pl.BlockSpec((1, A_WIDTH), const),
            pl.BlockSpec((A_WIDTH, D_MODEL), const),
            pl.BlockSpec((1, D_MODEL), const),
            pl.BlockSpec((1, D_MODEL), const),
        ],
        out_specs=[x_spec, s_spec],
        out_shape=[jax.ShapeDtypeStruct(x.shape, F32), jax.ShapeDtypeStruct(s0.shape, F32)],
        scratch_shapes=[pltpu.VMEM((A_HEADS, A_DV, A_DK), F32)],
        compiler_params=_params(*sem),
        name="hgrn_layer",
    )(x, s0, w_in, lb_logits.astype(F32), row2(norm_g), w_out, row2(ln_g), row2(ln_b))


def _ffn_kernel(x_ref, wa_ref, wu_ref, wo_ref, lng_ref, lnb_ref, o_ref, acc_ref):
    j = pl.program_id(1)
    xb = x_ref[...].astype(BF16)
    a = jnp.dot(xb, wa_ref[...], preferred_element_type=F32)
    u = jnp.dot(xb, wu_ref[...], preferred_element_type=F32)
    part = jnp.dot((_silu(a) * u).astype(BF16), wo_ref[...], preferred_element_type=F32)

    @pl.when(j == 0)
    def _():
        acc_ref[...] = part

    @pl.when(j > 0)
    def _():
        acc_ref[...] += part

    @pl.when(j == pl.num_programs(1) - 1)
    def _():
        o_ref[...] = _layer_norm(ALPHA * x_ref[...] + acc_ref[...], lng_ref[...], lnb_ref[...])


def ffn_layer(x, w_in, w_out, ln_g, ln_b, *, tm, tf):
    n, d = x.shape
    d_ff = w_out.shape[0]
    nf = d_ff // tf
    assert n % tm == 0 and d_ff % tf == 0
    w_in = w_in.astype(BF16)
    w_out = w_out.astype(BF16)
    row2 = lambda a: a.reshape(1, -1).astype(F32)
    return pl.pallas_call(
        _ffn_kernel,
        grid=(n // tm, nf),
        in_specs=[
            pl.BlockSpec((tm, d), lambda i, j: (i, 0)),
            pl.BlockSpec((d, tf), lambda i, j: (0, j)),
            pl.BlockSpec((d, tf), lambda i, j: (0, j + nf)),
            pl.BlockSpec((tf, d), lambda i, j: (j, 0)),
            pl.BlockSpec((1, d), lambda i, j: (0, 0)),
            pl.BlockSpec((1, d), lambda i, j: (0, 0)),
        ],
        out_specs=pl.BlockSpec((tm, d), lambda i, j: (i, 0)),
        out_shape=jax.ShapeDtypeStruct((n, d), F32),
        scratch_shapes=[pltpu.VMEM((tm, d), F32)],
        compiler_params=_params("parallel", "arbitrary"),
        name="ffn_layer",
    )(x, w_in, w_in, w_out, row2(ln_g), row2(ln_b))


def _router_gate(x, wr, br):
    logits = jnp.dot(x, wr, preferred_element_type=F32, precision=lax.Precision.HIGHEST) + br
    lane = lax.broadcasted_iota(jnp.int32, logits.shape, 1)
    m1 = jnp.max(logits, axis=-1, keepdims=True)
    i1 = jnp.min(jnp.where(logits == m1, lane, N_EXPERTS), axis=-1, keepdims=True)
    rest = jnp.where(lane == i1, -jnp.inf, logits)
    m2 = jnp.max(rest, axis=-1, keepdims=True)
    i2 = jnp.min(jnp.where(rest == m2, lane, N_EXPERTS), axis=-1, keepdims=True)
    e2 = jnp.exp(m2 - m1)
    den = 1.0 + e2
    return jnp.where(lane == i1, 1.0 / den, 0.0) + jnp.where(lane == i2, e2 / den, 0.0)


def _moe_kernel(x_ref, wr_ref, br_ref, wa_ref, wu_ref, wo_ref, lng_ref, lnb_ref, o_ref,
                gate_ref, acc_ref, tot_ref):
    e = pl.program_id(1)
    j = pl.program_id(2)
    nj = pl.num_programs(2)

    @pl.when((e == 0) & (j == 0))
    def _():
        gate_ref[...] = _router_gate(x_ref[...], wr_ref[...], br_ref[...])
        tot_ref[...] = jnp.zeros_like(tot_ref)

    xb = x_ref[...].astype(BF16)
    a = jnp.dot(xb, wa_ref[0], preferred_element_type=F32)
    u = jnp.dot(xb, wu_ref[0], preferred_element_type=F32)
    part = jnp.dot((_silu(a) * u).astype(BF16), wo_ref[0], preferred_element_type=F32)

    @pl.when(j == 0)
    def _():
        acc_ref[...] = part

    @pl.when(j > 0)
    def _():
        acc_ref[...] += part

    @pl.when(j == nj - 1)
    def _():
        gate = gate_ref[...]
        lane = lax.broadcasted_iota(jnp.int32, gate.shape, 1)
        ge = jnp.sum(jnp.where(lane == e, gate, 0.0), axis=-1, keepdims=True)
        tot_ref[...] += ge * acc_ref[...]

    @pl.when((j == nj - 1) & (e == pl.num_programs(1) - 1))
    def _():
        o_ref[...] = _layer_norm(ALPHA * x_ref[...] + tot_ref[...], lng_ref[...], lnb_ref[...])


def moe_layer_dense(x, w_router, b_router, w_in, w_out, ln_g, ln_b, *, tm, tf):
    n, d = x.shape
    ne, d_ff = w_out.shape[0], w_out.shape[1]
    nf = d_ff // tf
    assert n % tm == 0 and d_ff % tf == 0
    w_in = w_in.astype(BF16)
    w_out = w_out.astype(BF16)
    row2 = lambda a: a.reshape(1, -1).astype(F32)
    c2 = lambda i, e, j: (0, 0)
    return pl.pallas_call(
        _moe_kernel,
        grid=(n // tm, ne, nf),
        in_specs=[
            pl.BlockSpec((tm, d), lambda i, e, j: (i, 0)),
            pl.BlockSpec((d, ne), c2),
            pl.BlockSpec((1, ne), c2),
            pl.BlockSpec((1, d, tf), lambda i, e, j: (e, 0, j)),
            pl.BlockSpec((1, d, tf), lambda i, e, j: (e, 0, j + nf)),
            pl.BlockSpec((1, tf, d), lambda i, e, j: (e, j, 0)),
            pl.BlockSpec((1, d), c2),
            pl.BlockSpec((1, d), c2),
        ],
        out_specs=pl.BlockSpec((tm, d), lambda i, e, j: (i, 0)),
        out_shape=jax.ShapeDtypeStruct((n, d), F32),
        scratch_shapes=[pltpu.VMEM((tm, ne), F32), pltpu.VMEM((tm, d), F32), pltpu.VMEM((tm, d), F32)],
        compiler_params=_params("parallel", "arbitrary", "arbitrary"),
        name="moe_layer",
    )(x, w_router.astype(F32), row2(b_router), w_in, w_in, w_out, row2(ln_g), row2(ln_b))


def _proj_kernel(x_ref, w_ref, o_ref):
    o_ref[...] = jnp.dot(x_ref[...].astype(BF16), w_ref[...], preferred_element_type=F32)


def proj(x, w, *, tm):
    n, d = x.shape
    m = w.shape[1]
    return pl.pallas_call(
        _proj_kernel,
        grid=(n // tm,),
        in_specs=[pl.BlockSpec((tm, d), lambda i: (i, 0)), pl.BlockSpec((d, m), lambda i: (0, 0))],
        out_specs=pl.BlockSpec((tm, m), lambda i: (i, 0)),
        out_shape=jax.ShapeDtypeStruct((n, m), F32),
        compiler_params=_params("parallel"),
        name="proj",
    )(x, w.astype(BF16))


def _proj_ln_kernel(o_ref, x_ref, w_ref, lng_ref, lnb_ref, h_ref):
    y = jnp.dot(o_ref[...].astype(BF16), w_ref[...], preferred_element_type=F32)
    h_ref[...] = _layer_norm(ALPHA * x_ref[...] + y, lng_ref[...], lnb_ref[...])


def proj_ln(o, x, w, ln_g, ln_b, *, tm):
    n, d = x.shape
    row2 = lambda a: a.reshape(1, -1).astype(F32)
    c2 = lambda i: (0, 0)
    return pl.pallas_call(
        _proj_ln_kernel,
        grid=(n // tm,),
        in_specs=[pl.BlockSpec((tm, o.shape[1]), lambda i: (i, 0)), pl.BlockSpec((tm, d), lambda i: (i, 0)),
                  pl.BlockSpec(w.shape, c2), pl.BlockSpec((1, d), c2), pl.BlockSpec((1, d), c2)],
        out_specs=pl.BlockSpec((tm, d), lambda i: (i, 0)),
        out_shape=jax.ShapeDtypeStruct((n, d), F32),
        compiler_params=_params("parallel"),
        name="proj_ln",
    )(o, x, w.astype(BF16), row2(ln_g), row2(ln_b))


def _masked_softmax(s, mask):
    p = jax.nn.softmax(jnp.where(mask, s, NEG_BIG), axis=-1)
    return jnp.where(mask, p, 0.0)


def _compress_kv(kv_cmp, cmp_pe, cmp_w1, cmp_b1, cmp_w2):
    bsz, t = kv_cmp.shape[:2]
    n_ch = t // CMP_STRIDE
    ch = kv_cmp[:, :n_ch * CMP_STRIDE].reshape(bsz, n_ch, CMP_STRIDE, 2, B_KV, HEAD_DIM)
    first = jnp.einsum('bcskgd,skdh->bckgh', ch, cmp_w1[:CMP_STRIDE])
    second = jnp.einsum('bcskgd,skdh->bckgh', ch, cmp_w1[CMP_STRIDE:])
    pos_bias = jnp.einsum('skd,skdh->kh', cmp_pe, cmp_w1) + cmp_b1
    hid = jax.nn.gelu(first[:, :-1] + second[:, 1:] + pos_bias[:, None, :])
    out = jnp.einsum('bckgh,khd->bckgd', hid, cmp_w2)
    ends = jnp.arange(n_ch - 1) * CMP_STRIDE + CMP_BLOCK - 1
    return out[:, :, 0], out[:, :, 1], ends


def _nsa_attend(q, gates, qpos, kc, vc, cend, fetch, n_sel, win, wpos):
    s = jnp.einsum('bqgrd,bcgd->bgrqc', q, kc).astype(F32) * SCALE
    p_cmp = _masked_softmax(s, cend[None, :] <= qpos[:, None])
    o_cmp = jnp.einsum('bgrqc,bcgd->bqgrd', p_cmp, vc.astype(F32))
    nc = kc.shape[1]
    c0 = jnp.arange(nc) * CMP_STRIDE
    j0 = jnp.arange(n_sel) * SEL_BLOCK
    overlap = ((c0[:, None] <= j0[None, :] + SEL_BLOCK - 1) & (c0[:, None] + CMP_BLOCK - 1 >= j0[None, :])).astype(F32)
    imp = jnp.einsum('bgrqc,cj->bgqj', p_cmp, overlap)
    cur = qpos // SEL_BLOCK
    jj = jnp.arange(n_sel)[None, :]
    valid = jj <= cur[:, None]
    forced = (jj == 0) | (valid & (jj > cur[:, None] - N_LOCAL))
    score = jnp.where(forced, FORCE, jnp.where(valid, imp, -FORCE))
    _, idx = lax.top_k(score, min(TOP_N, n_sel))
    pos = idx[..., None] * SEL_BLOCK + jnp.arange(SEL_BLOCK)
    ok = (idx <= cur[None, None, :, None])[..., None] & (pos <= qpos[None, None, :, None, None])
    kv = fetch(pos)
    s = jnp.einsum('bqgrd,bgqnld->bgrqnl', q, kv[..., 0, :]).astype(F32) * SCALE
    sh = s.shape
    ok2 = ok.reshape(ok.shape[0], ok.shape[1], 1, ok.shape[2], -1)
    p_sel = _masked_softmax(s.reshape(sh[0], sh[1], sh[2], sh[3], -1), ok2).reshape(sh)
    o_sel = jnp.einsum('bgrqnl,bgqnld->bqgrd', p_sel, kv[..., 1, :].astype(F32))
    s = jnp.einsum('bqgrd,bkgd->bgrqk', q, win[:, :, 0]).astype(F32) * SCALE
    mw = (wpos[None, :] <= qpos[:, None]) & (wpos[None, :] >= qpos[:, None] - WINDOW) & (wpos[None, :] >= 0)
    p_win = _masked_softmax(s, mw)
    o_win = jnp.einsum('bgrqk,bkgd->bqgrd', p_win, win[:, :, 1].astype(F32))
    o = gates[..., 0:1] * o_cmp + gates[..., 1:2] * o_sel + gates[..., 2:3] * o_win
    return o.astype(q.dtype)


def _nsa_prompt_jax(kv, q, gates, cmp_pe, cmp_w1, cmp_b1, cmp_w2):
    bsz, t = kv.shape[:2]
    kv_cmp, kv_sel, kv_win = kv[:, :, 0], kv[:, :, 1], kv[:, :, 2]
    kc, vc, ends = _compress_kv(kv_cmp, cmp_pe, cmp_w1, cmp_b1, cmp_w2)
    bi = jnp.arange(bsz)[:, None, None, None, None]
    gi = jnp.arange(B_KV)[None, :, None, None, None]

    def fetch(pos):
        return kv_sel[bi, jnp.minimum(pos, t - 1), :, gi, :]

    win_pad = jnp.pad(kv_win, ((0, 0), (WINDOW, 0), (0, 0), (0, 0), (0, 0)))
    n_sel = -(-t // SEL_BLOCK)
    qb_ = 32
    nb = t // qb_
    qb = q.reshape(bsz, nb, qb_, B_KV, B_GROUP, HEAD_DIM).swapaxes(0, 1)
    gb = gates.reshape(bsz, nb, qb_, B_KV, B_GROUP, 3).swapaxes(0, 1)

    def one(args):
        qi, gti, s0 = args
        qpos = s0 + jnp.arange(qb_)
        win = lax.dynamic_slice_in_dim(win_pad, s0, WINDOW + qb_, axis=1)
        wpos = s0 - WINDOW + jnp.arange(WINDOW + qb_)
        return _nsa_attend(qi, gti, qpos, kc, vc, ends, fetch, n_sel, win, wpos)

    o = lax.map(one, (qb, gb, jnp.arange(nb) * qb_))
    return o.swapaxes(0, 1).reshape(bsz, t, B_HEADS * HEAD_DIM)


def _nsa_sample_jax(kv, q, gates, cache_kv_cmp, cache_kv_sel, state_kv_win, page_table,
                    cmp_pe, cmp_w1, cmp_b1, cmp_w2):
    bsz, t = kv.shape[:2]
    past = page_table.shape[1] * PAGE_SIZE
    kv_cmp, kv_sel, kv_win = kv[:, :, 0], kv[:, :, 1], kv[:, :, 2]
    past_cmp = cache_kv_cmp[page_table].reshape(bsz, past, 2, B_KV, HEAD_DIM)
    kc, vc, ends = _compress_kv(jnp.concatenate([past_cmp, kv_cmp], axis=1), cmp_pe, cmp_w1, cmp_b1, cmp_w2)
    bi = jnp.arange(bsz)[:, None, None, None, None]
    gi = jnp.arange(B_KV)[None, :, None, None, None]

    def fetch(pos):
        pp = jnp.minimum(pos, past - 1)
        phys = page_table[bi, pp // PAGE_SIZE]
        old = cache_kv_sel[phys, pp % PAGE_SIZE, :, gi, :]
        new = kv_sel[bi, jnp.clip(pos - past, 0, t - 1), :, gi, :]
        return jnp.where((pos < past)[..., None, None], old, new)

    n_keep = state_kv_win.shape[1]
    win = jnp.concatenate([state_kv_win, kv_win], axis=1)
    wpos = past - n_keep + jnp.arange(n_keep + t)
    qpos = past + jnp.arange(t)
    n_sel = -(-(past + t) // SEL_BLOCK)
    o = _nsa_attend(q, gates, qpos, kc, vc, ends, fetch, n_sel, win, wpos)
    return o.reshape(bsz, t, B_HEADS * HEAD_DIM), win[:, -n_keep:]


def _tile(n, pref):
    return pref if n % pref == 0 else n


def _run_group(x, s0, is_prompt, caches, a_w_in, a_lb_logits, a_norm_g, a_w_out, b_w_qg, b_w_out, kv_w,
               cmp_pe, cmp_w1, cmp_b1, cmp_w2, ffn_w_in, ffn_w_out, moe_w_router, moe_b_router,
               moe_w_in, moe_w_out, ln_g, ln_b):
    bsz, t, d = x.shape
    n = bsz * t
    if is_prompt:
        h, s_out = hgrn_layer(x, s0, a_w_in[0], a_lb_logits, a_norm_g[0], a_w_out[0], ln_g[0, 0], ln_b[0, 0],
                              layer=0, seg=A_CHUNK, n_seg=4, carry=True)
    else:
        h, s_out = hgrn_layer(x, s0, a_w_in[0], a_lb_logits, a_norm_g[0], a_w_out[0], ln_g[0, 0], ln_b[0, 0],
                              layer=0, seg=t, n_seg=8, carry=False)
    h = h.reshape(n, d)
    h = ffn_layer(h, ffn_w_in[0], ffn_w_out[0], ln_g[0, 1], ln_b[0, 1], tm=_tile(n, 1024), tf=256)
    nq = B_HEADS * HEAD_DIM
    w_cat = jnp.concatenate([kv_w, b_w_qg[0]], axis=1)
    pad = (-w_cat.shape[1]) % 128
    w_cat = jnp.pad(w_cat, ((0, 0), (0, pad)))
    z = proj(h, w_cat, tm=_tile(n, 512))
    n_kv = kv_w.shape[1]
    kv = z[:, :n_kv].reshape(bsz, t, 3, 2, B_KV, HEAD_DIM)
    q = z[:, n_kv:n_kv + nq].reshape(bsz, t, B_KV, B_GROUP, HEAD_DIM)
    gates = jax.nn.sigmoid(z[:, n_kv + nq:n_kv + nq + 3 * B_HEADS]).reshape(bsz, t, B_KV, B_GROUP, 3)
    if is_prompt:
        o = _nsa_prompt_jax(kv, q, gates, cmp_pe, cmp_w1, cmp_b1, cmp_w2)
        kv_win_out = kv[:, t - min(WINDOW, t):, 2]
    else:
        cache_kv_cmp, cache_kv_sel, state_kv_win, page_table = caches
        o, kv_win_out = _nsa_sample_jax(kv, q, gates, cache_kv_cmp, cache_kv_sel, state_kv_win, page_table,
                                        cmp_pe, cmp_w1, cmp_b1, cmp_w2)
    h = proj_ln(o.reshape(n, nq), h, b_w_out[0], ln_g[1, 0], ln_b[1, 0], tm=_tile(n, 512))
    h = moe_layer_dense(h, moe_w_router[0], moe_b_router[0], moe_w_in[0], moe_w_out[0], ln_g[1, 1], ln_b[1, 1],
                        tm=_tile(n, 1024), tf=512)
    return h.reshape(bsz, t, d), s_out[None], kv[:, :, 0], kv[:, :, 1], kv_win_out


def kernel(x_prompt, x_sample, state_hgrn, cache_kv_cmp, cache_kv_sel, state_kv_win, page_table,
           a_w_in, a_lb_logits, a_norm_g, a_w_out, b_w_qg, b_w_out, kv_w,
           cmp_pe, cmp_w1, cmp_b1, cmp_w2, ffn_w_in, ffn_w_out,
           moe_w_router, moe_b_router, moe_w_in, moe_w_out, ln_g, ln_b):
    weights = (a_w_in, a_lb_logits, a_norm_g, a_w_out, b_w_qg, b_w_out, kv_w, cmp_pe, cmp_w1, cmp_b1, cmp_w2,
               ffn_w_in, ffn_w_out, moe_w_router, moe_b_router, moe_w_in, moe_w_out, ln_g, ln_b)
    hgrn0 = jnp.zeros((x_prompt.shape[0], A_HEADS, A_DK, A_DV), F32)
    y_p, hg_p, cmp_p, sel_p, win_p = _run_group(x_prompt, hgrn0, True, None, *weights)
    y_s, hg_s, cmp_s, sel_s, win_s = _run_group(
        x_sample, state_hgrn[0], False, (cache_kv_cmp, cache_kv_sel, state_kv_win, page_table), *weights)
    return (y_p, y_s, hg_p, cmp_p, sel_p, win_p, hg_s, cmp_s, sel_s, win_s)
```

```python
import functools

import jax
import jax.numpy as jnp
from jax import lax
from jax.experimental import pallas as pl
from jax.experimental.pallas import tpu as pltpu

F32 = jnp.float32
BF16 = jnp.bfloat16

D_MODEL = 1024
DEPTH = 2
ALPHA = (2.0 * DEPTH) ** 0.25
LN_EPS = 1e-5
RMS_EPS = 1e-6
NEG_BIG = -1e30
FORCE = 1e6
PAGE_SIZE = 128

A_DK = 128
A_HEADS = D_MODEL // A_DK
A_DV = D_MODEL // A_HEADS
A_WIDTH = A_HEADS * A_DK
A_CHUNK = 64

B_HEADS = 16
B_KV = 4
B_GROUP = B_HEADS // B_KV
HEAD_DIM = D_MODEL // B_HEADS
SCALE = HEAD_DIM ** -0.5
CMP_STRIDE = 16
CMP_BLOCK = 2 * CMP_STRIDE
SEL_BLOCK = 64
TOP_N = 8
N_LOCAL = 2
WINDOW = 512

D_FF = 256 * ((8 * D_MODEL // 3 + 255) // 256)
N_EXPERTS = 8
MOE_TOP_K = 2
D_FF_E = 7 * D_MODEL // 2

VMEM_LIMIT_BYTES = 56 * 1024 * 1024


def _params(*sem):
    return pltpu.CompilerParams(dimension_semantics=sem, vmem_limit_bytes=VMEM_LIMIT_BYTES)


def _silu(x):
    return x * (1.0 / (1.0 + jnp.exp(-x)))


def _sigmoid(x):
    return 1.0 / (1.0 + jnp.exp(-x))


def _layer_norm(x, g, b):
    xc = x - jnp.mean(x, -1, keepdims=True)
    var = jnp.mean(xc * xc, -1, keepdims=True)
    return xc * lax.rsqrt(var + LN_EPS) * g + b


def _dot(a, b):
    return jnp.dot(a.astype(BF16), b.astype(BF16), preferred_element_type=F32)


def _dot_nt(a, b):
    return lax.dot_general(a.astype(BF16), b.astype(BF16), (((1,), (1,)), ((), ())),
                           preferred_element_type=F32)


def _dot_tn(a, b):
    return lax.dot_general(a.astype(BF16), b.astype(BF16), (((0,), (0,)), ((), ())),
                           preferred_element_type=F32)


def _hgrn_kernel(x_ref, s0_ref, win_ref, lbl_ref, ng_ref, wout_ref, lng_ref, lnb_ref,
                 h_ref, sout_ref, st_ref, *, layer, seg, n_seg, carry):
    rows = seg * n_seg
    x = x_ref[...].reshape(rows, D_MODEL)
    z = jnp.dot(x.astype(BF16), win_ref[...], preferred_element_type=F32)
    zq = z[:, 0 * A_WIDTH:1 * A_WIDTH]
    zf = z[:, 1 * A_WIDTH:2 * A_WIDTH]
    v = z[:, 2 * A_WIDTH:3 * A_WIDTH]
    zg = z[:, 3 * A_WIDTH:4 * A_WIDTH]

    lbl = lbl_ref[...]
    e = jnp.exp(lbl - jnp.max(lbl, axis=0, keepdims=True))
    lb = jnp.sum(e[:layer + 1], axis=0, keepdims=True) / jnp.sum(e, axis=0, keepdims=True)

    q = _silu(zq)
    f = lb + (1.0 - lb) * _sigmoid(zf)
    logf = jnp.log(f)
    k = 1.0 - f

    r_i = lax.broadcasted_iota(jnp.int32, (seg, seg), 0)
    c_i = lax.broadcasted_iota(jnp.int32, (seg, seg), 1)
    causal = c_i <= r_i
    tri = causal.astype(F32)

    if carry:
        @pl.when(pl.program_id(1) == 0)
        def _():
            for h in range(A_HEADS):
                st_ref[h] = s0_ref[0, h].T

    o_parts = []
    for s in range(n_seg):
        sl = slice(s * seg, (s + 1) * seg)
        g = jnp.dot(tri, logf[sl], preferred_element_type=F32, precision=lax.Precision.HIGHEST)
        glast = g[seg - 1:seg, :]
        qg = q[sl] * jnp.exp(g)
        kg = k[sl] * jnp.exp(-g)
        kd = k[sl] * jnp.exp(glast - g)
        eg = jnp.exp(glast)
        vs = v[sl]
        heads = []
        for h in range(A_HEADS):
            cl = slice(h * A_DK, (h + 1) * A_DK)
            if carry:
                st = st_ref[h]
            else:
                st = s0_ref[s, h].T
            att = jnp.where(causal, _dot_nt(qg[:, cl], kg[:, cl]), 0.0)
            o = _dot_nt(qg[:, cl], st) + _dot(att, vs[:, cl])
            st_new = eg[:, cl] * st + _dot_tn(vs[:, cl], kd[:, cl])
            if carry:
                st_ref[h] = st_new
            else:
                sout_ref[s, h] = st_new.T
            o = o * lax.rsqrt(jnp.mean(o * o, -1, keepdims=True) + RMS_EPS)
            heads.append(o)
        o_parts.append(jnp.concatenate(heads, axis=1))
    o = o_parts[0] if n_seg == 1 else jnp.concatenate(o_parts, axis=0)
    o = o * ng_ref[...] * _silu(zg)
    y = jnp.dot(o.astype(BF16), wout_ref[...], preferred_element_type=F32)
    hh = _layer_norm(ALPHA * x + y, lng_ref[...], lnb_ref[...])
    h_ref[...] = hh.reshape(h_ref.shape)

    if carry:
        @pl.when(pl.program_id(1) == pl.num_programs(1) - 1)
        def _():
            for h in range(A_HEADS):
                sout_ref[0, h] = st_ref[h].T


def hgrn_layer(x, s0, w_in, lb_logits, norm_g, w_out, ln_g, ln_b, *, layer, seg, n_seg, carry):
    bsz, t, _ = x.shape
    row2 = lambda a: a.reshape(1, -1).astype(F32)
    w_in = w_in.astype(BF16)
    w_out = w_out.astype(BF16)
    const = lambda *_: (0, 0)
    if carry:
        tile = seg * n_seg
        grid = (bsz, t // tile)
        x_spec = pl.BlockSpec((1, tile, D_MODEL), lambda b, c: (b, c, 0))
        s_spec = pl.BlockSpec((1, A_HEADS, A_DK, A_DV), lambda b, c: (b, 0, 0, 0))
        sem = ("parallel", "arbitrary")
    else:
        assert t == seg
        grid = (bsz // n_seg, 1)
        x_spec = pl.BlockSpec((n_seg, seg, D_MODEL), lambda b, c: (b, 0, 0))
        s_spec = pl.BlockSpec((n_seg, A_HEADS, A_DK, A_DV), lambda b, c: (b, 0, 0, 0))
        sem = ("parallel", "arbitrary")
    kern = functools.partial(_hgrn_kernel, layer=layer, seg=seg, n_seg=n_seg, carry=carry)
    return pl.pallas_call(
        kern,
        grid=grid,
        in_specs=[
            x_spec, s_spec,
            pl.BlockSpec((D_MODEL, 4 * A_WIDTH), const),
            pl.BlockSpec(lb_logits.shape, const),
            pl.BlockSpec((1, A_WIDTH), const),
            pl.BlockSpec((A_WIDTH, D_MODEL), const),
            pl.BlockSpec((1, D_MODEL), const),
            pl.BlockSpec((1, D_MODEL), const),
        ],
        out_specs=[x_spec, s_spec],
        out_shape=[jax.ShapeDtypeStruct(x.shape, F32), jax.ShapeDtypeStruct(s0.shape, F32)],
        scratch_shapes=[pltpu.VMEM((A_HEADS, A_DV, A_DK), F32)],
        compiler_params=_params(*sem),
        name="hgrn_layer",
    )(x, s0, w_in, lb_logits.astype(F32), row2(norm_g), w_out, row2(ln_g), row2(ln_b))


def _ffn_kernel(x_ref, wa_ref, wu_ref, wo_ref, lng_ref, lnb_ref, o_ref, acc_ref):
    j = pl.program_id(1)
    xb = x_ref[...].astype(BF16)
    a = jnp.dot(xb, wa_ref[...], preferred_element_type=F32)
    u = jnp.dot(xb, wu_ref[...], preferred_element_type=F32)
    part = jnp.dot((_silu(a) * u).astype(BF16), wo_ref[...], preferred_element_type=F32)

    @pl.when(j == 0)
    def _():
        acc_ref[...] = part

    @pl.when(j > 0)
    def _():
        acc_ref[...] += part

    @pl.when(j == pl.num_programs(1) - 1)
    def _():
        o_ref[...] = _layer_norm(ALPHA * x_ref[...] + acc_ref[...], lng_ref[...], lnb_ref[...])


def ffn_layer(x, w_in, w_out, ln_g, ln_b, *, tm, tf):
    n, d = x.shape
    d_ff = w_out.shape[0]
    nf = d_ff // tf
    assert n % tm == 0 and d_ff % tf == 0
    w_in = w_in.astype(BF16)
    w_out = w_out.astype(BF16)
    row2 = lambda a: a.reshape(1, -1).astype(F32)
    return pl.pallas_call(
        _ffn_kernel,
        grid=(n // tm, nf),
        in_specs=[
            pl.BlockSpec((tm, d), lambda i, j: (i, 0)),
            pl.BlockSpec((d, tf), lambda i, j: (0, j)),
            pl.BlockSpec((d, tf), lambda i, j: (0, j + nf)),
            pl.BlockSpec((tf, d), lambda i, j: (j, 0)),
            pl.BlockSpec((1, d), lambda i, j: (0, 0)),
            pl.BlockSpec((1, d), lambda i, j: (0, 0)),
        ],
        out_specs=pl.BlockSpec((tm, d), lambda i, j: (i, 0)),
        out_shape=jax.ShapeDtypeStruct((n, d), F32),
        scratch_shapes=[pltpu.VMEM((tm, d), F32)],
        compiler_params=_params("parallel", "arbitrary"),
        name="ffn_layer",
    )(x, w_in, w_in, w_out, row2(ln_g), row2(ln_b))


def _router_gate(x, wr, br):
    logits = jnp.dot(x, wr, preferred_element_type=F32, precision=lax.Precision.HIGHEST) + br
    lane = lax.broadcasted_iota(jnp.int32, logits.shape, 1)
    m1 = jnp.max(logits, axis=-1, keepdims=True)
    i1 = jnp.min(jnp.where(logits == m1, lane, N_EXPERTS), axis=-1, keepdims=True)
    rest = jnp.where(lane == i1, -jnp.inf, logits)
    m2 = jnp.max(rest, axis=-1, keepdims=True)
    i2 = jnp.min(jnp.where(rest == m2, lane, N_EXPERTS), axis=-1, keepdims=True)
    e2 = jnp.exp(m2 - m1)
    den = 1.0 + e2
    return jnp.where(lane == i1, 1.0 / den, 0.0) + jnp.where(lane == i2, e2 / den, 0.0)


def _moe_kernel(x_ref, wr_ref, br_ref, wa_ref, wu_ref, wo_ref, lng_ref, lnb_ref, o_ref,
                gate_ref, acc_ref, tot_ref):
    e = pl.program_id(1)
    j = pl.program_id(2)
    nj = pl.num_programs(2)

    @pl.when((e == 0) & (j == 0))
    def _():
        gate_ref[...] = _router_gate(x_ref[...], wr_ref[...], br_ref[...])
        tot_ref[...] = jnp.zeros_like(tot_ref)

    xb = x_ref[...].astype(BF16)
    a = jnp.dot(xb, wa_ref[0], preferred_element_type=F32)
    u = jnp.dot(xb, wu_ref[0], preferred_element_type=F32)
    part = jnp.dot((_silu(a) * u).astype(BF16), wo_ref[0], preferred_element_type=F32)

    @pl.when(j == 0)
    def _():
        acc_ref[...] = part

    @pl.when(j > 0)
    def _():
        acc_ref[...] += part

    @pl.when(j == nj - 1)
    def _():
        gate = gate_ref[...]
        lane = lax.broadcasted_iota(jnp.int32, gate.shape, 1)
        ge = jnp.sum(jnp.where(lane == e, gate, 0.0), axis=-1, keepdims=True)
        tot_ref[...] += ge * acc_ref[...]

    @pl.when((j == nj - 1) & (e == pl.num_programs(1) - 1))
    def _():
        o_ref[...] = _layer_norm(ALPHA * x_ref[...] + tot_ref[...], lng_ref[...], lnb_ref[...])


def moe_layer_dense(x, w_router, b_router, w_in, w_out, ln_g, ln_b, *, tm, tf):
    n, d = x.shape
    ne, d_ff = w_out.shape[0], w_out.shape[1]
    nf = d_ff // tf
    assert n % tm == 0 and d_ff % tf == 0
    w_in = w_in.astype(BF16)
    w_out = w_out.astype(BF16)
    row2 = lambda a: a.reshape(1, -1).astype(F32)
    c2 = lambda i, e, j: (0, 0)
    return pl.pallas_call(
        _moe_kernel,
        grid=(n // tm, ne, nf),
        in_specs=[
            pl.BlockSpec((tm, d), lambda i, e, j: (i, 0)),
            pl.BlockSpec((d, ne), c2),
            pl.BlockSpec((1, ne), c2),
            pl.BlockSpec((1, d, tf), lambda i, e, j: (e, 0, j)),
            pl.BlockSpec((1, d, tf), lambda i, e, j: (e, 0, j + nf)),
            pl.BlockSpec((1, tf, d), lambda i, e, j: (e, j, 0)),
            pl.BlockSpec((1, d), c2),
            pl.BlockSpec((1, d), c2),
        ],
        out_specs=pl.BlockSpec((tm, d), lambda i, e, j: (i, 0)),
        out_shape=jax.ShapeDtypeStruct((n, d), F32),
        scratch_shapes=[pltpu.VMEM((tm, ne), F32), pltpu.VMEM((tm, d), F32), pltpu.VMEM((tm, d), F32)],
        compiler_params=_params("parallel", "arbitrary", "arbitrary"),
        name="moe_layer",
    )(x, w_router.astype(F32), row2(b_router), w_in, w_in, w_out, row2(ln_g), row2(ln_b))


def _proj_kernel(x_ref, w_ref, o_ref):
    o_ref[...] = jnp.dot(x_ref[...].astype(BF16), w_ref[...], preferred_element_type=F32)


def proj(x, w, *, tm):
    n, d = x.shape
    m = w.shape[1]
    return pl.pallas_call(
        _proj_kernel,
        grid=(n // tm,),
        in_specs=[pl.BlockSpec((tm, d), lambda i: (i, 0)), pl.BlockSpec((d, m), lambda i: (0, 0))],
        out_specs=pl.BlockSpec((tm, m), lambda i: (i, 0)),
        out_shape=jax.ShapeDtypeStruct((n, m), F32),
        compiler_params=_params("parallel"),
        name="proj",
    )(x, w.astype(BF16))


def _proj_ln_kernel(o_ref, x_ref, w_ref, lng_ref, lnb_ref, h_ref):
    y = jnp.dot(o_ref[...].astype(BF16), w_ref[...], preferred_element_type=F32)
    h_ref[...] = _layer_norm(ALPHA * x_ref[...] + y, lng_ref[...], lnb_ref[...])


def proj_ln(o, x, w, ln_g, ln_b, *, tm):
    n, d = x.shape
    row2 = lambda a: a.reshape(1, -1).astype(F32)
    c2 = lambda i: (0, 0)
    return pl.pallas_call(
        _proj_ln_kernel,
        grid=(n // tm,),
        in_specs=[pl.BlockSpec((tm, o.shape[1]), lambda i: (i, 0)), pl.BlockSpec((tm, d), lambda i: (i, 0)),
                  pl.BlockSpec(w.shape, c2), pl.BlockSpec((1, d), c2), pl.BlockSpec((1, d), c2)],
        out_specs=pl.BlockSpec((tm, d), lambda i: (i, 0)),
        out_shape=jax.ShapeDtypeStruct((n, d), F32),
        compiler_params=_params("parallel"),
        name="proj_ln",
    )(o, x, w.astype(BF16), row2(ln_g), row2(ln_b))


N_KV_COLS = 6 * B_KV * HEAD_DIM
KV_SET = 2 * B_KV * HEAD_DIM
N_Q_COLS = B_HEADS * HEAD_DIM
GATE_LANES = 128


def _nsa_proj_kernel(x_ref, w_ref, cmp_ref, sel_ref, win_ref, q_ref, ks_ref, vs_ref, kw_ref, vw_ref, g_ref):
    z = jnp.dot(x_ref[0].astype(BF16), w_ref[...], preferred_element_type=F32)
    cmp_ref[0] = z[:, 0:KV_SET]
    sel_ref[0] = z[:, KV_SET:2 * KV_SET]
    win_ref[0] = z[:, 2 * KV_SET:3 * KV_SET]
    half = B_KV * HEAD_DIM
    for g in range(B_KV):
        lo = KV_SET + g * HEAD_DIM
        ks_ref[0, g] = z[:, lo:lo + HEAD_DIM].astype(BF16)
        vs_ref[0, g] = z[:, lo + half:lo + half + HEAD_DIM].astype(BF16)
        lo = 2 * KV_SET + g * HEAD_DIM
        kw_ref[0, g] = z[:, lo:lo + HEAD_DIM].astype(BF16)
        vw_ref[0, g] = z[:, lo + half:lo + half + HEAD_DIM].astype(BF16)
    for h in range(B_HEADS):
        lo = N_KV_COLS + h * HEAD_DIM
        q_ref[0, h] = (z[:, lo:lo + HEAD_DIM] * SCALE).astype(BF16)
    g_ref[0] = _sigmoid(z[:, N_KV_COLS + N_Q_COLS:N_KV_COLS + N_Q_COLS + GATE_LANES])


def nsa_proj(h, kv_w, w_qg, *, tm):
    bsz, t, d = h.shape
    w = jnp.concatenate([kv_w, w_qg], axis=1)
    w = jnp.pad(w, ((0, 0), (0, N_KV_COLS + N_Q_COLS + GATE_LANES - w.shape[1]))).astype(BF16)
    row = lambda: pl.BlockSpec((1, tm, KV_SET), lambda b, i: (b, i, 0))
    hm = lambda nh: pl.BlockSpec((1, nh, tm, HEAD_DIM), lambda b, i: (b, 0, i, 0))
    sds = jax.ShapeDtypeStruct
    return pl.pallas_call(
        _nsa_proj_kernel,
        grid=(bsz, t // tm),
        in_specs=[pl.BlockSpec((1, tm, d), lambda b, i: (b, i, 0)), pl.BlockSpec(w.shape, lambda b, i: (0, 0))],
        out_specs=[row(), row(), row(), hm(B_HEADS), hm(B_KV), hm(B_KV), hm(B_KV), hm(B_KV),
                   pl.BlockSpec((1, tm, GATE_LANES), lambda b, i: (b, i, 0))],
        out_shape=[sds((bsz, t, KV_SET), F32)] * 3 + [sds((bsz, B_HEADS, t, HEAD_DIM), BF16)]
        + [sds((bsz, B_KV, t, HEAD_DIM), BF16)] * 4 + [sds((bsz, t, GATE_LANES), F32)],
        compiler_params=_params("parallel", "parallel"),
        name="nsa_proj",
    )(h, w)


CHUNK_COLS = CMP_STRIDE * KV_SET
PAIR = 2 * HEAD_DIM


def _compress_kernel(*refs, n_in, head_major, paged):
    if paged:
        refs = refs[1:]
    x_refs = refs[:n_in]
    wp_ref, u_ref, b1_ref, w2_ref, kc_ref, vc_ref, carry_ref = refs[n_in:]
    i = pl.program_id(1)

    @pl.when(i == 0)
    def _():
        carry_ref[...] = jnp.zeros_like(carry_ref)

    if n_in == 1:
        x = x_refs[0][0]
    else:
        x = jnp.concatenate([r[0] for r in x_refs], axis=0)
    ch = x.shape[0]
    row0 = lax.broadcasted_iota(jnp.int32, (ch, PAIR), 0) == 0
    for k in range(2):
        wp = wp_ref[k]
        pbm = jnp.dot(u_ref[k], wp, preferred_element_type=F32)
        pb = pbm[0:1, 0:PAIR] + pbm[1:2, PAIR:2 * PAIR] + b1_ref[k]
        for gp in range(B_KV // 2):
            base = k * B_KV * HEAD_DIM + gp * PAIR
            lhs = jnp.concatenate([x[:, s * KV_SET + base:s * KV_SET + base + PAIR] for s in range(CMP_STRIDE)],
                                  axis=1).astype(BF16)
            r = jnp.dot(lhs, wp, preferred_element_type=F32)
            first, second = r[:, 0:PAIR], r[:, PAIR:2 * PAIR]
            slot = k * (B_KV // 2) + gp
            prev = carry_ref[slot]
            shifted = jnp.where(row0, prev[7:8, :], pltpu.roll(first, 1, axis=0))
            carry_ref[slot] = first[ch - 8:ch, :]
            hid = jax.nn.gelu(shifted + second + pb)
            out = jnp.dot(hid.astype(BF16), w2_ref[k], preferred_element_type=F32).astype(BF16)
            dst = kc_ref if k == 0 else vc_ref
            if head_major:
                for g2 in range(2):
                    dst[0, 2 * gp + g2] = out[:, g2 * HEAD_DIM:(g2 + 1) * HEAD_DIM]
            else:
                dst[0, :, gp * PAIR:(gp + 1) * PAIR] = out


def _compress_weights(cmp_pe, cmp_w1, cmp_b1, cmp_w2):
    eye2 = jnp.eye(2, dtype=F32)
    w1 = cmp_w1.reshape(2, CMP_STRIDE, 2, HEAD_DIM, HEAD_DIM)
    wp = jnp.einsum('fskdh,ab->ksadfbh', w1, eye2).reshape(2, CMP_STRIDE * PAIR, 2 * PAIR)
    pe = cmp_pe.reshape(2, CMP_STRIDE, 2, HEAD_DIM)
    u = jnp.broadcast_to(pe.transpose(2, 0, 1, 3)[:, :, :, None, :], (2, 2, CMP_STRIDE, 2, HEAD_DIM))
    u = jnp.pad(u.reshape(2, 2, CMP_STRIDE * PAIR), ((0, 0), (0, 6), (0, 0)))
    b1 = jnp.tile(cmp_b1, (1, 2)).reshape(2, 1, PAIR)
    w2 = jnp.einsum('khd,ab->kahbd', cmp_w2, eye2).reshape(2, PAIR, PAIR)
    return wp.astype(BF16), u.astype(BF16), b1.astype(F32), w2.astype(BF16)


def compress(x, page_table, cmp_pe, cmp_w1, cmp_b1, cmp_w2, *, ch, head_major):
    wp, u, b1, w2 = _compress_weights(cmp_pe, cmp_w1, cmp_b1, cmp_w2)
    paged = page_table is not None
    if paged:
        bsz, n_pages = page_table.shape
        cpp = x.shape[1]
        n_in = ch // cpp
        n_chunks = n_pages * cpp
        x_specs = [pl.BlockSpec((1, cpp, CHUNK_COLS), functools.partial(
            lambda b, i, pt, j: (pt[b, i * n_in + j], 0, 0), j=j)) for j in range(n_in)]
        cm = lambda f: (lambda b, i, pt: f(b, i))
    else:
        bsz, n_chunks, _ = x.shape
        n_in = 1
        x_specs = [pl.BlockSpec((1, ch, CHUNK_COLS), lambda b, i: (b, i, 0))]
        cm = lambda f: f
    c3 = cm(lambda b, i: (0, 0, 0))
    if head_major:
        o_spec = pl.BlockSpec((1, B_KV, ch, HEAD_DIM), cm(lambda b, i: (b, 0, i, 0)))
        o_shape = jax.ShapeDtypeStruct((bsz, B_KV, n_chunks, HEAD_DIM), BF16)
    else:
        o_spec = pl.BlockSpec((1, ch, B_KV * HEAD_DIM), cm(lambda b, i: (b, i, 0)))
        o_shape = jax.ShapeDtypeStruct((bsz, n_chunks, B_KV * HEAD_DIM), BF16)
    in_specs = x_specs + [pl.BlockSpec(wp.shape, c3), pl.BlockSpec(u.shape, c3),
                          pl.BlockSpec(b1.shape, c3), pl.BlockSpec(w2.shape, c3)]
    grid = (bsz, n_chunks // ch)
    scratch = [pltpu.VMEM((2 * (B_KV // 2), 8, PAIR), F32)]
    kern = functools.partial(_compress_kernel, n_in=n_in, head_major=head_major, paged=paged)
    if paged:
        grid_spec = pltpu.PrefetchScalarGridSpec(num_scalar_prefetch=1, grid=grid, in_specs=in_specs,
                                                 out_specs=[o_spec, o_spec], scratch_shapes=scratch)
        args = (page_table,) + (x,) * n_in
    else:
        grid_spec = pl.GridSpec(grid=grid, in_specs=in_specs, out_specs=[o_spec, o_spec], scratch_shapes=scratch)
        args = (x,)
    return pl.pallas_call(
        kern, grid_spec=grid_spec, out_shape=[o_shape, o_shape],
        compiler_params=_params("parallel", "arbitrary"), name="compress",
    )(*args, wp, u, b1, w2)


def _select_blocks(imp, blk, cur, n_sel, axis):
    valid = blk <= cur
    forced = (blk == 0) | (valid & (blk > cur - N_LOCAL))
    score = jnp.where(forced, FORCE, jnp.where(valid, imp, -FORCE))
    score = jnp.where(blk < n_sel, score, -jnp.inf)
    out = jnp.full(score.shape, NEG_BIG, F32)
    big = jnp.int32(2 ** 30)
    for _ in range(min(TOP_N, n_sel)):
        m = jnp.max(score, axis=axis, keepdims=True)
        first = jnp.min(jnp.where(score == m, blk, big), axis=axis, keepdims=True)
        pick = blk == first
        out = jnp.where(pick, 0.0, out)
        score = jnp.where(pick, -jnp.inf, score)
    return jnp.where(valid, out, NEG_BIG)


def _overlap(c1, j, n_sel):
    c0 = (c1 - 1) * CMP_STRIDE
    j0 = j * SEL_BLOCK
    return ((c1 >= 1) & (j < n_sel) & (c0 <= j0 + SEL_BLOCK - 1) & (c0 + CMP_BLOCK - 1 >= j0)).astype(F32)


def _softmax_tile(s, m, l, acc, v):
    m_new = jnp.maximum(m, jnp.max(s, axis=-1, keepdims=True))
    alpha = jnp.exp(m - m_new)
    p = jnp.exp(s - m_new)
    l = alpha * l + jnp.sum(p, axis=-1, keepdims=True)
    acc = alpha * acc + jnp.dot(p.astype(BF16), v, preferred_element_type=F32)
    return m_new, l, acc


def _prompt_attn_kernel(q_ref, kc_ref, vc_ref, ks_ref, vs_ref, kw_ref, vw_ref, g_ref, o_ref, *, tq, tk, n_sel):
    g = pl.program_id(2)
    t0 = pl.program_id(1) * tq
    rows = B_GROUP * tq
    q = q_ref[0].reshape(rows, HEAD_DIM)
    tpos = t0 + lax.broadcasted_iota(jnp.int32, (tq, 1), 0)
    nch = kc_ref.shape[2]

    c1 = lax.broadcasted_iota(jnp.int32, (tq, nch), 1)
    cmask = (c1 >= 1) & (c1 * CMP_STRIDE + (CMP_BLOCK - CMP_STRIDE - 1) <= tpos)
    s = _dot_nt(q, kc_ref[0, 0]).reshape(B_GROUP, tq, nch)
    s = jnp.where(cmask[None], s, NEG_BIG)
    m = jnp.max(s, axis=-1, keepdims=True)
    e = jnp.where(cmask[None], jnp.exp(s - m), 0.0)
    l = jnp.sum(e, axis=-1, keepdims=True)
    p = e / jnp.where(l == 0.0, 1.0, l)
    o_cmp = jnp.dot(p.reshape(rows, nch).astype(BF16), vc_ref[0, 0], preferred_element_type=F32)

    nblk = 128
    ov = _overlap(lax.broadcasted_iota(jnp.int32, (nch, nblk), 0),
                  lax.broadcasted_iota(jnp.int32, (nch, nblk), 1), n_sel)
    imp = jnp.dot(jnp.sum(p, axis=0), ov, preferred_element_type=F32, precision=lax.Precision.HIGHEST)
    blk = lax.broadcasted_iota(jnp.int32, (tq, nblk), 1)
    bias = _select_blocks(imp, blk, tpos // SEL_BLOCK, n_sel, axis=1).astype(BF16)

    def branch(k_ref, v_ref, lo, hi, sel):
        def body(kt, carry):
            m, l, acc = carry
            k0 = pl.multiple_of(kt * tk, tk)
            kpos = k0 + lax.broadcasted_iota(jnp.int32, (tq, tk), 1)
            if sel:
                blk_of_key = (k0 + lax.broadcasted_iota(jnp.int32, (nblk, tk), 1)) // SEL_BLOCK
                onehot = (blk_of_key == lax.broadcasted_iota(jnp.int32, (nblk, tk), 0)).astype(BF16)
                mb = jnp.dot(bias, onehot, preferred_element_type=F32)
                mb = jnp.where(kpos <= tpos, mb, NEG_BIG)
            else:
                mb = jnp.where((kpos <= tpos) & (kpos >= tpos - WINDOW), 0.0, NEG_BIG)
            s = _dot_nt(q, k_ref[0, 0, pl.ds(k0, tk), :]).reshape(B_GROUP, tq, tk) + mb[None]
            return _softmax_tile(s.reshape(rows, tk), m, l, acc, v_ref[0, 0, pl.ds(k0, tk), :])

        init = (jnp.full((rows, 1), NEG_BIG, F32), jnp.zeros((rows, 1), F32), jnp.zeros((rows, HEAD_DIM), F32))
        m, l, acc = lax.fori_loop(lo, hi, body, init)
        return acc / l

    hi = (t0 + tq - 1) // tk + 1
    o_sel = branch(ks_ref, vs_ref, 0, hi, True)
    o_win = branch(kw_ref, vw_ref, jnp.maximum(t0 - WINDOW, 0) // tk, hi, False)

    gates = g_ref[0]
    lane = lax.broadcasted_iota(jnp.int32, gates.shape, 1)
    for r in range(B_GROUP):
        col = g * (3 * B_GROUP) + 3 * r
        gv = [jnp.sum(jnp.where(lane == col + br, gates, 0.0), axis=-1, keepdims=True) for br in range(3)]
        sl = slice(r * tq, (r + 1) * tq)
        o = gv[0] * o_cmp[sl] + gv[1] * o_sel[sl] + gv[2] * o_win[sl]
        o_ref[0, :, r * HEAD_DIM:(r + 1) * HEAD_DIM] = o.astype(o_ref.dtype)


def nsa_prompt_attn(q, kc, vc, ks, vs, kw, vw, gates, *, tq, tk):
    bsz, _, t, _ = q.shape
    nch = kc.shape[2]
    n_sel = -(-t // SEL_BLOCK)
    assert n_sel <= 128 and t % tq == 0 and t % tk == 0
    seq = lambda n: pl.BlockSpec((1, 1, n, HEAD_DIM), lambda b, i, g: (b, g, 0, 0))
    kern = functools.partial(_prompt_attn_kernel, tq=tq, tk=tk, n_sel=n_sel)
    return pl.pallas_call(
        kern,
        grid=(bsz, t // tq, B_KV),
        in_specs=[pl.BlockSpec((1, B_GROUP, tq, HEAD_DIM), lambda b, i, g: (b, g, i, 0)),
                  seq(nch), seq(nch), seq(t), seq(t), seq(t), seq(t),
                  pl.BlockSpec((1, tq, GATE_LANES), lambda b, i, g: (b, i, 0))],
        out_specs=pl.BlockSpec((1, tq, B_GROUP * HEAD_DIM), lambda b, i, g: (b, i, g)),
        out_shape=jax.ShapeDtypeStruct((bsz, t, N_Q_COLS), BF16),
        compiler_params=_params("parallel", "parallel", "arbitrary"),
        name="nsa_prompt_attn",
    )(q, kc, vc, ks, vs, kw, vw, gates)


KV_HALF = B_KV * HEAD_DIM
Q_COLS = B_GROUP * B_KV * 8


def _sample_attn_kernel(*refs, n_pages_step, past, t_new, n_keep):
    pt_ref, qbd_ref, gl_ref, kc_ref, vc_ref, swin_ref, nsel_ref, nwin_ref = refs[:8]
    page_refs = refs[8:8 + n_pages_step]
    out_ref, bias_ref, m_ref, l_ref, acc_ref, oth_ref = refs[8 + n_pages_step:]
    del pt_ref
    i = pl.program_id(1)
    qbd = qbd_ref[0]
    ncol = qbd.shape[1]
    col = lax.broadcasted_iota(jnp.int32, (1, ncol), 1)
    tcol = col % t_new
    qpos = past + tcol
    sig = _sigmoid(gl_ref[0])
    n_sel = -(-(past + t_new) // SEL_BLOCK)
    blk_step = n_pages_step * PAGE_SIZE // SEL_BLOCK
    pad_rows = 8

    def scores(k):
        return jnp.dot(k.astype(BF16), qbd, preferred_element_type=F32) * SCALE

    @pl.when(i == 0)
    def _():
        nch = kc_ref.shape[1]
        c1 = lax.broadcasted_iota(jnp.int32, (nch, 1), 0)
        cmask = (c1 >= 1) & (c1 * CMP_STRIDE + (CMP_BLOCK - CMP_STRIDE - 1) <= qpos)
        s = jnp.where(cmask, scores(kc_ref[0]), NEG_BIG)
        m = jnp.max(s, axis=0, keepdims=True)
        e = jnp.where(cmask, jnp.exp(s - m), 0.0)
        l = jnp.sum(e, axis=0, keepdims=True)
        p = e / jnp.where(l == 0.0, 1.0, l)
        o_cmp = _dot_tn(vc_ref[0], p)
        nb = bias_ref.shape[0]
        ov_t = _overlap(lax.broadcasted_iota(jnp.int32, (nb, nch), 1),
                        lax.broadcasted_iota(jnp.int32, (nb, nch), 0), n_sel)
        imp = jnp.dot(ov_t, p, preferred_element_type=F32, precision=lax.Precision.HIGHEST)
        per = ncol // B_GROUP
        imp = imp + pltpu.roll(imp, per, axis=1) + pltpu.roll(imp, 2 * per, axis=1) + pltpu.roll(imp, 3 * per, axis=1)
        blk = lax.broadcasted_iota(jnp.int32, (nb, ncol), 0)
        bias_ref[...] = _select_blocks(imp, blk, qpos // SEL_BLOCK, n_sel, axis=0)
        kv_w = jnp.concatenate([swin_ref[0], nwin_ref[0], jnp.zeros((pad_rows, 2 * KV_HALF), F32)], axis=0)
        nw = kv_w.shape[0]
        wi = lax.broadcasted_iota(jnp.int32, (nw, 1), 0)
        wpos = past - n_keep + wi
        wmask = (wi < n_keep + t_new) & (wpos <= qpos) & (wpos >= qpos - WINDOW) & (wpos >= 0)
        s = jnp.where(wmask, scores(kv_w[:, :KV_HALF]), NEG_BIG)
        m = jnp.max(s, axis=0, keepdims=True)
        e = jnp.where(wmask, jnp.exp(s - m), 0.0)
        l = jnp.sum(e, axis=0, keepdims=True)
        o_win = _dot_tn(kv_w[:, KV_HALF:], e / jnp.where(l == 0.0, 1.0, l))
        oth_ref[...] = sig[0:1] * o_cmp + sig[2:3] * o_win
        m_ref[...] = jnp.full(m_ref.shape, NEG_BIG, F32)
        l_ref[...] = jnp.zeros(l_ref.shape, F32)
        acc_ref[...] = jnp.zeros(acc_ref.shape, F32)

    def update(s, v):
        m_new = jnp.maximum(m_ref[...], jnp.max(s, axis=0, keepdims=True))
        alpha = jnp.exp(m_ref[...] - m_new)
        p = jnp.exp(s - m_new)
        l_ref[...] = alpha * l_ref[...] + jnp.sum(p, axis=0, keepdims=True)
        acc_ref[...] = alpha * acc_ref[...] + _dot_tn(v, p)
        m_ref[...] = m_new

    bias = bias_ref[pl.ds(pl.multiple_of(i * blk_step, blk_step), blk_step), :]
    per_page = PAGE_SIZE // SEL_BLOCK
    s_parts, v_parts = [], []
    for p in range(n_pages_step):
        page = page_refs[p][0]
        mb = jnp.concatenate([jnp.broadcast_to(bias[per_page * p + j:per_page * p + j + 1, :], (SEL_BLOCK, ncol))
                              for j in range(per_page)], axis=0)
        s_parts.append(scores(page[:, :KV_HALF]) + mb)
        v_parts.append(page[:, KV_HALF:].astype(BF16))
    update(jnp.concatenate(s_parts, axis=0), jnp.concatenate(v_parts, axis=0))

    @pl.when(i == pl.num_programs(1) - 1)
    def _():
        kv_n = jnp.concatenate([nsel_ref[0], jnp.zeros((pad_rows, 2 * KV_HALF), F32)], axis=0)
        u = lax.broadcasted_iota(jnp.int32, (kv_n.shape[0], 1), 0)
        nb_new = past // SEL_BLOCK
        s = scores(kv_n[:, :KV_HALF]) + bias_ref[nb_new:nb_new + 1, :]
        update(jnp.where((u < t_new) & (past + u <= qpos), s, NEG_BIG), kv_n[:, KV_HALF:])
        o = oth_ref[...] + sig[1:2] * acc_ref[...] / l_ref[...]
        g_row = lax.broadcasted_iota(jnp.int32, o.shape, 0) // HEAD_DIM
        g_col = (lax.broadcasted_iota(jnp.int32, o.shape, 1) // t_new) % B_KV
        o = jnp.where(g_row == g_col, o, 0.0)
        out_ref[0] = o[0:HEAD_DIM] + o[HEAD_DIM:2 * HEAD_DIM] + o[2 * HEAD_DIM:3 * HEAD_DIM] + o[3 * HEAD_DIM:]


def nsa_sample_attn(q, gate_logits, kc, vc, cache_kv_sel, state_kv_win, kv_sel_new, kv_win_new, page_table,
                    *, n_pages_step):
    bsz, t, _ = q.shape
    n_pages = page_table.shape[1]
    past = n_pages * PAGE_SIZE
    n_keep = state_kv_win.shape[1]
    assert t * B_GROUP * B_KV == Q_COLS and past % SEL_BLOCK == 0 and n_pages % n_pages_step == 0
    qt = q.reshape(bsz, t, B_KV, B_GROUP, HEAD_DIM).transpose(0, 2, 4, 3, 1)
    qbd = jnp.einsum('bgdrt,gh->bgdrht', qt, jnp.eye(B_KV, dtype=F32)).reshape(bsz, KV_HALF, Q_COLS).astype(BF16)
    gl = gate_logits.reshape(bsz, t, B_KV, B_GROUP, 3).transpose(0, 4, 3, 2, 1).reshape(bsz, 3, Q_COLS)
    gl = jnp.pad(gl, ((0, 0), (0, 5), (0, 0)))
    n_blk = past // SEL_BLOCK + 8
    per_b = lambda shape: pl.BlockSpec((1,) + shape, lambda b, i, pt: (b, 0, 0))
    page_specs = [pl.BlockSpec((1, PAGE_SIZE, 2 * KV_HALF), functools.partial(
        lambda b, i, pt, j: (pt[b, i * n_pages_step + j], 0, 0), j=j)) for j in range(n_pages_step)]
    kern = functools.partial(_sample_attn_kernel, n_pages_step=n_pages_step, past=past, t_new=t, n_keep=n_keep)
    out = pl.pallas_call(
        kern,
        grid_spec=pltpu.PrefetchScalarGridSpec(
            num_scalar_prefetch=1,
            grid=(bsz, n_pages // n_pages_step),
            in_specs=[per_b((KV_HALF, Q_COLS)), per_b((8, Q_COLS)), per_b(kc.shape[1:]), per_b(vc.shape[1:]),
                      per_b((n_keep, 2 * KV_HALF)), per_b((t, 2 * KV_HALF)), per_b((t, 2 * KV_HALF))] + page_specs,
            out_specs=pl.BlockSpec((1, HEAD_DIM, Q_COLS), lambda b, i, pt: (b, 0, 0)),
            scratch_shapes=[pltpu.VMEM((n_blk, Q_COLS), F32), pltpu.VMEM((1, Q_COLS), F32),
                            pltpu.VMEM((1, Q_COLS), F32), pltpu.VMEM((KV_HALF, Q_COLS), F32),
                            pltpu.VMEM((KV_HALF, Q_COLS), F32)]),
        out_shape=jax.ShapeDtypeStruct((bsz, HEAD_DIM, Q_COLS), F32),
        compiler_params=_params("parallel", "arbitrary"),
        name="nsa_sample_attn",
    )(page_table, qbd, gl, kc, vc, state_kv_win, kv_sel_new, kv_win_new, *([cache_kv_sel] * n_pages_step))
    return out.reshape(bsz, HEAD_DIM, B_GROUP, B_KV, t).transpose(0, 4, 3, 2, 1).reshape(bsz, t, N_Q_COLS)


def _tile(n, pref):
    return pref if n % pref == 0 else n


def _run_group(x, s0, is_prompt, caches, a_w_in, a_lb_logits, a_norm_g, a_w_out, b_w_qg, b_w_out, kv_w,
               cmp_pe, cmp_w1, cmp_b1, cmp_w2, ffn_w_in, ffn_w_out, moe_w_router, moe_b_router,
               moe_w_in, moe_w_out, ln_g, ln_b):
    bsz, t, d = x.shape
    n = bsz * t
    if is_prompt:
        h, s_out = hgrn_layer(x, s0, a_w_in[0], a_lb_logits, a_norm_g[0], a_w_out[0], ln_g[0, 0], ln_b[0, 0],
                              layer=0, seg=A_CHUNK, n_seg=4, carry=True)
    else:
        h, s_out = hgrn_layer(x, s0, a_w_in[0], a_lb_logits, a_norm_g[0], a_w_out[0], ln_g[0, 0], ln_b[0, 0],
                              layer=0, seg=t, n_seg=8, carry=False)
    h = h.reshape(n, d)
    h = ffn_layer(h, ffn_w_in[0], ffn_w_out[0], ln_g[0, 1], ln_b[0, 1], tm=_tile(n, 1024), tf=256)
    nq = B_HEADS * HEAD_DIM
    if is_prompt:
        kv_cmp, kv_sel, kv_win, qh, ksh, vsh, kwh, vwh, gts = nsa_proj(h.reshape(bsz, t, d), kv_w, b_w_qg[0], tm=512)
        n_ch = t // CMP_STRIDE
        kc, vc = compress(kv_cmp.reshape(bsz, n_ch, CHUNK_COLS), None, cmp_pe, cmp_w1, cmp_b1, cmp_w2,
                          ch=n_ch, head_major=True)
        o = nsa_prompt_attn(qh, kc, vc, ksh, vsh, kwh, vwh, gts, tq=128, tk=256)
        h = proj_ln(o.reshape(n, nq), h, b_w_out[0], ln_g[1, 0], ln_b[1, 0], tm=_tile(n, 512))
        h = moe_layer_dense(h, moe_w_router[0], moe_b_router[0], moe_w_in[0], moe_w_out[0], ln_g[1, 1], ln_b[1, 1],
                            tm=_tile(n, 1024), tf=512)
        kvshape = (bsz, t, 2, B_KV, HEAD_DIM)
        return (h.reshape(bsz, t, d), s_out[None], kv_cmp.reshape(kvshape), kv_sel.reshape(kvshape),
                kv_win.reshape(kvshape)[:, t - min(WINDOW, t):])
    cache_kv_cmp, cache_kv_sel, state_kv_win, page_table = caches
    n_pool = cache_kv_cmp.shape[0]
    n_keep = state_kv_win.shape[1]
    w_cat = jnp.concatenate([kv_w, b_w_qg[0]], axis=1)
    w_cat = jnp.pad(w_cat, ((0, 0), (0, (-w_cat.shape[1]) % 128)))
    z = proj(h, w_cat, tm=_tile(n, 512))
    kv = z[:, :N_KV_COLS].reshape(bsz, t, 3, KV_SET)
    q = z[:, N_KV_COLS:N_KV_COLS + nq].reshape(bsz, t, nq)
    gate_logits = z[:, N_KV_COLS + nq:N_KV_COLS + nq + 3 * B_HEADS].reshape(bsz, t, 3 * B_HEADS)
    assert (page_table.shape[1] * PAGE_SIZE + t) // CMP_STRIDE == page_table.shape[1] * PAGE_SIZE // CMP_STRIDE
    kc, vc = compress(cache_kv_cmp.reshape(n_pool, PAGE_SIZE // CMP_STRIDE, CHUNK_COLS), page_table,
                      cmp_pe, cmp_w1, cmp_b1, cmp_w2, ch=128, head_major=False)
    o = nsa_sample_attn(q, gate_logits, kc, vc, cache_kv_sel.reshape(n_pool, PAGE_SIZE, KV_SET),
                        state_kv_win.reshape(bsz, n_keep, KV_SET), kv[:, :, 1], kv[:, :, 2], page_table,
                        n_pages_step=16)
    h = proj_ln(o.reshape(n, nq), h, b_w_out[0], ln_g[1, 0], ln_b[1, 0], tm=_tile(n, 512))
    h = moe_layer_dense(h, moe_w_router[0], moe_b_router[0], moe_w_in[0], moe_w_out[0], ln_g[1, 1], ln_b[1, 1],
                        tm=_tile(n, 1024), tf=512)
    kvshape = (bsz, t, 2, B_KV, HEAD_DIM)
    win_all = jnp.concatenate([state_kv_win, kv[:, :, 2].reshape(kvshape)], axis=1)
    return (h.reshape(bsz, t, d), s_out[None], kv[:, :, 0].reshape(kvshape), kv[:, :, 1].reshape(kvshape),
            win_all[:, -n_keep:])


def kernel(x_prompt, x_sample, state_hgrn, cache_kv_cmp, cache_kv_sel, state_kv_win, page_table,
           a_w_in, a_lb_logits, a_norm_g, a_w_out, b_w_qg, b_w_out, kv_w,
           cmp_pe, cmp_w1, cmp_b1, cmp_w2, ffn_w_in, ffn_w_out,
           moe_w_router, moe_b_router, moe_w_in, moe_w_out, ln_g, ln_b):
    weights = (a_w_in, a_lb_logits, a_norm_g, a_w_out, b_w_qg, b_w_out, kv_w, cmp_pe, cmp_w1, cmp_b1, cmp_w2,
               ffn_w_in, ffn_w_out, moe_w_router, moe_b_router, moe_w_in, moe_w_out, ln_g, ln_b)
    hgrn0 = jnp.zeros((x_prompt.shape[0], A_HEADS, A_DK, A_DV), F32)
    y_p, hg_p, cmp_p, sel_p, win_p = _run_group(x_prompt, hgrn0, True, None, *weights)
    y_s, hg_s, cmp_s, sel_s, win_s = _run_group(
        x_sample, state_hgrn[0], False, (cache_kv_cmp, cache_kv_sel, state_kv_win, page_table), *weights)
    return (y_p, y_s, hg_p, cmp_p, sel_p, win_p, hg_s, cmp_s, sel_s, win_s)
```

```python
import functools

import jax
import jax.numpy as jnp
from jax import lax
from jax.experimental import pallas as pl
from jax.experimental.pallas import tpu as pltpu

F32 = jnp.float32
BF16 = jnp.bfloat16

D_MODEL = 1024
DEPTH = 2
ALPHA = (2.0 * DEPTH) ** 0.25
LN_EPS = 1e-5
RMS_EPS = 1e-6
NEG_BIG = -1e30
FORCE = 1e6
PAGE_SIZE = 128

A_DK = 128
A_HEADS = D_MODEL // A_DK
A_DV = D_MODEL // A_HEADS
A_WIDTH = A_HEADS * A_DK
A_CHUNK = 64

B_HEADS = 16
B_KV = 4
B_GROUP = B_HEADS // B_KV
HEAD_DIM = D_MODEL // B_HEADS
SCALE = HEAD_DIM ** -0.5
CMP_STRIDE = 16
CMP_BLOCK = 2 * CMP_STRIDE
SEL_BLOCK = 64
TOP_N = 8
N_LOCAL = 2
WINDOW = 512

D_FF = 256 * ((8 * D_MODEL // 3 + 255) // 256)
N_EXPERTS = 8
MOE_TOP_K = 2
D_FF_E = 7 * D_MODEL // 2

VMEM_LIMIT_BYTES = 56 * 1024 * 1024


def _params(*sem):
    return pltpu.CompilerParams(dimension_semantics=sem, vmem_limit_bytes=VMEM_LIMIT_BYTES)


def _silu(x):
    return x * (1.0 / (1.0 + jnp.exp(-x)))


def _sigmoid(x):
    return 1.0 / (1.0 + jnp.exp(-x))


def _layer_norm(x, g, b):
    xc = x - jnp.mean(x, -1, keepdims=True)
    var = jnp.mean(xc * xc, -1, keepdims=True)
    return xc * lax.rsqrt(var + LN_EPS) * g + b


def _dot(a, b):
    return jnp.dot(a.astype(BF16), b.astype(BF16), preferred_element_type=F32)


def _dot_nt(a, b):
    return lax.dot_general(a.astype(BF16), b.astype(BF16), (((1,), (1,)), ((), ())),
                           preferred_element_type=F32)


def _dot_tn(a, b):
    return lax.dot_general(a.astype(BF16), b.astype(BF16), (((0,), (0,)), ((), ())),
                           preferred_element_type=F32)


def _hgrn_kernel(x_ref, s0_ref, win_ref, lbl_ref, ng_ref, wout_ref, lng_ref, lnb_ref,
                 h_ref, sout_ref, st_ref, *, layer, seg, n_seg, carry):
    rows = seg * n_seg
    x = x_ref[...].reshape(rows, D_MODEL)
    z = jnp.dot(x.astype(BF16), win_ref[...], preferred_element_type=F32)
    zq = z[:, 0 * A_WIDTH:1 * A_WIDTH]
    zf = z[:, 1 * A_WIDTH:2 * A_WIDTH]
    v = z[:, 2 * A_WIDTH:3 * A_WIDTH]
    zg = z[:, 3 * A_WIDTH:4 * A_WIDTH]

    lbl = lbl_ref[...]
    e = jnp.exp(lbl - jnp.max(lbl, axis=0, keepdims=True))
    lb = jnp.sum(e[:layer + 1], axis=0, keepdims=True) / jnp.sum(e, axis=0, keepdims=True)

    q = _silu(zq)
    f = lb + (1.0 - lb) * _sigmoid(zf)
    logf = jnp.log(f)
    k = 1.0 - f

    r_i = lax.broadcasted_iota(jnp.int32, (seg, seg), 0)
    c_i = lax.broadcasted_iota(jnp.int32, (seg, seg), 1)
    causal = c_i <= r_i
    tri = causal.astype(F32)

    if carry:
        @pl.when(pl.program_id(1) == 0)
        def _():
            for h in range(A_HEADS):
                st_ref[h] = s0_ref[0, h].T

    o_parts = []
    for s in range(n_seg):
        sl = slice(s * seg, (s + 1) * seg)
        g = jnp.dot(tri, logf[sl], preferred_element_type=F32, precision=lax.Precision.HIGHEST)
        glast = g[seg - 1:seg, :]
        qg = q[sl] * jnp.exp(g)
        kg = k[sl] * jnp.exp(-g)
        kd = k[sl] * jnp.exp(glast - g)
        eg = jnp.exp(glast)
        vs = v[sl]
        heads = []
        for h in range(A_HEADS):
            cl = slice(h * A_DK, (h + 1) * A_DK)
            if carry:
                st = st_ref[h]
            else:
                st = s0_ref[s, h].T
            att = jnp.where(causal, _dot_nt(qg[:, cl], kg[:, cl]), 0.0)
            o = _dot_nt(qg[:, cl], st) + _dot(att, vs[:, cl])
            st_new = eg[:, cl] * st + _dot_tn(vs[:, cl], kd[:, cl])
            if carry:
                st_ref[h] = st_new
            else:
                sout_ref[s, h] = st_new.T
            o = o * lax.rsqrt(jnp.mean(o * o, -1, keepdims=True) + RMS_EPS)
            heads.append(o)
        o_parts.append(jnp.concatenate(heads, axis=1))
    o = o_parts[0] if n_seg == 1 else jnp.concatenate(o_parts, axis=0)
    o = o * ng_ref[...] * _silu(zg)
    y = jnp.dot(o.astype(BF16), wout_ref[...], preferred_element_type=F32)
    hh = _layer_norm(ALPHA * x + y, lng_ref[...], lnb_ref[...])
    h_ref[...] = hh.reshape(h_ref.shape)

    if carry:
        @pl.when(pl.program_id(1) == pl.num_programs(1) - 1)
        def _():
            for h in range(A_HEADS):
                sout_ref[0, h] = st_ref[h].T


def hgrn_layer(x, s0, w_in, lb_logits, norm_g, w_out, ln_g, ln_b, *, layer, seg, n_seg, carry):
    bsz, t, _ = x.shape
    row2 = lambda a: a.reshape(1, -1).astype(F32)
    w_in = w_in.astype(BF16)
    w_out = w_out.astype(BF16)
    const = lambda *_: (0, 0)
    if carry:
        tile = seg * n_seg
        grid = (bsz, t // tile)
        x_spec = pl.BlockSpec((1, tile, D_MODEL), lambda b, c: (b, c, 0))
        s_spec = pl.BlockSpec((1, A_HEADS, A_DK, A_DV), lambda b, c: (b, 0, 0, 0))
        sem = ("parallel", "arbitrary")
    else:
        assert t == seg
        grid = (bsz // n_seg, 1)
        x_spec = pl.BlockSpec((n_seg, seg, D_MODEL), lambda b, c: (b, 0, 0))
        s_spec = pl.BlockSpec((n_seg, A_HEADS, A_DK, A_DV), lambda b, c: (b, 0, 0, 0))
        sem = ("parallel", "arbitrary")
    kern = functools.partial(_hgrn_kernel, layer=layer, seg=seg, n_seg=n_seg, carry=carry)
    return pl.pallas_call(
        kern,
        grid=grid,
        in_specs=[
            x_spec, s_spec,
            pl.BlockSpec((D_MODEL, 4 * A_WIDTH), const),
            pl.BlockSpec(lb_logits.shape, const),
            pl.BlockSpec((1, A_WIDTH), const),
            pl.BlockSpec((A_WIDTH, D_MODEL), const),
            pl.BlockSpec((1, D_MODEL), const),
            pl.BlockSpec((1, D_MODEL), const),
        ],
        out_specs=[x_spec, s_spec],
        out_shape=[jax.ShapeDtypeStruct(x.shape, F32), jax.ShapeDtypeStruct(s0.shape, F32)],
        scratch_shapes=[pltpu.VMEM((A_HEADS, A_DV, A_DK), F32)],
        compiler_params=_params(*sem),
        name="hgrn_layer",
    )(x, s0, w_in, lb_logits.astype(F32), row2(norm_g), w_out, row2(ln_g), row2(ln_b))


def _ffn_kernel(x_ref, wa_ref, wu_ref, wo_ref, lng_ref, lnb_ref, o_ref, acc_ref):
    j = pl.program_id(1)
    xb = x_ref[...].astype(BF16)
    a = jnp.dot(xb, wa_ref[...], preferred_element_type=F32)
    u = jnp.dot(xb, wu_ref[...], preferred_element_type=F32)
    part = jnp.dot((_silu(a) * u).astype(BF16), wo_ref[...], preferred_element_type=F32)

    @pl.when(j == 0)
    def _():
        acc_ref[...] = part

    @pl.when(j > 0)
    def _():
        acc_ref[...] += part

    @pl.when(j == pl.num_programs(1) - 1)
    def _():
        o_ref[...] = _layer_norm(ALPHA * x_ref[...] + acc_ref[...], lng_ref[...], lnb_ref[...])


def ffn_layer(x, w_in, w_out, ln_g, ln_b, *, tm, tf):
    n, d = x.shape
    d_ff = w_out.shape[0]
    nf = d_ff // tf
    assert n % tm == 0 and d_ff % tf == 0
    w_in = w_in.astype(BF16)
    w_out = w_out.astype(BF16)
    row2 = lambda a: a.reshape(1, -1).astype(F32)
    return pl.pallas_call(
        _ffn_kernel,
        grid=(n // tm, nf),
        in_specs=[
            pl.BlockSpec((tm, d), lambda i, j: (i, 0)),
            pl.BlockSpec((d, tf), lambda i, j: (0, j)),
            pl.BlockSpec((d, tf), lambda i, j: (0, j + nf)),
            pl.BlockSpec((tf, d), lambda i, j: (j, 0)),
            pl.BlockSpec((1, d), lambda i, j: (0, 0)),
            pl.BlockSpec((1, d), lambda i, j: (0, 0)),
        ],
        out_specs=pl.BlockSpec((tm, d), lambda i, j: (i, 0)),
        out_shape=jax.ShapeDtypeStruct((n, d), F32),
        scratch_shapes=[pltpu.VMEM((tm, d), F32)],
        compiler_params=_params("parallel", "arbitrary"),
        name="ffn_layer",
    )(x, w_in, w_in, w_out, row2(ln_g), row2(ln_b))


def _router_gate(x, wr, br):
    logits = jnp.dot(x, wr, preferred_element_type=F32, precision=lax.Precision.HIGHEST) + br
    lane = lax.broadcasted_iota(jnp.int32, logits.shape, 1)
    m1 = jnp.max(logits, axis=-1, keepdims=True)
    i1 = jnp.min(jnp.where(logits == m1, lane, N_EXPERTS), axis=-1, keepdims=True)
    rest = jnp.where(lane == i1, -jnp.inf, logits)
    m2 = jnp.max(rest, axis=-1, keepdims=True)
    i2 = jnp.min(jnp.where(rest == m2, lane, N_EXPERTS), axis=-1, keepdims=True)
    e2 = jnp.exp(m2 - m1)
    den = 1.0 + e2
    return jnp.where(lane == i1, 1.0 / den, 0.0) + jnp.where(lane == i2, e2 / den, 0.0)


def _moe_kernel(x_ref, wr_ref, br_ref, wa_ref, wu_ref, wo_ref, lng_ref, lnb_ref, o_ref,
                gate_ref, acc_ref, tot_ref):
    e = pl.program_id(1)
    j = pl.program_id(2)
    nj = pl.num_programs(2)

    @pl.when((e == 0) & (j == 0))
    def _():
        gate_ref[...] = _router_gate(x_ref[...], wr_ref[...], br_ref[...])
        tot_ref[...] = jnp.zeros_like(tot_ref)

    xb = x_ref[...].astype(BF16)
    a = jnp.dot(xb, wa_ref[0], preferred_element_type=F32)
    u = jnp.dot(xb, wu_ref[0], preferred_element_type=F32)
    part = jnp.dot((_silu(a) * u).astype(BF16), wo_ref[0], preferred_element_type=F32)

    @pl.when(j == 0)
    def _():
        acc_ref[...] = part

    @pl.when(j > 0)
    def _():
        acc_ref[...] += part

    @pl.when(j == nj - 1)
    def _():
        gate = gate_ref[...]
        lane = lax.broadcasted_iota(jnp.int32, gate.shape, 1)
        ge = jnp.sum(jnp.where(lane == e, gate, 0.0), axis=-1, keepdims=True)
        tot_ref[...] += ge * acc_ref[...]

    @pl.when((j == nj - 1) & (e == pl.num_programs(1) - 1))
    def _():
        o_ref[...] = _layer_norm(ALPHA * x_ref[...] + tot_ref[...], lng_ref[...], lnb_ref[...])


def moe_layer_dense(x, w_router, b_router, w_in, w_out, ln_g, ln_b, *, tm, tf):
    n, d = x.shape
    ne, d_ff = w_out.shape[0], w_out.shape[1]
    nf = d_ff // tf
    assert n % tm == 0 and d_ff % tf == 0
    w_in = w_in.astype(BF16)
    w_out = w_out.astype(BF16)
    row2 = lambda a: a.reshape(1, -1).astype(F32)
    c2 = lambda i, e, j: (0, 0)
    return pl.pallas_call(
        _moe_kernel,
        grid=(n // tm, ne, nf),
        in_specs=[
            pl.BlockSpec((tm, d), lambda i, e, j: (i, 0)),
            pl.BlockSpec((d, ne), c2),
            pl.BlockSpec((1, ne), c2),
            pl.BlockSpec((1, d, tf), lambda i, e, j: (e, 0, j)),
            pl.BlockSpec((1, d, tf), lambda i, e, j: (e, 0, j + nf)),
            pl.BlockSpec((1, tf, d), lambda i, e, j: (e, j, 0)),
            pl.BlockSpec((1, d), c2),
            pl.BlockSpec((1, d), c2),
        ],
        out_specs=pl.BlockSpec((tm, d), lambda i, e, j: (i, 0)),
        out_shape=jax.ShapeDtypeStruct((n, d), F32),
        scratch_shapes=[pltpu.VMEM((tm, ne), F32), pltpu.VMEM((tm, d), F32), pltpu.VMEM((tm, d), F32)],
        compiler_params=_params("parallel", "arbitrary", "arbitrary"),
        name="moe_layer",
    )(x, w_router.astype(F32), row2(b_router), w_in, w_in, w_out, row2(ln_g), row2(ln_b))


MOE_RB = 256
MOE_ST = 256


def _router_kernel(x_ref, wrt_ref, br_ref, gate_ref, xb_ref):
    x = x_ref[...]
    logits = lax.dot_general(wrt_ref[...], x, (((1,), (1,)), ((), ())), preferred_element_type=F32,
                             precision=lax.Precision.HIGHEST) + br_ref[...]
    eidx = lax.broadcasted_iota(jnp.int32, logits.shape, 0)
    m1 = jnp.max(logits, axis=0, keepdims=True)
    i1 = jnp.min(jnp.where(logits == m1, eidx, N_EXPERTS), axis=0, keepdims=True)
    rest = jnp.where(eidx == i1, -jnp.inf, logits)
    m2 = jnp.max(rest, axis=0, keepdims=True)
    i2 = jnp.min(jnp.where(rest == m2, eidx, N_EXPERTS), axis=0, keepdims=True)
    e2 = jnp.exp(m2 - m1)
    den = 1.0 + e2
    gate_ref[...] = jnp.where(eidx == i1, 1.0 / den, 0.0) + jnp.where(eidx == i2, e2 / den, 0.0)
    xb_ref[...] = x.astype(BF16)


def moe_router(x, w_router, b_router, *, tm):
    n, d = x.shape
    return pl.pallas_call(
        _router_kernel,
        grid=(n // tm,),
        in_specs=[pl.BlockSpec((tm, d), lambda i: (i, 0)), pl.BlockSpec((N_EXPERTS, d), lambda i: (0, 0)),
                  pl.BlockSpec((N_EXPERTS, 1), lambda i: (0, 0))],
        out_specs=[pl.BlockSpec((N_EXPERTS, tm), lambda i: (0, i)), pl.BlockSpec((tm, d), lambda i: (i, 0))],
        out_shape=[jax.ShapeDtypeStruct((N_EXPERTS, n), F32), jax.ShapeDtypeStruct((n, d), BF16)],
        compiler_params=_params("parallel"),
        name="moe_router",
    )(x, w_router.T.astype(F32), b_router.reshape(N_EXPERTS, 1).astype(F32))


def _moe_routed_kernel(cum_ref, xb_ref, gate_ref, rank_ref, wa_ref, wu_ref, wo_ref, y_ref, xs_ref, acc_ref,
                       *, n_sub):
    i = pl.program_id(0)
    e = pl.program_id(1)
    j = pl.program_id(2)
    nj = pl.num_programs(2)
    base = (i * N_EXPERTS + e) * (n_sub + 1)
    count = cum_ref[base + n_sub]
    n_blk = (count + MOE_RB - 1) // MOE_RB
    p_iota = lax.broadcasted_iota(jnp.int32, (MOE_RB, MOE_ST), 0)

    def overlaps(b, c):
        return (cum_ref[base + c] < (b + 1) * MOE_RB) & (cum_ref[base + c + 1] > b * MOE_RB)

    def one_hot(b, c):
        c0 = pl.multiple_of(c * MOE_ST, MOE_ST)
        rank = rank_ref[pl.ds(e, 1), pl.ds(c0, MOE_ST)]
        gate = gate_ref[pl.ds(e, 1), pl.ds(c0, MOE_ST)]
        hit = (rank == b * MOE_RB + p_iota) & (gate > 0.0)
        return hit, gate, c0

    @pl.when((e == 0) & (j == 0))
    def _():
        y_ref[...] = jnp.zeros_like(y_ref)

    @pl.when(j == 0)
    def _():
        def gather_block(b, _):
            xs_ref[b] = jnp.zeros((MOE_RB, D_MODEL), BF16)

            def sub(c, _):
                @pl.when(overlaps(b, c))
                def _():
                    hit, _, c0 = one_hot(b, c)
                    part = jnp.dot(jnp.where(hit, 1.0, 0.0).astype(BF16), xb_ref[pl.ds(c0, MOE_ST), :],
                                   preferred_element_type=F32)
                    xs_ref[b] = xs_ref[b] + part.astype(BF16)
                return 0

            lax.fori_loop(0, n_sub, sub, 0)
            return 0

        lax.fori_loop(0, n_blk, gather_block, 0)

    def ffn_block(b, _):
        xs = xs_ref[b]
        a = jnp.dot(xs, wa_ref[0], preferred_element_type=F32)
        u = jnp.dot(xs, wu_ref[0], preferred_element_type=F32)
        part = jnp.dot((_silu(a) * u).astype(BF16), wo_ref[0], preferred_element_type=F32)

        @pl.when(j == 0)
        def _():
            acc_ref[b] = part

        @pl.when(j > 0)
        def _():
            acc_ref[b] = acc_ref[b] + part
        return 0

    lax.fori_loop(0, n_blk, ffn_block, 0)

    @pl.when(j == nj - 1)
    def _():
        def scatter_block(b, _):
            def sub(c, _):
                @pl.when(overlaps(b, c))
                def _():
                    hit, gate, c0 = one_hot(b, c)
                    g_rows = jnp.sum(jnp.where(hit, gate, 0.0), axis=1, keepdims=True)
                    out = acc_ref[b] * g_rows
                    hi = out.astype(BF16)
                    lo = (out - hi.astype(F32)).astype(BF16)
                    oh = jnp.where(hit, 1.0, 0.0).astype(BF16)
                    y_ref[pl.ds(c0, MOE_ST), :] += _dot_tn(oh, hi) + _dot_tn(oh, lo)
                return 0

            lax.fori_loop(0, n_sub, sub, 0)
            return 0

        lax.fori_loop(0, n_blk, scatter_block, 0)


def moe_routed(xb, gate_t, w_in, w_out, *, tm, tf):
    n, d = xb.shape
    ne, d_ff = w_out.shape[0], w_out.shape[1]
    nf = d_ff // tf
    n_tiles, n_sub = n // tm, tm // MOE_ST
    assert n % tm == 0 and tm % MOE_ST == 0 and d_ff % tf == 0 and tm % MOE_RB == 0
    mask = (gate_t > 0.0).astype(jnp.int32).reshape(ne, n_tiles, tm)
    rank = (jnp.cumsum(mask, axis=-1) - mask).reshape(ne, n)
    cnt = mask.reshape(ne, n_tiles, n_sub, MOE_ST).sum(-1)
    cum = jnp.concatenate([jnp.zeros((ne, n_tiles, 1), jnp.int32), jnp.cumsum(cnt, axis=-1)], axis=-1)
    cum = cum.transpose(1, 0, 2).reshape(-1).astype(jnp.int32)
    w_in = w_in.astype(BF16)
    w_out = w_out.astype(BF16)
    kern = functools.partial(_moe_routed_kernel, n_sub=n_sub)
    return pl.pallas_call(
        kern,
        grid_spec=pltpu.PrefetchScalarGridSpec(
            num_scalar_prefetch=1,
            grid=(n_tiles, ne, nf),
            in_specs=[
                pl.BlockSpec((tm, d), lambda i, e, j, c: (i, 0)),
                pl.BlockSpec((ne, tm), lambda i, e, j, c: (0, i)),
                pl.BlockSpec((ne, tm), lambda i, e, j, c: (0, i)),
                pl.BlockSpec((1, d, tf), lambda i, e, j, c: (e, 0, j)),
                pl.BlockSpec((1, d, tf), lambda i, e, j, c: (e, 0, j + nf)),
                pl.BlockSpec((1, tf, d), lambda i, e, j, c: (e, j, 0)),
            ],
            out_specs=pl.BlockSpec((tm, d), lambda i, e, j, c: (i, 0)),
            scratch_shapes=[pltpu.VMEM((tm // MOE_RB, MOE_RB, d), BF16), pltpu.VMEM((tm // MOE_RB, MOE_RB, d), F32)]),
        out_shape=jax.ShapeDtypeStruct((n, d), F32),
        compiler_params=_params("parallel", "arbitrary", "arbitrary"),
        name="moe_routed",
    )(cum, xb, gate_t, rank, w_in, w_in, w_out)


def _add_ln_kernel(x_ref, y_ref, lng_ref, lnb_ref, o_ref):
    o_ref[...] = _layer_norm(ALPHA * x_ref[...] + y_ref[...], lng_ref[...], lnb_ref[...])


def add_ln(x, y, ln_g, ln_b, *, tm):
    n, d = x.shape
    row2 = lambda a: a.reshape(1, -1).astype(F32)
    blk = pl.BlockSpec((tm, d), lambda i: (i, 0))
    c2 = pl.BlockSpec((1, d), lambda i: (0, 0))
    return pl.pallas_call(
        _add_ln_kernel, grid=(n // tm,), in_specs=[blk, blk, c2, c2], out_specs=blk,
        out_shape=jax.ShapeDtypeStruct((n, d), F32), compiler_params=_params("parallel"), name="add_ln",
    )(x, y, row2(ln_g), row2(ln_b))


def moe_layer(x, w_router, b_router, w_in, w_out, ln_g, ln_b, *, tm, tf):
    n = x.shape[0]
    gate_t, xb = moe_router(x, w_router, b_router, tm=_tile(n, 1024))
    y = moe_routed(xb, gate_t, w_in, w_out, tm=tm, tf=tf)
    return add_ln(x, y, ln_g, ln_b, tm=_tile(n, 1024))


def _proj_kernel(x_ref, w_ref, o_ref):
    o_ref[...] = jnp.dot(x_ref[...].astype(BF16), w_ref[...], preferred_element_type=F32)


def proj(x, w, *, tm):
    n, d = x.shape
    m = w.shape[1]
    return pl.pallas_call(
        _proj_kernel,
        grid=(n // tm,),
        in_specs=[pl.BlockSpec((tm, d), lambda i: (i, 0)), pl.BlockSpec((d, m), lambda i: (0, 0))],
        out_specs=pl.BlockSpec((tm, m), lambda i: (i, 0)),
        out_shape=jax.ShapeDtypeStruct((n, m), F32),
        compiler_params=_params("parallel"),
        name="proj",
    )(x, w.astype(BF16))


def _proj_ln_kernel(o_ref, x_ref, w_ref, lng_ref, lnb_ref, h_ref):
    y = jnp.dot(o_ref[...].astype(BF16), w_ref[...], preferred_element_type=F32)
    h_ref[...] = _layer_norm(ALPHA * x_ref[...] + y, lng_ref[...], lnb_ref[...])


def proj_ln(o, x, w, ln_g, ln_b, *, tm):
    n, d = x.shape
    row2 = lambda a: a.reshape(1, -1).astype(F32)
    c2 = lambda i: (0, 0)
    return pl.pallas_call(
        _proj_ln_kernel,
        grid=(n // tm,),
        in_specs=[pl.BlockSpec((tm, o.shape[1]), lambda i: (i, 0)), pl.BlockSpec((tm, d), lambda i: (i, 0)),
                  pl.BlockSpec(w.shape, c2), pl.BlockSpec((1, d), c2), pl.BlockSpec((1, d), c2)],
        out_specs=pl.BlockSpec((tm, d), lambda i: (i, 0)),
        out_shape=jax.ShapeDtypeStruct((n, d), F32),
        compiler_params=_params("parallel"),
        name="proj_ln",
    )(o, x, w.astype(BF16), row2(ln_g), row2(ln_b))


N_KV_COLS = 6 * B_KV * HEAD_DIM
KV_SET = 2 * B_KV * HEAD_DIM
N_Q_COLS = B_HEADS * HEAD_DIM
GATE_LANES = 128


def _nsa_proj_kernel(x_ref, w_ref, cmp_ref, sel_ref, win_ref, q_ref, ks_ref, vs_ref, kw_ref, vw_ref, g_ref):
    z = jnp.dot(x_ref[0].astype(BF16), w_ref[...], preferred_element_type=F32)
    cmp_ref[0] = z[:, 0:KV_SET]
    sel_ref[0] = z[:, KV_SET:2 * KV_SET]
    win_ref[0] = z[:, 2 * KV_SET:3 * KV_SET]
    half = B_KV * HEAD_DIM
    for g in range(B_KV):
        lo = KV_SET + g * HEAD_DIM
        ks_ref[0, g] = z[:, lo:lo + HEAD_DIM].astype(BF16)
        vs_ref[0, g] = z[:, lo + half:lo + half + HEAD_DIM].astype(BF16)
        lo = 2 * KV_SET + g * HEAD_DIM
        kw_ref[0, g] = z[:, lo:lo + HEAD_DIM].astype(BF16)
        vw_ref[0, g] = z[:, lo + half:lo + half + HEAD_DIM].astype(BF16)
    for h in range(B_HEADS):
        lo = N_KV_COLS + h * HEAD_DIM
        q_ref[0, h] = (z[:, lo:lo + HEAD_DIM] * SCALE).astype(BF16)
    g_ref[0] = _sigmoid(z[:, N_KV_COLS + N_Q_COLS:N_KV_COLS + N_Q_COLS + GATE_LANES])


def nsa_proj(h, kv_w, w_qg, *, tm):
    bsz, t, d = h.shape
    w = jnp.concatenate([kv_w, w_qg], axis=1)
    w = jnp.pad(w, ((0, 0), (0, N_KV_COLS + N_Q_COLS + GATE_LANES - w.shape[1]))).astype(BF16)
    row = lambda: pl.BlockSpec((1, tm, KV_SET), lambda b, i: (b, i, 0))
    hm = lambda nh: pl.BlockSpec((1, nh, tm, HEAD_DIM), lambda b, i: (b, 0, i, 0))
    sds = jax.ShapeDtypeStruct
    return pl.pallas_call(
        _nsa_proj_kernel,
        grid=(bsz, t // tm),
        in_specs=[pl.BlockSpec((1, tm, d), lambda b, i: (b, i, 0)), pl.BlockSpec(w.shape, lambda b, i: (0, 0))],
        out_specs=[row(), row(), row(), hm(B_HEADS), hm(B_KV), hm(B_KV), hm(B_KV), hm(B_KV),
                   pl.BlockSpec((1, tm, GATE_LANES), lambda b, i: (b, i, 0))],
        out_shape=[sds((bsz, t, KV_SET), F32)] * 3 + [sds((bsz, B_HEADS, t, HEAD_DIM), BF16)]
        + [sds((bsz, B_KV, t, HEAD_DIM), BF16)] * 4 + [sds((bsz, t, GATE_LANES), F32)],
        compiler_params=_params("parallel", "parallel"),
        name="nsa_proj",
    )(h, w)


PAIR = 2 * HEAD_DIM


def _compress_kernel(*refs, n_in, head_major, paged):
    if paged:
        refs = refs[1:]
    x_refs = refs[:n_in]
    wp_ref, u_ref, b1_ref, w2_ref, kc_ref, vc_ref, carry_ref = refs[n_in:]
    i = pl.program_id(1)

    @pl.when(i == 0)
    def _():
        carry_ref[...] = jnp.zeros_like(carry_ref)

    cpp = PAGE_SIZE // CMP_STRIDE
    pr = lax.broadcasted_iota(jnp.int32, (PAGE_SIZE, PAGE_SIZE), 0)
    pc = lax.broadcasted_iota(jnp.int32, (PAGE_SIZE, PAGE_SIZE), 1)
    perm = jnp.where(pc == CMP_STRIDE * (pr % cpp) + pr // cpp, 1.0, 0.0).astype(BF16)
    pages = []
    for r in x_refs:
        for p0 in range(0, r.shape[1], PAGE_SIZE):
            pages.append(jnp.dot(perm, r[0, p0:p0 + PAGE_SIZE, :].astype(BF16), preferred_element_type=F32))
    ch = len(pages) * cpp
    row0 = lax.broadcasted_iota(jnp.int32, (ch, PAIR), 0) == 0

    def chunk_rows(s, lo):
        parts = [pg[s * cpp:(s + 1) * cpp, lo:lo + PAIR] for pg in pages]
        return parts[0] if len(parts) == 1 else jnp.concatenate(parts, axis=0)
    for k in range(2):
        wp = wp_ref[k]
        pbm = jnp.dot(u_ref[k], wp, preferred_element_type=F32)
        pb = pbm[0:1, 0:PAIR] + pbm[1:2, PAIR:2 * PAIR] + b1_ref[k]
        for gp in range(B_KV // 2):
            base = k * B_KV * HEAD_DIM + gp * PAIR
            lhs = jnp.concatenate([chunk_rows(s, base) for s in range(CMP_STRIDE)], axis=1).astype(BF16)
            r = jnp.dot(lhs, wp, preferred_element_type=F32)
            first, second = r[:, 0:PAIR], r[:, PAIR:2 * PAIR]
            slot = k * (B_KV // 2) + gp
            prev = carry_ref[slot]
            shifted = jnp.where(row0, prev[7:8, :], pltpu.roll(first, 1, axis=0))
            carry_ref[slot] = first[ch - 8:ch, :]
            hid = jax.nn.gelu(shifted + second + pb)
            out = jnp.dot(hid.astype(BF16), w2_ref[k], preferred_element_type=F32).astype(BF16)
            dst = kc_ref if k == 0 else vc_ref
            if head_major:
                for g2 in range(2):
                    dst[0, 2 * gp + g2] = out[:, g2 * HEAD_DIM:(g2 + 1) * HEAD_DIM]
            else:
                dst[0, :, gp * PAIR:(gp + 1) * PAIR] = out


def _compress_weights(cmp_pe, cmp_w1, cmp_b1, cmp_w2):
    eye2 = jnp.eye(2, dtype=F32)
    w1 = cmp_w1.reshape(2, CMP_STRIDE, 2, HEAD_DIM, HEAD_DIM)
    wp = jnp.einsum('fskdh,ab->ksadfbh', w1, eye2).reshape(2, CMP_STRIDE * PAIR, 2 * PAIR)
    pe = cmp_pe.reshape(2, CMP_STRIDE, 2, HEAD_DIM)
    u = jnp.broadcast_to(pe.transpose(2, 0, 1, 3)[:, :, :, None, :], (2, 2, CMP_STRIDE, 2, HEAD_DIM))
    u = jnp.pad(u.reshape(2, 2, CMP_STRIDE * PAIR), ((0, 0), (0, 6), (0, 0)))
    b1 = jnp.tile(cmp_b1, (1, 2)).reshape(2, 1, PAIR)
    w2 = jnp.einsum('khd,ab->kahbd', cmp_w2, eye2).reshape(2, PAIR, PAIR)
    return wp.astype(BF16), u.astype(BF16), b1.astype(F32), w2.astype(BF16)


def compress(x, page_table, cmp_pe, cmp_w1, cmp_b1, cmp_w2, *, ch, head_major):
    wp, u, b1, w2 = _compress_weights(cmp_pe, cmp_w1, cmp_b1, cmp_w2)
    paged = page_table is not None
    if paged:
        bsz, n_pages = page_table.shape
        page = x.shape[1]
        n_in = ch * CMP_STRIDE // page
        n_chunks = n_pages * page // CMP_STRIDE
        x_specs = [pl.BlockSpec((1, page, KV_SET), functools.partial(
            lambda b, i, pt, j: (pt[b, i * n_in + j], 0, 0), j=j)) for j in range(n_in)]
        cm = lambda f: (lambda b, i, pt: f(b, i))
    else:
        bsz, t, _ = x.shape
        n_chunks = t // CMP_STRIDE
        n_in = 1
        x_specs = [pl.BlockSpec((1, ch * CMP_STRIDE, KV_SET), lambda b, i: (b, i, 0))]
        cm = lambda f: f
    c3 = cm(lambda b, i: (0, 0, 0))
    if head_major:
        o_spec = pl.BlockSpec((1, B_KV, ch, HEAD_DIM), cm(lambda b, i: (b, 0, i, 0)))
        o_shape = jax.ShapeDtypeStruct((bsz, B_KV, n_chunks, HEAD_DIM), BF16)
    else:
        o_spec = pl.BlockSpec((1, ch, B_KV * HEAD_DIM), cm(lambda b, i: (b, i, 0)))
        o_shape = jax.ShapeDtypeStruct((bsz, n_chunks, B_KV * HEAD_DIM), BF16)
    in_specs = x_specs + [pl.BlockSpec(wp.shape, c3), pl.BlockSpec(u.shape, c3),
                          pl.BlockSpec(b1.shape, c3), pl.BlockSpec(w2.shape, c3)]
    grid = (bsz, n_chunks // ch)
    scratch = [pltpu.VMEM((2 * (B_KV // 2), 8, PAIR), F32)]
    kern = functools.partial(_compress_kernel, n_in=n_in, head_major=head_major, paged=paged)
    if paged:
        grid_spec = pltpu.PrefetchScalarGridSpec(num_scalar_prefetch=1, grid=grid, in_specs=in_specs,
                                                 out_specs=[o_spec, o_spec], scratch_shapes=scratch)
        args = (page_table,) + (x,) * n_in
    else:
        grid_spec = pl.GridSpec(grid=grid, in_specs=in_specs, out_specs=[o_spec, o_spec], scratch_shapes=scratch)
        args = (x,)
    return pl.pallas_call(
        kern, grid_spec=grid_spec, out_shape=[o_shape, o_shape],
        compiler_params=_params("parallel", "arbitrary"), name="compress",
    )(*args, wp, u, b1, w2)


def _select_blocks(imp, blk, cur, n_sel, axis):
    valid = blk <= cur
    forced = (blk == 0) | (valid & (blk > cur - N_LOCAL))
    score = jnp.where(forced, FORCE, jnp.where(valid, imp, -FORCE))
    score = jnp.where(blk < n_sel, score, -jnp.inf)
    out = jnp.full(score.shape, NEG_BIG, F32)
    big = jnp.int32(2 ** 30)
    for _ in range(min(TOP_N, n_sel)):
        m = jnp.max(score, axis=axis, keepdims=True)
        first = jnp.min(jnp.where(score == m, blk, big), axis=axis, keepdims=True)
        pick = blk == first
        out = jnp.where(pick, 0.0, out)
        score = jnp.where(pick, -jnp.inf, score)
    return jnp.where(valid, out, NEG_BIG)


def _overlap(c1, j, n_sel):
    c0 = (c1 - 1) * CMP_STRIDE
    j0 = j * SEL_BLOCK
    return ((c1 >= 1) & (j < n_sel) & (c0 <= j0 + SEL_BLOCK - 1) & (c0 + CMP_BLOCK - 1 >= j0)).astype(F32)


def _rank_select(imp, blk, cur, n_sel):
    valid = blk <= cur
    forced = (blk == 0) | (valid & (blk > cur - N_LOCAL))
    score = jnp.where(forced, FORCE, jnp.where(valid, imp, -FORCE))
    rank = jnp.zeros(score.shape, F32)
    for jp in range(n_sel):
        row = score[jp:jp + 1, :]
        gt = jnp.where(row > score, 1.0, 0.0)
        ge = jnp.where(row >= score, 1.0, 0.0)
        rank = rank + jnp.where(blk > jp, ge, gt)
    keep = jnp.where(rank < TOP_N, 0.0, NEG_BIG)
    return jnp.where(valid, jnp.where(blk < n_sel, keep, NEG_BIG), NEG_BIG)


def _online_softmax_t(s, m, l, acc, v):
    m_new = jnp.maximum(m, jnp.max(s, axis=0, keepdims=True))
    alpha = jnp.exp(m - m_new)
    p = jnp.exp(s - m_new)
    l = alpha * l + jnp.sum(p, axis=0, keepdims=True)
    acc = alpha * acc + _dot_tn(v, p)
    return m_new, l, acc


def _prompt_attn_kernel(q_ref, kc_ref, vc_ref, ks_ref, vs_ref, kw_ref, vw_ref, g_ref, o_ref, gt_ref,
                        *, tq, tk, n_sel, nb):
    g = pl.program_id(2)
    t0 = pl.program_id(1) * tq
    cols = B_GROUP * tq
    q = q_ref[0].reshape(cols, HEAD_DIM)
    tpos = t0 + lax.broadcasted_iota(jnp.int32, (1, tq), 1)
    nch = kc_ref.shape[2]
    rep = lambda a: jnp.concatenate([a] * B_GROUP, axis=1)

    c1 = lax.broadcasted_iota(jnp.int32, (nch, 1), 0)
    cmask = rep((c1 >= 1) & (c1 * CMP_STRIDE + (CMP_BLOCK - CMP_STRIDE - 1) <= tpos))
    s = jnp.where(cmask, _dot_nt(kc_ref[0, 0], q), NEG_BIG)
    m = jnp.max(s, axis=0, keepdims=True)
    e = jnp.where(cmask, jnp.exp(s - m), 0.0)
    l = jnp.sum(e, axis=0, keepdims=True)
    p = e * (1.0 / jnp.where(l == 0.0, 1.0, l))
    o_cmp = _dot_tn(vc_ref[0, 0], p)

    p4 = p[:, 0:tq]
    for r in range(1, B_GROUP):
        p4 = p4 + p[:, r * tq:(r + 1) * tq]
    ov_t = _overlap(lax.broadcasted_iota(jnp.int32, (nb, nch), 1),
                    lax.broadcasted_iota(jnp.int32, (nb, nch), 0), n_sel)
    imp = jnp.dot(ov_t, p4, preferred_element_type=F32, precision=lax.Precision.HIGHEST)
    blk = lax.broadcasted_iota(jnp.int32, (nb, tq), 0)
    bias = _rank_select(imp, blk, tpos // SEL_BLOCK, n_sel).astype(BF16)

    def sel_scores(kt):
        k0 = pl.multiple_of(kt * tk, tk)
        kpos = k0 + lax.broadcasted_iota(jnp.int32, (tk, 1), 0)
        blk_of_key = (k0 + lax.broadcasted_iota(jnp.int32, (tk, nb), 0)) // SEL_BLOCK
        onehot = jnp.where(blk_of_key == lax.broadcasted_iota(jnp.int32, (tk, nb), 1), 1.0, 0.0).astype(BF16)
        mb = jnp.dot(onehot, bias, preferred_element_type=F32)
        mb = jnp.where(kpos <= tpos, mb, NEG_BIG)
        return _dot_nt(ks_ref[0, 0, pl.ds(k0, tk), :], q) + rep(mb), vs_ref[0, 0, pl.ds(k0, tk), :]

    def win_scores(kt):
        k0 = pl.multiple_of(kt * tk, tk)
        kpos = k0 + lax.broadcasted_iota(jnp.int32, (tk, 1), 0)
        mb = jnp.where((kpos <= tpos) & (kpos >= tpos - WINDOW), 0.0, NEG_BIG)
        return _dot_nt(kw_ref[0, 0, pl.ds(k0, tk), :], q) + rep(mb), vw_ref[0, 0, pl.ds(k0, tk), :]

    def sel_only(kt, carry):
        s, v = sel_scores(kt)
        return _online_softmax_t(s, *carry, v)

    def both(kt, carry):
        s, v = sel_scores(kt)
        sw, vw = win_scores(kt)
        return _online_softmax_t(s, *carry[:3], v) + _online_softmax_t(sw, *carry[3:], vw)

    init = (jnp.full((1, cols), NEG_BIG, F32), jnp.zeros((1, cols), F32), jnp.zeros((HEAD_DIM, cols), F32))
    hi = (t0 + tq - 1) // tk + 1
    lo_w = jnp.maximum(t0 - WINDOW, 0) // tk
    sel_state = lax.fori_loop(0, lo_w, sel_only, init)
    _, l_s, acc_s, _, l_w, acc_w = lax.fori_loop(lo_w, hi, both, sel_state + init)

    gt_ref[...] = g_ref[0].T
    gate = lambda br: jnp.concatenate(
        [gt_ref[pl.ds(g * (3 * B_GROUP) + 3 * r + br, 1), :] for r in range(B_GROUP)], axis=1)
    o = gate(0) * o_cmp + (gate(1) * (1.0 / l_s)) * acc_s + (gate(2) * (1.0 / l_w)) * acc_w
    for r in range(B_GROUP):
        o_ref[0, :, r * HEAD_DIM:(r + 1) * HEAD_DIM] = o[:, r * tq:(r + 1) * tq].T.astype(o_ref.dtype)


def nsa_prompt_attn(q, kc, vc, ks, vs, kw, vw, gates, *, tq, tk):
    bsz, _, t, _ = q.shape
    nch = kc.shape[2]
    n_sel = -(-t // SEL_BLOCK)
    nb = -(-n_sel // 16) * 16
    assert t % tq == 0 and t % tk == 0 and tq == GATE_LANES
    seq = lambda n: pl.BlockSpec((1, 1, n, HEAD_DIM), lambda b, i, g: (b, g, 0, 0))
    kern = functools.partial(_prompt_attn_kernel, tq=tq, tk=tk, n_sel=n_sel, nb=nb)
    return pl.pallas_call(
        kern,
        grid=(bsz, t // tq, B_KV),
        in_specs=[pl.BlockSpec((1, B_GROUP, tq, HEAD_DIM), lambda b, i, g: (b, g, i, 0)),
                  seq(nch), seq(nch), seq(t), seq(t), seq(t), seq(t),
                  pl.BlockSpec((1, tq, GATE_LANES), lambda b, i, g: (b, i, 0))],
        out_specs=pl.BlockSpec((1, tq, B_GROUP * HEAD_DIM), lambda b, i, g: (b, i, g)),
        out_shape=jax.ShapeDtypeStruct((bsz, t, N_Q_COLS), BF16),
        scratch_shapes=[pltpu.VMEM((GATE_LANES, tq), F32)],
        compiler_params=_params("parallel", "parallel", "arbitrary"),
        name="nsa_prompt_attn",
    )(q, kc, vc, ks, vs, kw, vw, gates)


KV_HALF = B_KV * HEAD_DIM
Q_COLS = B_GROUP * B_KV * 8


def _sample_attn_kernel(*refs, n_pages_step, past, t_new, n_keep):
    pt_ref, qbd_ref, gl_ref, kc_ref, vc_ref, swin_ref, nsel_ref, nwin_ref = refs[:8]
    page_refs = refs[8:8 + n_pages_step]
    out_ref, bias_ref, m_ref, l_ref, acc_ref, oth_ref = refs[8 + n_pages_step:]
    del pt_ref
    i = pl.program_id(1)
    qbd = qbd_ref[0]
    ncol = qbd.shape[1]
    col = lax.broadcasted_iota(jnp.int32, (1, ncol), 1)
    tcol = col % t_new
    qpos = past + tcol
    sig = _sigmoid(gl_ref[0])
    n_sel = -(-(past + t_new) // SEL_BLOCK)
    blk_step = n_pages_step * PAGE_SIZE // SEL_BLOCK
    pad_rows = 8

    def scores(k):
        return jnp.dot(k.astype(BF16), qbd, preferred_element_type=F32) * SCALE

    @pl.when(i == 0)
    def _():
        nch = kc_ref.shape[1]
        c1 = lax.broadcasted_iota(jnp.int32, (nch, 1), 0)
        cmask = (c1 >= 1) & (c1 * CMP_STRIDE + (CMP_BLOCK - CMP_STRIDE - 1) <= qpos)
        s = jnp.where(cmask, scores(kc_ref[0]), NEG_BIG)
        m = jnp.max(s, axis=0, keepdims=True)
        e = jnp.where(cmask, jnp.exp(s - m), 0.0)
        l = jnp.sum(e, axis=0, keepdims=True)
        p = e / jnp.where(l == 0.0, 1.0, l)
        o_cmp = _dot_tn(vc_ref[0], p)
        nb = bias_ref.shape[0]
        ov_t = _overlap(lax.broadcasted_iota(jnp.int32, (nb, nch), 1),
                        lax.broadcasted_iota(jnp.int32, (nb, nch), 0), n_sel)
        imp = jnp.dot(ov_t, p, preferred_element_type=F32, precision=lax.Precision.HIGHEST)
        per = ncol // B_GROUP
        imp = imp + pltpu.roll(imp, per, axis=1) + pltpu.roll(imp, 2 * per, axis=1) + pltpu.roll(imp, 3 * per, axis=1)
        blk = lax.broadcasted_iota(jnp.int32, (nb, ncol), 0)
        bias_ref[...] = _select_blocks(imp, blk, qpos // SEL_BLOCK, n_sel, axis=0)
        kv_w = jnp.concatenate([swin_ref[0], nwin_ref[0], jnp.zeros((pad_rows, 2 * KV_HALF), F32)], axis=0)
        nw = kv_w.shape[0]
        wi = lax.broadcasted_iota(jnp.int32, (nw, 1), 0)
        wpos = past - n_keep + wi
        wmask = (wi < n_keep + t_new) & (wpos <= qpos) & (wpos >= qpos - WINDOW) & (wpos >= 0)
        s = jnp.where(wmask, scores(kv_w[:, :KV_HALF]), NEG_BIG)
        m = jnp.max(s, axis=0, keepdims=True)
        e = jnp.where(wmask, jnp.exp(s - m), 0.0)
        l = jnp.sum(e, axis=0, keepdims=True)
        o_win = _dot_tn(kv_w[:, KV_HALF:], e / jnp.where(l == 0.0, 1.0, l))
        oth_ref[...] = sig[0:1] * o_cmp + sig[2:3] * o_win
        m_ref[...] = jnp.full(m_ref.shape, NEG_BIG, F32)
        l_ref[...] = jnp.zeros(l_ref.shape, F32)
        acc_ref[...] = jnp.zeros(acc_ref.shape, F32)

    def update(s, v):
        m_new = jnp.maximum(m_ref[...], jnp.max(s, axis=0, keepdims=True))
        alpha = jnp.exp(m_ref[...] - m_new)
        p = jnp.exp(s - m_new)
        l_ref[...] = alpha * l_ref[...] + jnp.sum(p, axis=0, keepdims=True)
        acc_ref[...] = alpha * acc_ref[...] + _dot_tn(v, p)
        m_ref[...] = m_new

    bias = bias_ref[pl.ds(pl.multiple_of(i * blk_step, blk_step), blk_step), :]
    per_page = PAGE_SIZE // SEL_BLOCK
    s_parts, v_parts = [], []
    for p in range(n_pages_step):
        page = page_refs[p][0]
        mb = jnp.concatenate([jnp.broadcast_to(bias[per_page * p + j:per_page * p + j + 1, :], (SEL_BLOCK, ncol))
                              for j in range(per_page)], axis=0)
        s_parts.append(scores(page[:, :KV_HALF]) + mb)
        v_parts.append(page[:, KV_HALF:].astype(BF16))
    update(jnp.concatenate(s_parts, axis=0), jnp.concatenate(v_parts, axis=0))

    @pl.when(i == pl.num_programs(1) - 1)
    def _():
        kv_n = jnp.concatenate([nsel_ref[0], jnp.zeros((pad_rows, 2 * KV_HALF), F32)], axis=0)
        u = lax.broadcasted_iota(jnp.int32, (kv_n.shape[0], 1), 0)
        nb_new = past // SEL_BLOCK
        s = scores(kv_n[:, :KV_HALF]) + bias_ref[nb_new:nb_new + 1, :]
        update(jnp.where((u < t_new) & (past + u <= qpos), s, NEG_BIG), kv_n[:, KV_HALF:])
        o = oth_ref[...] + sig[1:2] * acc_ref[...] / l_ref[...]
        g_row = lax.broadcasted_iota(jnp.int32, o.shape, 0) // HEAD_DIM
        g_col = (lax.broadcasted_iota(jnp.int32, o.shape, 1) // t_new) % B_KV
        o = jnp.where(g_row == g_col, o, 0.0)
        out_ref[0] = o[0:HEAD_DIM] + o[HEAD_DIM:2 * HEAD_DIM] + o[2 * HEAD_DIM:3 * HEAD_DIM] + o[3 * HEAD_DIM:]


def nsa_sample_attn(q, gate_logits, kc, vc, cache_kv_sel, state_kv_win, kv_sel_new, kv_win_new, page_table,
                    *, n_pages_step):
    bsz, t, _ = q.shape
    n_pages = page_table.shape[1]
    past = n_pages * PAGE_SIZE
    n_keep = state_kv_win.shape[1]
    assert t * B_GROUP * B_KV == Q_COLS and past % SEL_BLOCK == 0 and n_pages % n_pages_step == 0
    qt = q.reshape(bsz, t, B_KV, B_GROUP, HEAD_DIM).transpose(0, 2, 4, 3, 1)
    qbd = jnp.einsum('bgdrt,gh->bgdrht', qt, jnp.eye(B_KV, dtype=F32)).reshape(bsz, KV_HALF, Q_COLS).astype(BF16)
    gl = gate_logits.reshape(bsz, t, B_KV, B_GROUP, 3).transpose(0, 4, 3, 2, 1).reshape(bsz, 3, Q_COLS)
    gl = jnp.pad(gl, ((0, 0), (0, 5), (0, 0)))
    n_blk = past // SEL_BLOCK + 8
    per_b = lambda shape: pl.BlockSpec((1,) + shape, lambda b, i, pt: (b, 0, 0))
    page_specs = [pl.BlockSpec((1, PAGE_SIZE, 2 * KV_HALF), functools.partial(
        lambda b, i, pt, j: (pt[b, i * n_pages_step + j], 0, 0), j=j)) for j in range(n_pages_step)]
    kern = functools.partial(_sample_attn_kernel, n_pages_step=n_pages_step, past=past, t_new=t, n_keep=n_keep)
    out = pl.pallas_call(
        kern,
        grid_spec=pltpu.PrefetchScalarGridSpec(
            num_scalar_prefetch=1,
            grid=(bsz, n_pages // n_pages_step),
            in_specs=[per_b((KV_HALF, Q_COLS)), per_b((8, Q_COLS)), per_b(kc.shape[1:]), per_b(vc.shape[1:]),
                      per_b((n_keep, 2 * KV_HALF)), per_b((t, 2 * KV_HALF)), per_b((t, 2 * KV_HALF))] + page_specs,
            out_specs=pl.BlockSpec((1, HEAD_DIM, Q_COLS), lambda b, i, pt: (b, 0, 0)),
            scratch_shapes=[pltpu.VMEM((n_blk, Q_COLS), F32), pltpu.VMEM((1, Q_COLS), F32),
                            pltpu.VMEM((1, Q_COLS), F32), pltpu.VMEM((KV_HALF, Q_COLS), F32),
                            pltpu.VMEM((KV_HALF, Q_COLS), F32)]),
        out_shape=jax.ShapeDtypeStruct((bsz, HEAD_DIM, Q_COLS), F32),
        compiler_params=_params("parallel", "arbitrary"),
        name="nsa_sample_attn",
    )(page_table, qbd, gl, kc, vc, state_kv_win, kv_sel_new, kv_win_new, *([cache_kv_sel] * n_pages_step))
    return out.reshape(bsz, HEAD_DIM, B_GROUP, B_KV, t).transpose(0, 4, 3, 2, 1).reshape(bsz, t, N_Q_COLS)


def _tile(n, pref):
    return pref if n % pref == 0 else n


def _run_group(x, s0, is_prompt, caches, a_w_in, a_lb_logits, a_norm_g, a_w_out, b_w_qg, b_w_out, kv_w,
               cmp_pe, cmp_w1, cmp_b1, cmp_w2, ffn_w_in, ffn_w_out, moe_w_router, moe_b_router,
               moe_w_in, moe_w_out, ln_g, ln_b):
    bsz, t, d = x.shape
    n = bsz * t
    if is_prompt:
        h, s_out = hgrn_layer(x, s0, a_w_in[0], a_lb_logits, a_norm_g[0], a_w_out[0], ln_g[0, 0], ln_b[0, 0],
                              layer=0, seg=A_CHUNK, n_seg=4, carry=True)
    else:
        h, s_out = hgrn_layer(x, s0, a_w_in[0], a_lb_logits, a_norm_g[0], a_w_out[0], ln_g[0, 0], ln_b[0, 0],
                              layer=0, seg=t, n_seg=8, carry=False)
    h = h.reshape(n, d)
    h = ffn_layer(h, ffn_w_in[0], ffn_w_out[0], ln_g[0, 1], ln_b[0, 1], tm=_tile(n, 1024), tf=256)
    nq = B_HEADS * HEAD_DIM
    if is_prompt:
        kv_cmp, kv_sel, kv_win, qh, ksh, vsh, kwh, vwh, gts = nsa_proj(h.reshape(bsz, t, d), kv_w, b_w_qg[0], tm=512)
        n_ch = t // CMP_STRIDE
        kc, vc = compress(kv_cmp, None, cmp_pe, cmp_w1, cmp_b1, cmp_w2,
                          ch=n_ch, head_major=True)
        o = nsa_prompt_attn(qh, kc, vc, ksh, vsh, kwh, vwh, gts, tq=128, tk=256)
        h = proj_ln(o.reshape(n, nq), h, b_w_out[0], ln_g[1, 0], ln_b[1, 0], tm=_tile(n, 512))
        h = moe_layer(h, moe_w_router[0], moe_b_router[0], moe_w_in[0], moe_w_out[0], ln_g[1, 1], ln_b[1, 1],
                      tm=_tile(n, 2048), tf=896)
        kvshape = (bsz, t, 2, B_KV, HEAD_DIM)
        return (h.reshape(bsz, t, d), s_out[None], kv_cmp.reshape(kvshape), kv_sel.reshape(kvshape),
                kv_win.reshape(kvshape)[:, t - min(WINDOW, t):])
    cache_kv_cmp, cache_kv_sel, state_kv_win, page_table = caches
    n_pool = cache_kv_cmp.shape[0]
    n_keep = state_kv_win.shape[1]
    w_cat = jnp.concatenate([kv_w, b_w_qg[0]], axis=1)
    w_cat = jnp.pad(w_cat, ((0, 0), (0, (-w_cat.shape[1]) % 128)))
    z = proj(h, w_cat, tm=_tile(n, 512))
    kv = z[:, :N_KV_COLS].reshape(bsz, t, 3, KV_SET)
    q = z[:, N_KV_COLS:N_KV_COLS + nq].reshape(bsz, t, nq)
    gate_logits = z[:, N_KV_COLS + nq:N_KV_COLS + nq + 3 * B_HEADS].reshape(bsz, t, 3 * B_HEADS)
    assert (page_table.shape[1] * PAGE_SIZE + t) // CMP_STRIDE == page_table.shape[1] * PAGE_SIZE // CMP_STRIDE
    kc, vc = compress(cache_kv_cmp.reshape(n_pool, PAGE_SIZE, KV_SET), page_table,
                      cmp_pe, cmp_w1, cmp_b1, cmp_w2, ch=128, head_major=False)
    o = nsa_sample_attn(q, gate_logits, kc, vc, cache_kv_sel.reshape(n_pool, PAGE_SIZE, KV_SET),
                        state_kv_win.reshape(bsz, n_keep, KV_SET), kv[:, :, 1], kv[:, :, 2], page_table,
                        n_pages_step=16)
    h = proj_ln(o.reshape(n, nq), h, b_w_out[0], ln_g[1, 0], ln_b[1, 0], tm=_tile(n, 512))
    h = moe_layer(h, moe_w_router[0], moe_b_router[0], moe_w_in[0], moe_w_out[0], ln_g[1, 1], ln_b[1, 1],
                  tm=_tile(n, 2048), tf=896)
    kvshape = (bsz, t, 2, B_KV, HEAD_DIM)
    win_all = jnp.concatenate([state_kv_win, kv[:, :, 2].reshape(kvshape)], axis=1)
    return (h.reshape(bsz, t, d), s_out[None], kv[:, :, 0].reshape(kvshape), kv[:, :, 1].reshape(kvshape),
            win_all[:, -n_keep:])


def kernel(x_prompt, x_sample, state_hgrn, cache_kv_cmp, cache_kv_sel, state_kv_win, page_table,
           a_w_in, a_lb_logits, a_norm_g, a_w_out, b_w_qg, b_w_out, kv_w,
           cmp_pe, cmp_w1, cmp_b1, cmp_w2, ffn_w_in, ffn_w_out,
           moe_w_router, moe_b_router, moe_w_in, moe_w_out, ln_g, ln_b):
    weights = (a_w_in, a_lb_logits, a_norm_g, a_w_out, b_w_qg, b_w_out, kv_w, cmp_pe, cmp_w1, cmp_b1, cmp_w2,
               ffn_w_in, ffn_w_out, moe_w_router, moe_b_router, moe_w_in, moe_w_out, ln_g, ln_b)
    hgrn0 = jnp.zeros((x_prompt.shape[0], A_HEADS, A_DK, A_DV), F32)
    y_p, hg_p, cmp_p, sel_p, win_p = _run_group(x_prompt, hgrn0, True, None, *weights)
    y_s, hg_s, cmp_s, sel_s, win_s = _run_group(
        x_sample, state_hgrn[0], False, (cache_kv_cmp, cache_kv_sel, state_kv_win, page_table), *weights)
    return (y_p, y_s, hg_p, cmp_p, sel_p, win_p, hg_s, cmp_s, sel_s, win_s)
```

```python
import functools

import jax
import jax.numpy as jnp
from jax import lax
from jax.experimental import pallas as pl
from jax.experimental.pallas import tpu as pltpu

F32 = jnp.float32
BF16 = jnp.bfloat16

D_MODEL = 1024
DEPTH = 2
ALPHA = (2.0 * DEPTH) ** 0.25
LN_EPS = 1e-5
RMS_EPS = 1e-6
NEG_BIG = -1e30
FORCE = 1e6
PAGE_SIZE = 128

A_DK = 128
A_HEADS = D_MODEL // A_DK
A_DV = D_MODEL // A_HEADS
A_WIDTH = A_HEADS * A_DK
A_CHUNK = 64

B_HEADS = 16
B_KV = 4
B_GROUP = B_HEADS // B_KV
HEAD_DIM = D_MODEL // B_HEADS
SCALE = HEAD_DIM ** -0.5
CMP_STRIDE = 16
CMP_BLOCK = 2 * CMP_STRIDE
SEL_BLOCK = 64
TOP_N = 8
N_LOCAL = 2
WINDOW = 512

D_FF = 256 * ((8 * D_MODEL // 3 + 255) // 256)
N_EXPERTS = 8
MOE_TOP_K = 2
D_FF_E = 7 * D_MODEL // 2

VMEM_LIMIT_BYTES = 56 * 1024 * 1024


def _params(*sem):
    return pltpu.CompilerParams(dimension_semantics=sem, vmem_limit_bytes=VMEM_LIMIT_BYTES)


def _silu(x):
    return x * (1.0 / (1.0 + jnp.exp(-x)))


def _sigmoid(x):
    return 1.0 / (1.0 + jnp.exp(-x))


def _layer_norm(x, g, b):
    xc = x - jnp.mean(x, -1, keepdims=True)
    var = jnp.mean(xc * xc, -1, keepdims=True)
    return xc * lax.rsqrt(var + LN_EPS) * g + b


def _dot(a, b):
    return jnp.dot(a.astype(BF16), b.astype(BF16), preferred_element_type=F32)


def _dot_nt(a, b):
    return lax.dot_general(a.astype(BF16), b.astype(BF16), (((1,), (1,)), ((), ())),
                           preferred_element_type=F32)


def _dot_tn(a, b):
    return lax.dot_general(a.astype(BF16), b.astype(BF16), (((0,), (0,)), ((), ())),
                           preferred_element_type=F32)


def _hgrn_kernel(x_ref, s0_ref, win_ref, lbl_ref, ng_ref, wout_ref, lng_ref, lnb_ref,
                 h_ref, sout_ref, st_ref, *, layer, seg, n_seg, carry):
    rows = seg * n_seg
    x = x_ref[...].reshape(rows, D_MODEL)
    z = jnp.dot(x.astype(BF16), win_ref[...], preferred_element_type=F32)
    zq = z[:, 0 * A_WIDTH:1 * A_WIDTH]
    zf = z[:, 1 * A_WIDTH:2 * A_WIDTH]
    v = z[:, 2 * A_WIDTH:3 * A_WIDTH]
    zg = z[:, 3 * A_WIDTH:4 * A_WIDTH]

    lbl = lbl_ref[...]
    e = jnp.exp(lbl - jnp.max(lbl, axis=0, keepdims=True))
    lb = jnp.sum(e[:layer + 1], axis=0, keepdims=True) / jnp.sum(e, axis=0, keepdims=True)

    q = _silu(zq)
    f = lb + (1.0 - lb) * _sigmoid(zf)
    logf = jnp.log(f)
    k = 1.0 - f

    r_i = lax.broadcasted_iota(jnp.int32, (seg, seg), 0)
    c_i = lax.broadcasted_iota(jnp.int32, (seg, seg), 1)
    causal = c_i <= r_i
    tri = jnp.where(causal, 1.0, 0.0).astype(BF16)

    if carry:
        @pl.when(pl.program_id(1) == 0)
        def _():
            for h in range(A_HEADS):
                st_ref[h] = s0_ref[0, h].T

    o_parts = []
    for s in range(n_seg):
        sl = slice(s * seg, (s + 1) * seg)
        lf = logf[sl]
        lf_hi = lf.astype(BF16)
        lf_lo = (lf - lf_hi.astype(F32)).astype(BF16)
        g = jnp.dot(tri, lf_hi, preferred_element_type=F32) + jnp.dot(tri, lf_lo, preferred_element_type=F32)
        glast = g[seg - 1:seg, :]
        qg = q[sl] * jnp.exp(g)
        kg = k[sl] * jnp.exp(-g)
        kd = k[sl] * jnp.exp(glast - g)
        eg = jnp.exp(glast)
        vs = v[sl]
        heads = []
        for h in range(A_HEADS):
            cl = slice(h * A_DK, (h + 1) * A_DK)
            if carry:
                st = st_ref[h]
            else:
                st = s0_ref[s, h].T
            att = jnp.where(causal, _dot_nt(qg[:, cl], kg[:, cl]), 0.0)
            o = _dot_nt(qg[:, cl], st) + _dot(att, vs[:, cl])
            st_new = eg[:, cl] * st + _dot_tn(vs[:, cl], kd[:, cl])
            if carry:
                st_ref[h] = st_new
            else:
                sout_ref[s, h] = st_new.T
            o = o * lax.rsqrt(jnp.mean(o * o, -1, keepdims=True) + RMS_EPS)
            heads.append(o)
        o_parts.append(jnp.concatenate(heads, axis=1))
    o = o_parts[0] if n_seg == 1 else jnp.concatenate(o_parts, axis=0)
    o = o * ng_ref[...] * _silu(zg)
    y = jnp.dot(o.astype(BF16), wout_ref[...], preferred_element_type=F32)
    hh = _layer_norm(ALPHA * x + y, lng_ref[...], lnb_ref[...])
    h_ref[...] = hh.reshape(h_ref.shape)

    if carry:
        @pl.when(pl.program_id(1) == pl.num_programs(1) - 1)
        def _():
            for h in range(A_HEADS):
                sout_ref[0, h] = st_ref[h].T


def hgrn_layer(x, s0, w_in, lb_logits, norm_g, w_out, ln_g, ln_b, *, layer, seg, n_seg, carry):
    bsz, t, _ = x.shape
    row2 = lambda a: a.reshape(1, -1).astype(F32)
    w_in = w_in.astype(BF16)
    w_out = w_out.astype(BF16)
    const = lambda *_: (0, 0)
    if carry:
        tile = seg * n_seg
        grid = (bsz, t // tile)
        x_spec = pl.BlockSpec((1, tile, D_MODEL), lambda b, c: (b, c, 0))
        s_spec = pl.BlockSpec((1, A_HEADS, A_DK, A_DV), lambda b, c: (b, 0, 0, 0))
        sem = ("parallel", "arbitrary")
    else:
        assert t == seg
        grid = (bsz // n_seg, 1)
        x_spec = pl.BlockSpec((n_seg, seg, D_MODEL), lambda b, c: (b, 0, 0))
        s_spec = pl.BlockSpec((n_seg, A_HEADS, A_DK, A_DV), lambda b, c: (b, 0, 0, 0))
        sem = ("parallel", "arbitrary")
    kern = functools.partial(_hgrn_kernel, layer=layer, seg=seg, n_seg=n_seg, carry=carry)
    return pl.pallas_call(
        kern,
        grid=grid,
        in_specs=[
            x_spec, s_spec,
            pl.BlockSpec((D_MODEL, 4 * A_WIDTH), const),
            pl.BlockSpec(lb_logits.shape, const),
            pl.BlockSpec((1, A_WIDTH), const),
            pl.BlockSpec((A_WIDTH, D_MODEL), const),
            pl.BlockSpec((1, D_MODEL), const),
            pl.BlockSpec((1, D_MODEL), const),
        ],
        out_specs=[x_spec, s_spec],
        out_shape=[jax.ShapeDtypeStruct(x.shape, F32), jax.ShapeDtypeStruct(s0.shape, F32)],
        scratch_shapes=[pltpu.VMEM((A_HEADS, A_DV, A_DK), F32)],
        compiler_params=_params(*sem),
        name="hgrn_layer",
    )(x, s0, w_in, lb_logits.astype(F32), row2(norm_g), w_out, row2(ln_g), row2(ln_b))


def _ffn_kernel(x_ref, wa_ref, wu_ref, wo_ref, lng_ref, lnb_ref, o_ref, acc_ref):
    j = pl.program_id(1)
    xb = x_ref[...].astype(BF16)
    a = jnp.dot(xb, wa_ref[...], preferred_element_type=F32)
    u = jnp.dot(xb, wu_ref[...], preferred_element_type=F32)
    part = jnp.dot((_silu(a) * u).astype(BF16), wo_ref[...], preferred_element_type=F32)

    @pl.when(j == 0)
    def _():
        acc_ref[...] = part

    @pl.when(j > 0)
    def _():
        acc_ref[...] += part

    @pl.when(j == pl.num_programs(1) - 1)
    def _():
        o_ref[...] = _layer_norm(ALPHA * x_ref[...] + acc_ref[...], lng_ref[...], lnb_ref[...])


def ffn_layer(x, w_in, w_out, ln_g, ln_b, *, tm, tf):
    n, d = x.shape
    d_ff = w_out.shape[0]
    nf = d_ff // tf
    assert n % tm == 0 and d_ff % tf == 0
    w_in = w_in.astype(BF16)
    w_out = w_out.astype(BF16)
    row2 = lambda a: a.reshape(1, -1).astype(F32)
    return pl.pallas_call(
        _ffn_kernel,
        grid=(n // tm, nf),
        in_specs=[
            pl.BlockSpec((tm, d), lambda i, j: (i, 0)),
            pl.BlockSpec((d, tf), lambda i, j: (0, j)),
            pl.BlockSpec((d, tf), lambda i, j: (0, j + nf)),
            pl.BlockSpec((tf, d), lambda i, j: (j, 0)),
            pl.BlockSpec((1, d), lambda i, j: (0, 0)),
            pl.BlockSpec((1, d), lambda i, j: (0, 0)),
        ],
        out_specs=pl.BlockSpec((tm, d), lambda i, j: (i, 0)),
        out_shape=jax.ShapeDtypeStruct((n, d), F32),
        scratch_shapes=[pltpu.VMEM((tm, d), F32)],
        compiler_params=_params("parallel", "arbitrary"),
        name="ffn_layer",
    )(x, w_in, w_in, w_out, row2(ln_g), row2(ln_b))


def _router_gate(x, wr, br):
    logits = jnp.dot(x, wr, preferred_element_type=F32, precision=lax.Precision.HIGHEST) + br
    lane = lax.broadcasted_iota(jnp.int32, logits.shape, 1)
    m1 = jnp.max(logits, axis=-1, keepdims=True)
    i1 = jnp.min(jnp.where(logits == m1, lane, N_EXPERTS), axis=-1, keepdims=True)
    rest = jnp.where(lane == i1, -jnp.inf, logits)
    m2 = jnp.max(rest, axis=-1, keepdims=True)
    i2 = jnp.min(jnp.where(rest == m2, lane, N_EXPERTS), axis=-1, keepdims=True)
    e2 = jnp.exp(m2 - m1)
    den = 1.0 + e2
    return jnp.where(lane == i1, 1.0 / den, 0.0) + jnp.where(lane == i2, e2 / den, 0.0)


def _moe_kernel(x_ref, wr_ref, br_ref, wa_ref, wu_ref, wo_ref, lng_ref, lnb_ref, o_ref,
                gate_ref, acc_ref, tot_ref):
    e = pl.program_id(1)
    j = pl.program_id(2)
    nj = pl.num_programs(2)

    @pl.when((e == 0) & (j == 0))
    def _():
        gate_ref[...] = _router_gate(x_ref[...], wr_ref[...], br_ref[...])
        tot_ref[...] = jnp.zeros_like(tot_ref)

    xb = x_ref[...].astype(BF16)
    a = jnp.dot(xb, wa_ref[0], preferred_element_type=F32)
    u = jnp.dot(xb, wu_ref[0], preferred_element_type=F32)
    part = jnp.dot((_silu(a) * u).astype(BF16), wo_ref[0], preferred_element_type=F32)

    @pl.when(j == 0)
    def _():
        acc_ref[...] = part

    @pl.when(j > 0)
    def _():
        acc_ref[...] += part

    @pl.when(j == nj - 1)
    def _():
        gate = gate_ref[...]
        lane = lax.broadcasted_iota(jnp.int32, gate.shape, 1)
        ge = jnp.sum(jnp.where(lane == e, gate, 0.0), axis=-1, keepdims=True)
        tot_ref[...] += ge * acc_ref[...]

    @pl.when((j == nj - 1) & (e == pl.num_programs(1) - 1))
    def _():
        o_ref[...] = _layer_norm(ALPHA * x_ref[...] + tot_ref[...], lng_ref[...], lnb_ref[...])


def moe_layer_dense(x, w_router, b_router, w_in, w_out, ln_g, ln_b, *, tm, tf):
    n, d = x.shape
    ne, d_ff = w_out.shape[0], w_out.shape[1]
    nf = d_ff // tf
    assert n % tm == 0 and d_ff % tf == 0
    w_in = w_in.astype(BF16)
    w_out = w_out.astype(BF16)
    row2 = lambda a: a.reshape(1, -1).astype(F32)
    c2 = lambda i, e, j: (0, 0)
    return pl.pallas_call(
        _moe_kernel,
        grid=(n // tm, ne, nf),
        in_specs=[
            pl.BlockSpec((tm, d), lambda i, e, j: (i, 0)),
            pl.BlockSpec((d, ne), c2),
            pl.BlockSpec((1, ne), c2),
            pl.BlockSpec((1, d, tf), lambda i, e, j: (e, 0, j)),
            pl.BlockSpec((1, d, tf), lambda i, e, j: (e, 0, j + nf)),
            pl.BlockSpec((1, tf, d), lambda i, e, j: (e, j, 0)),
            pl.BlockSpec((1, d), c2),
            pl.BlockSpec((1, d), c2),
        ],
        out_specs=pl.BlockSpec((tm, d), lambda i, e, j: (i, 0)),
        out_shape=jax.ShapeDtypeStruct((n, d), F32),
        scratch_shapes=[pltpu.VMEM((tm, ne), F32), pltpu.VMEM((tm, d), F32), pltpu.VMEM((tm, d), F32)],
        compiler_params=_params("parallel", "arbitrary", "arbitrary"),
        name="moe_layer",
    )(x, w_router.astype(F32), row2(b_router), w_in, w_in, w_out, row2(ln_g), row2(ln_b))


MOE_RB = 272
MOE_ST = 256


def _router_kernel(x_ref, wrt_ref, br_ref, gate_ref, xb_ref):
    x = x_ref[...]
    logits = lax.dot_general(wrt_ref[...], x, (((1,), (1,)), ((), ())), preferred_element_type=F32,
                             precision=lax.Precision.HIGHEST) + br_ref[...]
    eidx = lax.broadcasted_iota(jnp.int32, logits.shape, 0)
    m1 = jnp.max(logits, axis=0, keepdims=True)
    i1 = jnp.min(jnp.where(logits == m1, eidx, N_EXPERTS), axis=0, keepdims=True)
    rest = jnp.where(eidx == i1, -jnp.inf, logits)
    m2 = jnp.max(rest, axis=0, keepdims=True)
    i2 = jnp.min(jnp.where(rest == m2, eidx, N_EXPERTS), axis=0, keepdims=True)
    e2 = jnp.exp(m2 - m1)
    den = 1.0 + e2
    gate_ref[...] = jnp.where(eidx == i1, 1.0 / den, 0.0) + jnp.where(eidx == i2, e2 / den, 0.0)
    xb_ref[...] = x.astype(BF16)


def moe_router(x, w_router, b_router, *, tm):
    n, d = x.shape
    return pl.pallas_call(
        _router_kernel,
        grid=(n // tm,),
        in_specs=[pl.BlockSpec((tm, d), lambda i: (i, 0)), pl.BlockSpec((N_EXPERTS, d), lambda i: (0, 0)),
                  pl.BlockSpec((N_EXPERTS, 1), lambda i: (0, 0))],
        out_specs=[pl.BlockSpec((N_EXPERTS, tm), lambda i: (0, i)), pl.BlockSpec((tm, d), lambda i: (i, 0))],
        out_shape=[jax.ShapeDtypeStruct((N_EXPERTS, n), F32), jax.ShapeDtypeStruct((n, d), BF16)],
        compiler_params=_params("parallel"),
        name="moe_router",
    )(x, w_router.T.astype(F32), b_router.reshape(N_EXPERTS, 1).astype(F32))


def _moe_routed_kernel(cum_ref, xb_ref, gate_ref, rank_ref, wa_ref, wu_ref, wo_ref, y_ref, xs_ref, acc_ref,
                       *, n_sub):
    i = pl.program_id(0)
    e = pl.program_id(1)
    j = pl.program_id(2)
    nj = pl.num_programs(2)
    base = (i * N_EXPERTS + e) * (n_sub + 1)
    count = cum_ref[base + n_sub]
    n_blk = (count + MOE_RB - 1) // MOE_RB
    p_iota = lax.broadcasted_iota(jnp.int32, (MOE_RB, MOE_ST), 0)

    def overlaps(b, c):
        return (cum_ref[base + c] < (b + 1) * MOE_RB) & (cum_ref[base + c + 1] > b * MOE_RB)

    def one_hot(b, c):
        c0 = pl.multiple_of(c * MOE_ST, MOE_ST)
        rank = rank_ref[pl.ds(e, 1), pl.ds(c0, MOE_ST)]
        gate = gate_ref[pl.ds(e, 1), pl.ds(c0, MOE_ST)]
        hit = (rank == b * MOE_RB + p_iota) & (gate > 0.0)
        return hit, gate, c0

    @pl.when((e == 0) & (j == 0))
    def _():
        y_ref[...] = jnp.zeros_like(y_ref)

    @pl.when(j == 0)
    def _():
        def gather_block(b, _):
            xs_ref[b] = jnp.zeros((MOE_RB, D_MODEL), BF16)

            def sub(c, _):
                @pl.when(overlaps(b, c))
                def _():
                    hit, _, c0 = one_hot(b, c)
                    part = jnp.dot(jnp.where(hit, 1.0, 0.0).astype(BF16), xb_ref[pl.ds(c0, MOE_ST), :],
                                   preferred_element_type=F32)
                    xs_ref[b] = xs_ref[b] + part.astype(BF16)
                return 0

            lax.fori_loop(0, n_sub, sub, 0)
            return 0

        lax.fori_loop(0, n_blk, gather_block, 0)

    def ffn_block(b, _):
        xs = xs_ref[b]
        a = jnp.dot(xs, wa_ref[0], preferred_element_type=F32)
        u = jnp.dot(xs, wu_ref[0], preferred_element_type=F32)
        part = jnp.dot((_silu(a) * u).astype(BF16), wo_ref[0], preferred_element_type=F32)

        @pl.when(j == 0)
        def _():
            acc_ref[b] = part

        @pl.when(j > 0)
        def _():
            acc_ref[b] = acc_ref[b] + part
        return 0

    lax.fori_loop(0, n_blk, ffn_block, 0)

    @pl.when(j == nj - 1)
    def _():
        def scatter_block(b, _):
            def sub(c, _):
                @pl.when(overlaps(b, c))
                def _():
                    hit, gate, c0 = one_hot(b, c)
                    g_rows = jnp.sum(jnp.where(hit, gate, 0.0), axis=1, keepdims=True)
                    out = acc_ref[b] * g_rows
                    hi = out.astype(BF16)
                    lo = (out - hi.astype(F32)).astype(BF16)
                    oh = jnp.where(hit, 1.0, 0.0).astype(BF16)
                    y_ref[pl.ds(c0, MOE_ST), :] += _dot_tn(oh, hi) + _dot_tn(oh, lo)
                return 0

            lax.fori_loop(0, n_sub, sub, 0)
            return 0

        lax.fori_loop(0, n_blk, scatter_block, 0)


def moe_routed(xb, gate_t, w_in, w_out, *, tm, tf):
    n, d = xb.shape
    ne, d_ff = w_out.shape[0], w_out.shape[1]
    nf = d_ff // tf
    n_tiles, n_sub = n // tm, tm // MOE_ST
    assert n % tm == 0 and tm % MOE_ST == 0 and d_ff % tf == 0
    max_blk = -(-tm // MOE_RB)
    mask = (gate_t > 0.0).astype(jnp.int32).reshape(ne, n_tiles, tm)
    rank = (jnp.cumsum(mask, axis=-1) - mask).reshape(ne, n)
    cnt = mask.reshape(ne, n_tiles, n_sub, MOE_ST).sum(-1)
    cum = jnp.concatenate([jnp.zeros((ne, n_tiles, 1), jnp.int32), jnp.cumsum(cnt, axis=-1)], axis=-1)
    cum = cum.transpose(1, 0, 2).reshape(-1).astype(jnp.int32)
    w_in = w_in.astype(BF16)
    w_out = w_out.astype(BF16)
    kern = functools.partial(_moe_routed_kernel, n_sub=n_sub)
    return pl.pallas_call(
        kern,
        grid_spec=pltpu.PrefetchScalarGridSpec(
            num_scalar_prefetch=1,
            grid=(n_tiles, ne, nf),
            in_specs=[
                pl.BlockSpec((tm, d), lambda i, e, j, c: (i, 0)),
                pl.BlockSpec((ne, tm), lambda i, e, j, c: (0, i)),
                pl.BlockSpec((ne, tm), lambda i, e, j, c: (0, i)),
                pl.BlockSpec((1, d, tf), lambda i, e, j, c: (e, 0, j)),
                pl.BlockSpec((1, d, tf), lambda i, e, j, c: (e, 0, j + nf)),
                pl.BlockSpec((1, tf, d), lambda i, e, j, c: (e, j, 0)),
            ],
            out_specs=pl.BlockSpec((tm, d), lambda i, e, j, c: (i, 0)),
            scratch_shapes=[pltpu.VMEM((max_blk, MOE_RB, d), BF16), pltpu.VMEM((max_blk, MOE_RB, d), F32)]),
        out_shape=jax.ShapeDtypeStruct((n, d), F32),
        compiler_params=_params("parallel", "arbitrary", "arbitrary"),
        name="moe_routed",
    )(cum, xb, gate_t, rank, w_in, w_in, w_out)


def _add_ln_kernel(x_ref, y_ref, lng_ref, lnb_ref, o_ref):
    o_ref[...] = _layer_norm(ALPHA * x_ref[...] + y_ref[...], lng_ref[...], lnb_ref[...])


def add_ln(x, y, ln_g, ln_b, *, tm):
    n, d = x.shape
    row2 = lambda a: a.reshape(1, -1).astype(F32)
    blk = pl.BlockSpec((tm, d), lambda i: (i, 0))
    c2 = pl.BlockSpec((1, d), lambda i: (0, 0))
    return pl.pallas_call(
        _add_ln_kernel, grid=(n // tm,), in_specs=[blk, blk, c2, c2], out_specs=blk,
        out_shape=jax.ShapeDtypeStruct((n, d), F32), compiler_params=_params("parallel"), name="add_ln",
    )(x, y, row2(ln_g), row2(ln_b))


def moe_layer(x, w_router, b_router, w_in, w_out, ln_g, ln_b, *, tm, tf):
    n = x.shape[0]
    gate_t, xb = moe_router(x, w_router, b_router, tm=_tile(n, 1024))
    y = moe_routed(xb, gate_t, w_in, w_out, tm=tm, tf=tf)
    return add_ln(x, y, ln_g, ln_b, tm=_tile(n, 1024))


def _proj_kernel(x_ref, w_ref, o_ref):
    o_ref[...] = jnp.dot(x_ref[...].astype(BF16), w_ref[...], preferred_element_type=F32)


def proj(x, w, *, tm):
    n, d = x.shape
    m = w.shape[1]
    return pl.pallas_call(
        _proj_kernel,
        grid=(n // tm,),
        in_specs=[pl.BlockSpec((tm, d), lambda i: (i, 0)), pl.BlockSpec((d, m), lambda i: (0, 0))],
        out_specs=pl.BlockSpec((tm, m), lambda i: (i, 0)),
        out_shape=jax.ShapeDtypeStruct((n, m), F32),
        compiler_params=_params("parallel"),
        name="proj",
    )(x, w.astype(BF16))


def _proj_ln_kernel(o_ref, x_ref, w_ref, lng_ref, lnb_ref, h_ref):
    y = jnp.dot(o_ref[...].astype(BF16), w_ref[...], preferred_element_type=F32)
    h_ref[...] = _layer_norm(ALPHA * x_ref[...] + y, lng_ref[...], lnb_ref[...])


def proj_ln(o, x, w, ln_g, ln_b, *, tm):
    n, d = x.shape
    row2 = lambda a: a.reshape(1, -1).astype(F32)
    c2 = lambda i: (0, 0)
    return pl.pallas_call(
        _proj_ln_kernel,
        grid=(n // tm,),
        in_specs=[pl.BlockSpec((tm, o.shape[1]), lambda i: (i, 0)), pl.BlockSpec((tm, d), lambda i: (i, 0)),
                  pl.BlockSpec(w.shape, c2), pl.BlockSpec((1, d), c2), pl.BlockSpec((1, d), c2)],
        out_specs=pl.BlockSpec((tm, d), lambda i: (i, 0)),
        out_shape=jax.ShapeDtypeStruct((n, d), F32),
        compiler_params=_params("parallel"),
        name="proj_ln",
    )(o, x, w.astype(BF16), row2(ln_g), row2(ln_b))


N_KV_COLS = 6 * B_KV * HEAD_DIM
KV_SET = 2 * B_KV * HEAD_DIM
N_Q_COLS = B_HEADS * HEAD_DIM
GATE_LANES = 128


def _nsa_proj_kernel(x_ref, w_ref, cmp_ref, sel_ref, win_ref, q_ref, ks_ref, vs_ref, kw_ref, vw_ref, g_ref):
    z = jnp.dot(x_ref[0].astype(BF16), w_ref[...], preferred_element_type=F32)
    cmp_ref[0] = z[:, 0:KV_SET]
    sel_ref[0] = z[:, KV_SET:2 * KV_SET]
    win_ref[0] = z[:, 2 * KV_SET:3 * KV_SET]
    half = B_KV * HEAD_DIM
    for g in range(B_KV):
        lo = KV_SET + g * HEAD_DIM
        ks_ref[0, g] = z[:, lo:lo + HEAD_DIM].astype(BF16)
        vs_ref[0, g] = z[:, lo + half:lo + half + HEAD_DIM].astype(BF16)
        lo = 2 * KV_SET + g * HEAD_DIM
        kw_ref[0, g] = z[:, lo:lo + HEAD_DIM].astype(BF16)
        vw_ref[0, g] = z[:, lo + half:lo + half + HEAD_DIM].astype(BF16)
    for h in range(B_HEADS):
        lo = N_KV_COLS + h * HEAD_DIM
        q_ref[0, h] = (z[:, lo:lo + HEAD_DIM] * SCALE).astype(BF16)
    g_ref[0] = _sigmoid(z[:, N_KV_COLS + N_Q_COLS:N_KV_COLS + N_Q_COLS + GATE_LANES])


def nsa_proj(h, kv_w, w_qg, *, tm):
    bsz, t, d = h.shape
    w = jnp.concatenate([kv_w, w_qg], axis=1)
    w = jnp.pad(w, ((0, 0), (0, N_KV_COLS + N_Q_COLS + GATE_LANES - w.shape[1]))).astype(BF16)
    row = lambda: pl.BlockSpec((1, tm, KV_SET), lambda b, i: (b, i, 0))
    hm = lambda nh: pl.BlockSpec((1, nh, tm, HEAD_DIM), lambda b, i: (b, 0, i, 0))
    sds = jax.ShapeDtypeStruct
    return pl.pallas_call(
        _nsa_proj_kernel,
        grid=(bsz, t // tm),
        in_specs=[pl.BlockSpec((1, tm, d), lambda b, i: (b, i, 0)), pl.BlockSpec(w.shape, lambda b, i: (0, 0))],
        out_specs=[row(), row(), row(), hm(B_HEADS), hm(B_KV), hm(B_KV), hm(B_KV), hm(B_KV),
                   pl.BlockSpec((1, tm, GATE_LANES), lambda b, i: (b, i, 0))],
        out_shape=[sds((bsz, t, KV_SET), F32)] * 3 + [sds((bsz, B_HEADS, t, HEAD_DIM), BF16)]
        + [sds((bsz, B_KV, t, HEAD_DIM), BF16)] * 4 + [sds((bsz, t, GATE_LANES), F32)],
        compiler_params=_params("parallel", "parallel"),
        name="nsa_proj",
    )(h, w)


PAIR = 2 * HEAD_DIM


def _compress_kernel(*refs, n_in, head_major, paged):
    if paged:
        refs = refs[1:]
    x_refs = refs[:n_in]
    wp_ref, u_ref, b1_ref, w2_ref, kc_ref, vc_ref, carry_ref = refs[n_in:]
    i = pl.program_id(1)

    @pl.when(i == 0)
    def _():
        carry_ref[...] = jnp.zeros_like(carry_ref)

    cpp = PAGE_SIZE // CMP_STRIDE
    pr = lax.broadcasted_iota(jnp.int32, (PAGE_SIZE, PAGE_SIZE), 0)
    pc = lax.broadcasted_iota(jnp.int32, (PAGE_SIZE, PAGE_SIZE), 1)
    perm = jnp.where(pc == CMP_STRIDE * (pr % cpp) + pr // cpp, 1.0, 0.0).astype(BF16)
    pages = []
    for r in x_refs:
        if paged:
            pages.append(_dot_nt(perm, r[0]))
        else:
            for p0 in range(0, r.shape[1], PAGE_SIZE):
                pages.append(jnp.dot(perm, r[0, p0:p0 + PAGE_SIZE, :].astype(BF16), preferred_element_type=F32))
    ch = len(pages) * cpp
    row0 = lax.broadcasted_iota(jnp.int32, (ch, PAIR), 0) == 0

    def chunk_rows(s, lo):
        parts = [pg[s * cpp:(s + 1) * cpp, lo:lo + PAIR] for pg in pages]
        return parts[0] if len(parts) == 1 else jnp.concatenate(parts, axis=0)
    for k in range(2):
        wp = wp_ref[k]
        pbm = jnp.dot(u_ref[k], wp, preferred_element_type=F32)
        pb = pbm[0:1, 0:PAIR] + pbm[1:2, PAIR:2 * PAIR] + b1_ref[k]
        for gp in range(B_KV // 2):
            base = k * B_KV * HEAD_DIM + gp * PAIR
            lhs = jnp.concatenate([chunk_rows(s, base) for s in range(CMP_STRIDE)], axis=1).astype(BF16)
            r = jnp.dot(lhs, wp, preferred_element_type=F32)
            first, second = r[:, 0:PAIR], r[:, PAIR:2 * PAIR]
            slot = k * (B_KV // 2) + gp
            prev = carry_ref[slot]
            shifted = jnp.where(row0, prev[7:8, :], pltpu.roll(first, 1, axis=0))
            carry_ref[slot] = first[ch - 8:ch, :]
            hid = jax.nn.gelu(shifted + second + pb)
            out = jnp.dot(hid.astype(BF16), w2_ref[k], preferred_element_type=F32).astype(BF16)
            dst = kc_ref if k == 0 else vc_ref
            if head_major:
                for g2 in range(2):
                    dst[0, 2 * gp + g2] = out[:, g2 * HEAD_DIM:(g2 + 1) * HEAD_DIM]
            else:
                dst[0, :, gp * PAIR:(gp + 1) * PAIR] = out


def _compress_weights(cmp_pe, cmp_w1, cmp_b1, cmp_w2):
    eye2 = jnp.eye(2, dtype=F32)
    w1 = cmp_w1.reshape(2, CMP_STRIDE, 2, HEAD_DIM, HEAD_DIM)
    wp = jnp.einsum('fskdh,ab->ksadfbh', w1, eye2).reshape(2, CMP_STRIDE * PAIR, 2 * PAIR)
    pe = cmp_pe.reshape(2, CMP_STRIDE, 2, HEAD_DIM)
    u = jnp.broadcast_to(pe.transpose(2, 0, 1, 3)[:, :, :, None, :], (2, 2, CMP_STRIDE, 2, HEAD_DIM))
    u = jnp.pad(u.reshape(2, 2, CMP_STRIDE * PAIR), ((0, 0), (0, 6), (0, 0)))
    b1 = jnp.tile(cmp_b1, (1, 2)).reshape(2, 1, PAIR)
    w2 = jnp.einsum('khd,ab->kahbd', cmp_w2, eye2).reshape(2, PAIR, PAIR)
    return wp.astype(BF16), u.astype(BF16), b1.astype(F32), w2.astype(BF16)


def compress(x, page_table, cmp_pe, cmp_w1, cmp_b1, cmp_w2, *, ch, head_major):
    wp, u, b1, w2 = _compress_weights(cmp_pe, cmp_w1, cmp_b1, cmp_w2)
    paged = page_table is not None
    if paged:
        bsz, n_pages = page_table.shape
        n_in = ch * CMP_STRIDE // PAGE_SIZE
        n_chunks = n_pages * PAGE_SIZE // CMP_STRIDE
        x_specs = [pl.BlockSpec((1, KV_SET, PAGE_SIZE), functools.partial(
            lambda b, i, pt, j: (pt[b, i * n_in + j], 0, 0), j=j)) for j in range(n_in)]
        cm = lambda f: (lambda b, i, pt: f(b, i))
    else:
        bsz, t, _ = x.shape
        n_chunks = t // CMP_STRIDE
        n_in = 1
        x_specs = [pl.BlockSpec((1, ch * CMP_STRIDE, KV_SET), lambda b, i: (b, i, 0))]
        cm = lambda f: f
    c3 = cm(lambda b, i: (0, 0, 0))
    if head_major:
        o_spec = pl.BlockSpec((1, B_KV, ch, HEAD_DIM), cm(lambda b, i: (b, 0, i, 0)))
        o_shape = jax.ShapeDtypeStruct((bsz, B_KV, n_chunks, HEAD_DIM), BF16)
    else:
        o_spec = pl.BlockSpec((1, ch, B_KV * HEAD_DIM), cm(lambda b, i: (b, i, 0)))
        o_shape = jax.ShapeDtypeStruct((bsz, n_chunks, B_KV * HEAD_DIM), BF16)
    in_specs = x_specs + [pl.BlockSpec(wp.shape, c3), pl.BlockSpec(u.shape, c3),
                          pl.BlockSpec(b1.shape, c3), pl.BlockSpec(w2.shape, c3)]
    grid = (bsz, n_chunks // ch)
    scratch = [pltpu.VMEM((2 * (B_KV // 2), 8, PAIR), F32)]
    kern = functools.partial(_compress_kernel, n_in=n_in, head_major=head_major, paged=paged)
    if paged:
        grid_spec = pltpu.PrefetchScalarGridSpec(num_scalar_prefetch=1, grid=grid, in_specs=in_specs,
                                                 out_specs=[o_spec, o_spec], scratch_shapes=scratch)
        args = (page_table,) + (x,) * n_in
    else:
        grid_spec = pl.GridSpec(grid=grid, in_specs=in_specs, out_specs=[o_spec, o_spec], scratch_shapes=scratch)
        args = (x,)
    return pl.pallas_call(
        kern, grid_spec=grid_spec, out_shape=[o_shape, o_shape],
        compiler_params=_params("parallel", "arbitrary"), name="compress",
    )(*args, wp, u, b1, w2)


def _select_blocks(imp, blk, cur, n_sel, axis):
    valid = blk <= cur
    forced = (blk == 0) | (valid & (blk > cur - N_LOCAL))
    score = jnp.where(forced, FORCE, jnp.where(valid, imp, -FORCE))
    score = jnp.where(blk < n_sel, score, -jnp.inf)
    out = jnp.full(score.shape, NEG_BIG, F32)
    big = jnp.int32(2 ** 30)
    for _ in range(min(TOP_N, n_sel)):
        m = jnp.max(score, axis=axis, keepdims=True)
        first = jnp.min(jnp.where(score == m, blk, big), axis=axis, keepdims=True)
        pick = blk == first
        out = jnp.where(pick, 0.0, out)
        score = jnp.where(pick, -jnp.inf, score)
    return jnp.where(valid, out, NEG_BIG)


def _overlap(c1, j, n_sel):
    c0 = (c1 - 1) * CMP_STRIDE
    j0 = j * SEL_BLOCK
    return ((c1 >= 1) & (j < n_sel) & (c0 <= j0 + SEL_BLOCK - 1) & (c0 + CMP_BLOCK - 1 >= j0)).astype(F32)


def _rank_select(imp, blk, cur, n_sel):
    valid = blk <= cur
    forced = (blk == 0) | (valid & (blk > cur - N_LOCAL))
    score = jnp.where(forced, FORCE, jnp.where(valid, imp, -FORCE))
    rank = jnp.zeros(score.shape, F32)
    for jp in range(n_sel):
        row = score[jp:jp + 1, :]
        gt = jnp.where(row > score, 1.0, 0.0)
        ge = jnp.where(row >= score, 1.0, 0.0)
        rank = rank + jnp.where(blk > jp, ge, gt)
    keep = jnp.where(rank < TOP_N, 0.0, NEG_BIG)
    return jnp.where(valid, jnp.where(blk < n_sel, keep, NEG_BIG), NEG_BIG)


def _online_softmax_t(s, m, l, acc, v):
    m_new = jnp.maximum(m, jnp.max(s, axis=0, keepdims=True))
    alpha = jnp.exp(m - m_new)
    p = jnp.exp(s - m_new)
    l = alpha * l + jnp.sum(p, axis=0, keepdims=True)
    acc = alpha * acc + _dot_tn(v, p)
    return m_new, l, acc


def _prompt_attn_kernel(q_ref, kc_ref, vc_ref, ks_ref, vs_ref, kw_ref, vw_ref, g_ref, o_ref, gt_ref,
                        *, tq, tk, n_sel, nb):
    g = pl.program_id(2)
    t0 = pl.program_id(1) * tq
    cols = B_GROUP * tq
    q = q_ref[0].reshape(cols, HEAD_DIM)
    tpos = t0 + lax.broadcasted_iota(jnp.int32, (1, tq), 1)
    nch = kc_ref.shape[2]
    rep = lambda a: jnp.concatenate([a] * B_GROUP, axis=1)

    c1 = lax.broadcasted_iota(jnp.int32, (nch, 1), 0)
    cmask = rep((c1 >= 1) & (c1 * CMP_STRIDE + (CMP_BLOCK - CMP_STRIDE - 1) <= tpos))
    s = jnp.where(cmask, _dot_nt(kc_ref[0, 0], q), NEG_BIG)
    m = jnp.max(s, axis=0, keepdims=True)
    e = jnp.where(cmask, jnp.exp(s - m), 0.0)
    l = jnp.sum(e, axis=0, keepdims=True)
    p = e * (1.0 / jnp.where(l == 0.0, 1.0, l))
    o_cmp = _dot_tn(vc_ref[0, 0], p)

    p4 = p[:, 0:tq]
    for r in range(1, B_GROUP):
        p4 = p4 + p[:, r * tq:(r + 1) * tq]
    ov_t = _overlap(lax.broadcasted_iota(jnp.int32, (nb, nch), 1),
                    lax.broadcasted_iota(jnp.int32, (nb, nch), 0), n_sel)
    imp = jnp.dot(ov_t, p4, preferred_element_type=F32, precision=lax.Precision.HIGHEST)
    blk = lax.broadcasted_iota(jnp.int32, (nb, tq), 0)
    bias = _rank_select(imp, blk, tpos // SEL_BLOCK, n_sel).astype(BF16)

    def sel_scores(kt):
        k0 = pl.multiple_of(kt * tk, tk)
        kpos = k0 + lax.broadcasted_iota(jnp.int32, (tk, 1), 0)
        blk_of_key = (k0 + lax.broadcasted_iota(jnp.int32, (tk, nb), 0)) // SEL_BLOCK
        onehot = jnp.where(blk_of_key == lax.broadcasted_iota(jnp.int32, (tk, nb), 1), 1.0, 0.0).astype(BF16)
        mb = jnp.dot(onehot, bias, preferred_element_type=F32)
        mb = jnp.where(kpos <= tpos, mb, NEG_BIG)
        return _dot_nt(ks_ref[0, 0, pl.ds(k0, tk), :], q) + rep(mb), vs_ref[0, 0, pl.ds(k0, tk), :]

    def win_scores(kt):
        k0 = pl.multiple_of(kt * tk, tk)
        kpos = k0 + lax.broadcasted_iota(jnp.int32, (tk, 1), 0)
        mb = jnp.where((kpos <= tpos) & (kpos >= tpos - WINDOW), 0.0, NEG_BIG)
        return _dot_nt(kw_ref[0, 0, pl.ds(k0, tk), :], q) + rep(mb), vw_ref[0, 0, pl.ds(k0, tk), :]

    def sel_only(kt, carry):
        s, v = sel_scores(kt)
        return _online_softmax_t(s, *carry, v)

    def both(kt, carry):
        s, v = sel_scores(kt)
        sw, vw = win_scores(kt)
        return _online_softmax_t(s, *carry[:3], v) + _online_softmax_t(sw, *carry[3:], vw)

    init = (jnp.full((1, cols), NEG_BIG, F32), jnp.zeros((1, cols), F32), jnp.zeros((HEAD_DIM, cols), F32))
    hi = (t0 + tq - 1) // tk + 1
    lo_w = jnp.maximum(t0 - WINDOW, 0) // tk
    sel_state = lax.fori_loop(0, lo_w, sel_only, init)
    _, l_s, acc_s, _, l_w, acc_w = lax.fori_loop(lo_w, hi, both, sel_state + init)

    gt_ref[...] = g_ref[0].T
    gate = lambda br: jnp.concatenate(
        [gt_ref[pl.ds(g * (3 * B_GROUP) + 3 * r + br, 1), :] for r in range(B_GROUP)], axis=1)
    o = gate(0) * o_cmp + (gate(1) * (1.0 / l_s)) * acc_s + (gate(2) * (1.0 / l_w)) * acc_w
    for r in range(B_GROUP):
        o_ref[0, :, r * HEAD_DIM:(r + 1) * HEAD_DIM] = o[:, r * tq:(r + 1) * tq].T.astype(o_ref.dtype)


def nsa_prompt_attn(q, kc, vc, ks, vs, kw, vw, gates, *, tq, tk):
    bsz, _, t, _ = q.shape
    nch = kc.shape[2]
    n_sel = -(-t // SEL_BLOCK)
    nb = -(-n_sel // 16) * 16
    assert t % tq == 0 and t % tk == 0 and tq == GATE_LANES
    seq = lambda n: pl.BlockSpec((1, 1, n, HEAD_DIM), lambda b, i, g: (b, g, 0, 0))
    kern = functools.partial(_prompt_attn_kernel, tq=tq, tk=tk, n_sel=n_sel, nb=nb)
    return pl.pallas_call(
        kern,
        grid=(bsz, t // tq, B_KV),
        in_specs=[pl.BlockSpec((1, B_GROUP, tq, HEAD_DIM), lambda b, i, g: (b, g, i, 0)),
                  seq(nch), seq(nch), seq(t), seq(t), seq(t), seq(t),
                  pl.BlockSpec((1, tq, GATE_LANES), lambda b, i, g: (b, i, 0))],
        out_specs=pl.BlockSpec((1, tq, B_GROUP * HEAD_DIM), lambda b, i, g: (b, i, g)),
        out_shape=jax.ShapeDtypeStruct((bsz, t, N_Q_COLS), BF16),
        scratch_shapes=[pltpu.VMEM((GATE_LANES, tq), F32)],
        compiler_params=_params("parallel", "parallel", "arbitrary"),
        name="nsa_prompt_attn",
    )(q, kc, vc, ks, vs, kw, vw, gates)


KV_HALF = B_KV * HEAD_DIM
Q_COLS = B_GROUP * B_KV * 8


def _sample_attn_kernel(*refs, n_pages_step, past, t_new, n_keep):
    pt_ref, qbd_ref, gl_ref, kc_ref, vc_ref, swin_ref, nsel_ref, nwin_ref = refs[:8]
    page_refs = refs[8:8 + n_pages_step]
    out_ref, bias_ref, m_ref, l_ref, acc_ref, oth_ref = refs[8 + n_pages_step:]
    del pt_ref
    i = pl.program_id(1)
    qbd = qbd_ref[0]
    ncol = qbd.shape[1]
    col = lax.broadcasted_iota(jnp.int32, (1, ncol), 1)
    tcol = col % t_new
    qpos = past + tcol
    sig = _sigmoid(gl_ref[0])
    n_sel = -(-(past + t_new) // SEL_BLOCK)
    blk_step = n_pages_step * PAGE_SIZE // SEL_BLOCK
    pad_rows = 8

    def scores(k):
        return jnp.dot(k.astype(BF16), qbd, preferred_element_type=F32) * SCALE

    @pl.when(i == 0)
    def _():
        nch = kc_ref.shape[1]
        c1 = lax.broadcasted_iota(jnp.int32, (nch, 1), 0)
        cmask = (c1 >= 1) & (c1 * CMP_STRIDE + (CMP_BLOCK - CMP_STRIDE - 1) <= qpos)
        s = jnp.where(cmask, scores(kc_ref[0]), NEG_BIG)
        m = jnp.max(s, axis=0, keepdims=True)
        e = jnp.where(cmask, jnp.exp(s - m), 0.0)
        l = jnp.sum(e, axis=0, keepdims=True)
        p = e / jnp.where(l == 0.0, 1.0, l)
        o_cmp = _dot_tn(vc_ref[0], p)
        nb = bias_ref.shape[0]
        ov_t = _overlap(lax.broadcasted_iota(jnp.int32, (nb, nch), 1),
                        lax.broadcasted_iota(jnp.int32, (nb, nch), 0), n_sel)
        imp = jnp.dot(ov_t, p, preferred_element_type=F32, precision=lax.Precision.HIGHEST)
        per = ncol // B_GROUP
        imp = imp + pltpu.roll(imp, per, axis=1) + pltpu.roll(imp, 2 * per, axis=1) + pltpu.roll(imp, 3 * per, axis=1)
        blk = lax.broadcasted_iota(jnp.int32, (nb, ncol), 0)
        bias_ref[...] = _select_blocks(imp, blk, qpos // SEL_BLOCK, n_sel, axis=0)
        kv_w = jnp.concatenate([swin_ref[0], nwin_ref[0], jnp.zeros((pad_rows, 2 * KV_HALF), F32)], axis=0)
        nw = kv_w.shape[0]
        wi = lax.broadcasted_iota(jnp.int32, (nw, 1), 0)
        wpos = past - n_keep + wi
        wmask = (wi < n_keep + t_new) & (wpos <= qpos) & (wpos >= qpos - WINDOW) & (wpos >= 0)
        s = jnp.where(wmask, scores(kv_w[:, :KV_HALF]), NEG_BIG)
        m = jnp.max(s, axis=0, keepdims=True)
        e = jnp.where(wmask, jnp.exp(s - m), 0.0)
        l = jnp.sum(e, axis=0, keepdims=True)
        o_win = _dot_tn(kv_w[:, KV_HALF:], e / jnp.where(l == 0.0, 1.0, l))
        oth_ref[...] = sig[0:1] * o_cmp + sig[2:3] * o_win
        m_ref[...] = jnp.full(m_ref.shape, NEG_BIG, F32)
        l_ref[...] = jnp.zeros(l_ref.shape, F32)
        acc_ref[...] = jnp.zeros(acc_ref.shape, F32)

    def update(s, v=None, v_t=None):
        m_new = jnp.maximum(m_ref[...], jnp.max(s, axis=0, keepdims=True))
        alpha = jnp.exp(m_ref[...] - m_new)
        p = jnp.exp(s - m_new)
        l_ref[...] = alpha * l_ref[...] + jnp.sum(p, axis=0, keepdims=True)
        pv = _dot_tn(v, p) if v_t is None else _dot(v_t, p)
        acc_ref[...] = alpha * acc_ref[...] + pv
        m_ref[...] = m_new

    bias = bias_ref[pl.ds(pl.multiple_of(i * blk_step, blk_step), blk_step), :]
    per_page = PAGE_SIZE // SEL_BLOCK
    s_parts, v_parts = [], []
    for p in range(n_pages_step):
        page = page_refs[p][0]
        mb = jnp.concatenate([jnp.broadcast_to(bias[per_page * p + j:per_page * p + j + 1, :], (SEL_BLOCK, ncol))
                              for j in range(per_page)], axis=0)
        s_parts.append(_dot_tn(page[:KV_HALF], qbd) * SCALE + mb)
        v_parts.append(page[KV_HALF:].astype(BF16))
    update(jnp.concatenate(s_parts, axis=0), v_t=jnp.concatenate(v_parts, axis=1))

    @pl.when(i == pl.num_programs(1) - 1)
    def _():
        kv_n = jnp.concatenate([nsel_ref[0], jnp.zeros((pad_rows, 2 * KV_HALF), F32)], axis=0)
        u = lax.broadcasted_iota(jnp.int32, (kv_n.shape[0], 1), 0)
        nb_new = past // SEL_BLOCK
        s = scores(kv_n[:, :KV_HALF]) + bias_ref[nb_new:nb_new + 1, :]
        update(jnp.where((u < t_new) & (past + u <= qpos), s, NEG_BIG), kv_n[:, KV_HALF:])
        o = oth_ref[...] + sig[1:2] * acc_ref[...] / l_ref[...]
        g_row = lax.broadcasted_iota(jnp.int32, o.shape, 0) // HEAD_DIM
        g_col = (lax.broadcasted_iota(jnp.int32, o.shape, 1) // t_new) % B_KV
        o = jnp.where(g_row == g_col, o, 0.0)
        out_ref[0] = o[0:HEAD_DIM] + o[HEAD_DIM:2 * HEAD_DIM] + o[2 * HEAD_DIM:3 * HEAD_DIM] + o[3 * HEAD_DIM:]


def nsa_sample_attn(q, gate_logits, kc, vc, cache_kv_sel, state_kv_win, kv_sel_new, kv_win_new, page_table,
                    *, n_pages_step):
    bsz, t, _ = q.shape
    n_pages = page_table.shape[1]
    past = n_pages * PAGE_SIZE
    n_keep = state_kv_win.shape[1]
    assert t * B_GROUP * B_KV == Q_COLS and past % SEL_BLOCK == 0 and n_pages % n_pages_step == 0
    qt = q.reshape(bsz, t, B_KV, B_GROUP, HEAD_DIM).transpose(0, 2, 4, 3, 1)
    qbd = jnp.einsum('bgdrt,gh->bgdrht', qt, jnp.eye(B_KV, dtype=F32)).reshape(bsz, KV_HALF, Q_COLS).astype(BF16)
    gl = gate_logits.reshape(bsz, t, B_KV, B_GROUP, 3).transpose(0, 4, 3, 2, 1).reshape(bsz, 3, Q_COLS)
    gl = jnp.pad(gl, ((0, 0), (0, 5), (0, 0)))
    n_blk = past // SEL_BLOCK + 8
    per_b = lambda shape: pl.BlockSpec((1,) + shape, lambda b, i, pt: (b, 0, 0))
    page_specs = [pl.BlockSpec((1, 2 * KV_HALF, PAGE_SIZE), functools.partial(
        lambda b, i, pt, j: (pt[b, i * n_pages_step + j], 0, 0), j=j)) for j in range(n_pages_step)]
    kern = functools.partial(_sample_attn_kernel, n_pages_step=n_pages_step, past=past, t_new=t, n_keep=n_keep)
    out = pl.pallas_call(
        kern,
        grid_spec=pltpu.PrefetchScalarGridSpec(
            num_scalar_prefetch=1,
            grid=(bsz, n_pages // n_pages_step),
            in_specs=[per_b((KV_HALF, Q_COLS)), per_b((8, Q_COLS)), per_b(kc.shape[1:]), per_b(vc.shape[1:]),
                      per_b((n_keep, 2 * KV_HALF)), per_b((t, 2 * KV_HALF)), per_b((t, 2 * KV_HALF))] + page_specs,
            out_specs=pl.BlockSpec((1, HEAD_DIM, Q_COLS), lambda b, i, pt: (b, 0, 0)),
            scratch_shapes=[pltpu.VMEM((n_blk, Q_COLS), F32), pltpu.VMEM((1, Q_COLS), F32),
                            pltpu.VMEM((1, Q_COLS), F32), pltpu.VMEM((KV_HALF, Q_COLS), F32),
                            pltpu.VMEM((KV_HALF, Q_COLS), F32)]),
        out_shape=jax.ShapeDtypeStruct((bsz, HEAD_DIM, Q_COLS), F32),
        compiler_params=_params("parallel", "arbitrary"),
        name="nsa_sample_attn",
    )(page_table, qbd, gl, kc, vc, state_kv_win, kv_sel_new, kv_win_new, *([cache_kv_sel] * n_pages_step))
    return out.reshape(bsz, HEAD_DIM, B_GROUP, B_KV, t).transpose(0, 4, 3, 2, 1).reshape(bsz, t, N_Q_COLS)


def _tile(n, pref):
    return pref if n % pref == 0 else n


def _run_group(x, s0, is_prompt, caches, a_w_in, a_lb_logits, a_norm_g, a_w_out, b_w_qg, b_w_out, kv_w,
               cmp_pe, cmp_w1, cmp_b1, cmp_w2, ffn_w_in, ffn_w_out, moe_w_router, moe_b_router,
               moe_w_in, moe_w_out, ln_g, ln_b):
    bsz, t, d = x.shape
    n = bsz * t
    if is_prompt:
        h, s_out = hgrn_layer(x, s0, a_w_in[0], a_lb_logits, a_norm_g[0], a_w_out[0], ln_g[0, 0], ln_b[0, 0],
                              layer=0, seg=A_CHUNK, n_seg=4, carry=True)
    else:
        h, s_out = hgrn_layer(x, s0, a_w_in[0], a_lb_logits, a_norm_g[0], a_w_out[0], ln_g[0, 0], ln_b[0, 0],
                              layer=0, seg=t, n_seg=8, carry=False)
    h = h.reshape(n, d)
    h = ffn_layer(h, ffn_w_in[0], ffn_w_out[0], ln_g[0, 1], ln_b[0, 1], tm=_tile(n, 1024), tf=256)
    nq = B_HEADS * HEAD_DIM
    if is_prompt:
        kv_cmp, kv_sel, kv_win, qh, ksh, vsh, kwh, vwh, gts = nsa_proj(h.reshape(bsz, t, d), kv_w, b_w_qg[0], tm=512)
        n_ch = t // CMP_STRIDE
        kc, vc = compress(kv_cmp, None, cmp_pe, cmp_w1, cmp_b1, cmp_w2,
                          ch=n_ch, head_major=True)
        o = nsa_prompt_attn(qh, kc, vc, ksh, vsh, kwh, vwh, gts, tq=128, tk=256)
        h = proj_ln(o.reshape(n, nq), h, b_w_out[0], ln_g[1, 0], ln_b[1, 0], tm=_tile(n, 512))
        h = moe_layer(h, moe_w_router[0], moe_b_router[0], moe_w_in[0], moe_w_out[0], ln_g[1, 1], ln_b[1, 1],
                      tm=_tile(n, 2048), tf=896)
        kvshape = (bsz, t, 2, B_KV, HEAD_DIM)
        return (h.reshape(bsz, t, d), s_out[None], kv_cmp.reshape(kvshape), kv_sel.reshape(kvshape),
                kv_win.reshape(kvshape)[:, t - min(WINDOW, t):])
    cache_kv_cmp, cache_kv_sel, state_kv_win, page_table = caches
    n_pool = cache_kv_cmp.shape[0]
    n_keep = state_kv_win.shape[1]
    w_cat = jnp.concatenate([kv_w, b_w_qg[0]], axis=1)
    w_cat = jnp.pad(w_cat, ((0, 0), (0, (-w_cat.shape[1]) % 128)))
    z = proj(h, w_cat, tm=_tile(n, 512))
    kv = z[:, :N_KV_COLS].reshape(bsz, t, 3, KV_SET)
    q = z[:, N_KV_COLS:N_KV_COLS + nq].reshape(bsz, t, nq)
    gate_logits = z[:, N_KV_COLS + nq:N_KV_COLS + nq + 3 * B_HEADS].reshape(bsz, t, 3 * B_HEADS)
    assert (page_table.shape[1] * PAGE_SIZE + t) // CMP_STRIDE == page_table.shape[1] * PAGE_SIZE // CMP_STRIDE
    token_minor = lambda c: c.transpose(0, 2, 3, 4, 1).reshape(n_pool, KV_SET, PAGE_SIZE)
    kc, vc = compress(token_minor(cache_kv_cmp), page_table,
                      cmp_pe, cmp_w1, cmp_b1, cmp_w2, ch=128, head_major=False)
    o = nsa_sample_attn(q, gate_logits, kc, vc, token_minor(cache_kv_sel),
                        state_kv_win.reshape(bsz, n_keep, KV_SET), kv[:, :, 1], kv[:, :, 2], page_table,
                        n_pages_step=16)
    h = proj_ln(o.reshape(n, nq), h, b_w_out[0], ln_g[1, 0], ln_b[1, 0], tm=_tile(n, 512))
    h = moe_layer(h, moe_w_router[0], moe_b_router[0], moe_w_in[0], moe_w_out[0], ln_g[1, 1], ln_b[1, 1],
                  tm=_tile(n, 2048), tf=896)
    kvshape = (bsz, t, 2, B_KV, HEAD_DIM)
    win_all = jnp.concatenate([state_kv_win, kv[:, :, 2].reshape(kvshape)], axis=1)
    return (h.reshape(bsz, t, d), s_out[None], kv[:, :, 0].reshape(kvshape), kv[:, :, 1].reshape(kvshape),
            win_all[:, -n_keep:])


def kernel(x_prompt, x_sample, state_hgrn, cache_kv_cmp, cache_kv_sel, state_kv_win, page_table,
           a_w_in, a_lb_logits, a_norm_g, a_w_out, b_w_qg, b_w_out, kv_w,
           cmp_pe, cmp_w1, cmp_b1, cmp_w2, ffn_w_in, ffn_w_out,
           moe_w_router, moe_b_router, moe_w_in, moe_w_out, ln_g, ln_b):
    weights = (a_w_in, a_lb_logits, a_norm_g, a_w_out, b_w_qg, b_w_out, kv_w, cmp_pe, cmp_w1, cmp_b1, cmp_w2,
               ffn_w_in, ffn_w_out, moe_w_router, moe_b_router, moe_w_in, moe_w_out, ln_g, ln_b)
    hgrn0 = jnp.zeros((x_prompt.shape[0], A_HEADS, A_DK, A_DV), F32)
    y_p, hg_p, cmp_p, sel_p, win_p = _run_group(x_prompt, hgrn0, True, None, *weights)
    y_s, hg_s, cmp_s, sel_s, win_s = _run_group(
        x_sample, state_hgrn[0], False, (cache_kv_cmp, cache_kv_sel, state_kv_win, page_table), *weights)
    return (y_p, y_s, hg_p, cmp_p, sel_p, win_p, hg_s, cmp_s, sel_s, win_s)
```

```python
import functools

import jax
import jax.numpy as jnp
from jax import lax
from jax.experimental import pallas as pl
from jax.experimental.pallas import tpu as pltpu

F32 = jnp.float32
BF16 = jnp.bfloat16

D_MODEL = 1024
DEPTH = 2
ALPHA = (2.0 * DEPTH) ** 0.25
LN_EPS = 1e-5
RMS_EPS = 1e-6
NEG_BIG = -1e30
FORCE = 1e6
PAGE_SIZE = 128

A_DK = 128
A_HEADS = D_MODEL // A_DK
A_DV = D_MODEL // A_HEADS
A_WIDTH = A_HEADS * A_DK
A_CHUNK = 64

B_HEADS = 16
B_KV = 4
B_GROUP = B_HEADS // B_KV
HEAD_DIM = D_MODEL // B_HEADS
SCALE = HEAD_DIM ** -0.5
CMP_STRIDE = 16
CMP_BLOCK = 2 * CMP_STRIDE
SEL_BLOCK = 64
TOP_N = 8
N_LOCAL = 2
WINDOW = 512

D_FF = 256 * ((8 * D_MODEL // 3 + 255) // 256)
N_EXPERTS = 8
MOE_TOP_K = 2
D_FF_E = 7 * D_MODEL // 2

VMEM_LIMIT_BYTES = 56 * 1024 * 1024


def _params(*sem):
    return pltpu.CompilerParams(dimension_semantics=sem, vmem_limit_bytes=VMEM_LIMIT_BYTES)


def _silu(x):
    return x * (1.0 / (1.0 + jnp.exp(-x)))


def _sigmoid(x):
    return 1.0 / (1.0 + jnp.exp(-x))


def _layer_norm(x, g, b):
    xc = x - jnp.mean(x, -1, keepdims=True)
    var = jnp.mean(xc * xc, -1, keepdims=True)
    return xc * lax.rsqrt(var + LN_EPS) * g + b


def _dot(a, b):
    return jnp.dot(a.astype(BF16), b.astype(BF16), preferred_element_type=F32)


def _dot_nt(a, b):
    return lax.dot_general(a.astype(BF16), b.astype(BF16), (((1,), (1,)), ((), ())),
                           preferred_element_type=F32)


def _dot_tn(a, b):
    return lax.dot_general(a.astype(BF16), b.astype(BF16), (((0,), (0,)), ((), ())),
                           preferred_element_type=F32)


def _hgrn_kernel(x_ref, s0_ref, win_ref, lbl_ref, ng_ref, wout_ref, lng_ref, lnb_ref,
                 h_ref, sout_ref, st_ref, *, layer, seg, n_seg, carry):
    rows = seg * n_seg
    x = x_ref[...].reshape(rows, D_MODEL)
    z = jnp.dot(x.astype(BF16), win_ref[...], preferred_element_type=F32)
    zq = z[:, 0 * A_WIDTH:1 * A_WIDTH]
    zf = z[:, 1 * A_WIDTH:2 * A_WIDTH]
    v = z[:, 2 * A_WIDTH:3 * A_WIDTH]
    zg = z[:, 3 * A_WIDTH:4 * A_WIDTH]

    lbl = lbl_ref[...]
    e = jnp.exp(lbl - jnp.max(lbl, axis=0, keepdims=True))
    lb = jnp.sum(e[:layer + 1], axis=0, keepdims=True) / jnp.sum(e, axis=0, keepdims=True)

    q = _silu(zq)
    f = lb + (1.0 - lb) * _sigmoid(zf)
    logf = jnp.log(f)
    k = 1.0 - f

    r_i = lax.broadcasted_iota(jnp.int32, (seg, seg), 0)
    c_i = lax.broadcasted_iota(jnp.int32, (seg, seg), 1)
    causal = c_i <= r_i
    tri = jnp.where(causal, 1.0, 0.0).astype(BF16)

    if carry:
        @pl.when(pl.program_id(1) == 0)
        def _():
            for h in range(A_HEADS):
                st_ref[h] = s0_ref[0, h].T

    o_parts = []
    for s in range(n_seg):
        sl = slice(s * seg, (s + 1) * seg)
        lf = logf[sl]
        lf_hi = lf.astype(BF16)
        lf_lo = (lf - lf_hi.astype(F32)).astype(BF16)
        g = jnp.dot(tri, lf_hi, preferred_element_type=F32) + jnp.dot(tri, lf_lo, preferred_element_type=F32)
        glast = g[seg - 1:seg, :]
        qg = q[sl] * jnp.exp(g)
        kg = k[sl] * jnp.exp(-g)
        kd = k[sl] * jnp.exp(glast - g)
        eg = jnp.exp(glast)
        vs = v[sl]
        heads = []
        for h in range(A_HEADS):
            cl = slice(h * A_DK, (h + 1) * A_DK)
            if carry:
                st = st_ref[h]
            else:
                st = s0_ref[s, h].T
            att = jnp.where(causal, _dot_nt(qg[:, cl], kg[:, cl]), 0.0)
            o = _dot_nt(qg[:, cl], st) + _dot(att, vs[:, cl])
            st_new = eg[:, cl] * st + _dot_tn(vs[:, cl], kd[:, cl])
            if carry:
                st_ref[h] = st_new
            else:
                sout_ref[s, h] = st_new.T
            o = o * lax.rsqrt(jnp.mean(o * o, -1, keepdims=True) + RMS_EPS)
            heads.append(o)
        o_parts.append(jnp.concatenate(heads, axis=1))
    o = o_parts[0] if n_seg == 1 else jnp.concatenate(o_parts, axis=0)
    o = o * ng_ref[...] * _silu(zg)
    y = jnp.dot(o.astype(BF16), wout_ref[...], preferred_element_type=F32)
    hh = _layer_norm(ALPHA * x + y, lng_ref[...], lnb_ref[...])
    h_ref[...] = hh.reshape(h_ref.shape)

    if carry:
        @pl.when(pl.program_id(1) == pl.num_programs(1) - 1)
        def _():
            for h in range(A_HEADS):
                sout_ref[0, h] = st_ref[h].T


def hgrn_layer(x, s0, w_in, lb_logits, norm_g, w_out, ln_g, ln_b, *, layer, seg, n_seg, carry):
    bsz, t, _ = x.shape
    row2 = lambda a: a.reshape(1, -1).astype(F32)
    w_in = w_in.astype(BF16)
    w_out = w_out.astype(BF16)
    const = lambda *_: (0, 0)
    if carry:
        tile = seg * n_seg
        grid = (bsz, t // tile)
        x_spec = pl.BlockSpec((1, tile, D_MODEL), lambda b, c: (b, c, 0))
        s_spec = pl.BlockSpec((1, A_HEADS, A_DK, A_DV), lambda b, c: (b, 0, 0, 0))
        sem = ("parallel", "arbitrary")
    else:
        assert t == seg
        grid = (bsz // n_seg, 1)
        x_spec = pl.BlockSpec((n_seg, seg, D_MODEL), lambda b, c: (b, 0, 0))
        s_spec = pl.BlockSpec((n_seg, A_HEADS, A_DK, A_DV), lambda b, c: (b, 0, 0, 0))
        sem = ("parallel", "arbitrary")
    kern = functools.partial(_hgrn_kernel, layer=layer, seg=seg, n_seg=n_seg, carry=carry)
    return pl.pallas_call(
        kern,
        grid=grid,
        in_specs=[
            x_spec, s_spec,
            pl.BlockSpec((D_MODEL, 4 * A_WIDTH), const),
            pl.BlockSpec(lb_logits.shape, const),
            pl.BlockSpec((1, A_WIDTH), const),
            pl.BlockSpec((A_WIDTH, D_MODEL), const),
            pl.BlockSpec((1, D_MODEL), const),
            pl.BlockSpec((1, D_MODEL), const),
        ],
        out_specs=[x_spec, s_spec],
        out_shape=[jax.ShapeDtypeStruct(x.shape, F32), jax.ShapeDtypeStruct(s0.shape, F32)],
        scratch_shapes=[pltpu.VMEM((A_HEADS, A_DV, A_DK), F32)],
        compiler_params=_params(*sem),
        name="hgrn_layer",
    )(x, s0, w_in, lb_logits.astype(F32), row2(norm_g), w_out, row2(ln_g), row2(ln_b))


def _ffn_kernel(x_ref, wa_ref, wu_ref, wo_ref, lng_ref, lnb_ref, o_ref, acc_ref):
    j = pl.program_id(1)
    xb = x_ref[...].astype(BF16)
    a = jnp.dot(xb, wa_ref[...], preferred_element_type=F32)
    u = jnp.dot(xb, wu_ref[...], preferred_element_type=F32)
    part = jnp.dot((_silu(a) * u).astype(BF16), wo_ref[...], preferred_element_type=F32)

    @pl.when(j == 0)
    def _():
        acc_ref[...] = part

    @pl.when(j > 0)
    def _():
        acc_ref[...] += part

    @pl.when(j == pl.num_programs(1) - 1)
    def _():
        o_ref[...] = _layer_norm(ALPHA * x_ref[...] + acc_ref[...], lng_ref[...], lnb_ref[...])


def ffn_layer(x, w_in, w_out, ln_g, ln_b, *, tm, tf):
    n, d = x.shape
    d_ff = w_out.shape[0]
    nf = d_ff // tf
    assert n % tm == 0 and d_ff % tf == 0
    w_in = w_in.astype(BF16)
    w_out = w_out.astype(BF16)
    row2 = lambda a: a.reshape(1, -1).astype(F32)
    return pl.pallas_call(
        _ffn_kernel,
        grid=(n // tm, nf),
        in_specs=[
            pl.BlockSpec((tm, d), lambda i, j: (i, 0)),
            pl.BlockSpec((d, tf), lambda i, j: (0, j)),
            pl.BlockSpec((d, tf), lambda i, j: (0, j + nf)),
            pl.BlockSpec((tf, d), lambda i, j: (j, 0)),
            pl.BlockSpec((1, d), lambda i, j: (0, 0)),
            pl.BlockSpec((1, d), lambda i, j: (0, 0)),
        ],
        out_specs=pl.BlockSpec((tm, d), lambda i, j: (i, 0)),
        out_shape=jax.ShapeDtypeStruct((n, d), F32),
        scratch_shapes=[pltpu.VMEM((tm, d), F32)],
        compiler_params=_params("parallel", "arbitrary"),
        name="ffn_layer",
    )(x, w_in, w_in, w_out, row2(ln_g), row2(ln_b))


def _router_gate(x, wr, br):
    logits = jnp.dot(x, wr, preferred_element_type=F32, precision=lax.Precision.HIGHEST) + br
    lane = lax.broadcasted_iota(jnp.int32, logits.shape, 1)
    m1 = jnp.max(logits, axis=-1, keepdims=True)
    i1 = jnp.min(jnp.where(logits == m1, lane, N_EXPERTS), axis=-1, keepdims=True)
    rest = jnp.where(lane == i1, -jnp.inf, logits)
    m2 = jnp.max(rest, axis=-1, keepdims=True)
    i2 = jnp.min(jnp.where(rest == m2, lane, N_EXPERTS), axis=-1, keepdims=True)
    e2 = jnp.exp(m2 - m1)
    den = 1.0 + e2
    return jnp.where(lane == i1, 1.0 / den, 0.0) + jnp.where(lane == i2, e2 / den, 0.0)


def _moe_kernel(x_ref, wr_ref, br_ref, wa_ref, wu_ref, wo_ref, lng_ref, lnb_ref, o_ref,
                gate_ref, acc_ref, tot_ref):
    e = pl.program_id(1)
    j = pl.program_id(2)
    nj = pl.num_programs(2)

    @pl.when((e == 0) & (j == 0))
    def _():
        gate_ref[...] = _router_gate(x_ref[...], wr_ref[...], br_ref[...])
        tot_ref[...] = jnp.zeros_like(tot_ref)

    xb = x_ref[...].astype(BF16)
    a = jnp.dot(xb, wa_ref[0], preferred_element_type=F32)
    u = jnp.dot(xb, wu_ref[0], preferred_element_type=F32)
    part = jnp.dot((_silu(a) * u).astype(BF16), wo_ref[0], preferred_element_type=F32)

    @pl.when(j == 0)
    def _():
        acc_ref[...] = part

    @pl.when(j > 0)
    def _():
        acc_ref[...] += part

    @pl.when(j == nj - 1)
    def _():
        gate = gate_ref[...]
        lane = lax.broadcasted_iota(jnp.int32, gate.shape, 1)
        ge = jnp.sum(jnp.where(lane == e, gate, 0.0), axis=-1, keepdims=True)
        tot_ref[...] += ge * acc_ref[...]

    @pl.when((j == nj - 1) & (e == pl.num_programs(1) - 1))
    def _():
        o_ref[...] = _layer_norm(ALPHA * x_ref[...] + tot_ref[...], lng_ref[...], lnb_ref[...])


def moe_layer_dense(x, w_router, b_router, w_in, w_out, ln_g, ln_b, *, tm, tf):
    n, d = x.shape
    ne, d_ff = w_out.shape[0], w_out.shape[1]
    nf = d_ff // tf
    assert n % tm == 0 and d_ff % tf == 0
    w_in = w_in.astype(BF16)
    w_out = w_out.astype(BF16)
    row2 = lambda a: a.reshape(1, -1).astype(F32)
    c2 = lambda i, e, j: (0, 0)
    return pl.pallas_call(
        _moe_kernel,
        grid=(n // tm, ne, nf),
        in_specs=[
            pl.BlockSpec((tm, d), lambda i, e, j: (i, 0)),
            pl.BlockSpec((d, ne), c2),
            pl.BlockSpec((1, ne), c2),
            pl.BlockSpec((1, d, tf), lambda i, e, j: (e, 0, j)),
            pl.BlockSpec((1, d, tf), lambda i, e, j: (e, 0, j + nf)),
            pl.BlockSpec((1, tf, d), lambda i, e, j: (e, j, 0)),
            pl.BlockSpec((1, d), c2),
            pl.BlockSpec((1, d), c2),
        ],
        out_specs=pl.BlockSpec((tm, d), lambda i, e, j: (i, 0)),
        out_shape=jax.ShapeDtypeStruct((n, d), F32),
        scratch_shapes=[pltpu.VMEM((tm, ne), F32), pltpu.VMEM((tm, d), F32), pltpu.VMEM((tm, d), F32)],
        compiler_params=_params("parallel", "arbitrary", "arbitrary"),
        name="moe_layer",
    )(x, w_router.astype(F32), row2(b_router), w_in, w_in, w_out, row2(ln_g), row2(ln_b))


MOE_RB = 256
MOE_TAIL = 128
MOE_ST = 256


def _router_kernel(x_ref, wrt_ref, br_ref, gate_ref, xb_ref):
    x = x_ref[...]
    logits = lax.dot_general(wrt_ref[...], x, (((1,), (1,)), ((), ())), preferred_element_type=F32,
                             precision=lax.Precision.HIGHEST) + br_ref[...]
    eidx = lax.broadcasted_iota(jnp.int32, logits.shape, 0)
    m1 = jnp.max(logits, axis=0, keepdims=True)
    i1 = jnp.min(jnp.where(logits == m1, eidx, N_EXPERTS), axis=0, keepdims=True)
    rest = jnp.where(eidx == i1, -jnp.inf, logits)
    m2 = jnp.max(rest, axis=0, keepdims=True)
    i2 = jnp.min(jnp.where(rest == m2, eidx, N_EXPERTS), axis=0, keepdims=True)
    e2 = jnp.exp(m2 - m1)
    den = 1.0 + e2
    gate_ref[...] = jnp.where(eidx == i1, 1.0 / den, 0.0) + jnp.where(eidx == i2, e2 / den, 0.0)
    xb_ref[...] = x.astype(BF16)


def moe_router(x, w_router, b_router, *, tm):
    n, d = x.shape
    return pl.pallas_call(
        _router_kernel,
        grid=(n // tm,),
        in_specs=[pl.BlockSpec((tm, d), lambda i: (i, 0)), pl.BlockSpec((N_EXPERTS, d), lambda i: (0, 0)),
                  pl.BlockSpec((N_EXPERTS, 1), lambda i: (0, 0))],
        out_specs=[pl.BlockSpec((N_EXPERTS, tm), lambda i: (0, i)), pl.BlockSpec((tm, d), lambda i: (i, 0))],
        out_shape=[jax.ShapeDtypeStruct((N_EXPERTS, n), F32), jax.ShapeDtypeStruct((n, d), BF16)],
        compiler_params=_params("parallel"),
        name="moe_router",
    )(x, w_router.T.astype(F32), b_router.reshape(N_EXPERTS, 1).astype(F32))


def _moe_routed_kernel(cum_ref, xb_ref, gate_ref, rank_ref, wa_ref, wu_ref, wo_ref, y_ref, xs_ref, acc_ref,
                       *, n_sub):
    i = pl.program_id(0)
    e = pl.program_id(1)
    j = pl.program_id(2)
    nj = pl.num_programs(2)
    base = (i * N_EXPERTS + e) * (n_sub + 1)
    count = cum_ref[base + n_sub]
    n_full = (count + (MOE_RB - MOE_TAIL - 1)) // MOE_RB

    def for_blocks(fn):
        def body(b, _):
            fn(pl.multiple_of(b * MOE_RB, MOE_RB), MOE_RB)
            return 0

        lax.fori_loop(0, n_full, body, 0)

        @pl.when(count > n_full * MOE_RB)
        def _():
            fn(pl.multiple_of(n_full * MOE_RB, MOE_RB), MOE_TAIL)

    def for_subtiles(r0, rb, fn):
        def body(c, _):
            @pl.when((cum_ref[base + c] < r0 + rb) & (cum_ref[base + c + 1] > r0))
            def _():
                fn(c)
            return 0

        lax.fori_loop(0, n_sub, body, 0)

    def one_hot(r0, rb, c):
        c0 = pl.multiple_of(c * MOE_ST, MOE_ST)
        rank = rank_ref[pl.ds(e, 1), pl.ds(c0, MOE_ST)]
        gate = gate_ref[pl.ds(e, 1), pl.ds(c0, MOE_ST)]
        hit = (rank == r0 + lax.broadcasted_iota(jnp.int32, (rb, MOE_ST), 0)) & (gate > 0.0)
        return hit, gate, c0

    def gather_block(r0, rb):
        rows = pl.ds(r0, rb)
        xs_ref[rows, :] = jnp.zeros((rb, D_MODEL), BF16)

        def sub(c):
            hit, _, c0 = one_hot(r0, rb, c)
            part = jnp.dot(jnp.where(hit, 1.0, 0.0).astype(BF16), xb_ref[pl.ds(c0, MOE_ST), :],
                           preferred_element_type=F32)
            xs_ref[rows, :] = xs_ref[rows, :] + part.astype(BF16)

        for_subtiles(r0, rb, sub)

    def ffn_block(r0, rb):
        rows = pl.ds(r0, rb)
        xs = xs_ref[rows, :]
        a = jnp.dot(xs, wa_ref[0], preferred_element_type=F32)
        u = jnp.dot(xs, wu_ref[0], preferred_element_type=F32)
        part = jnp.dot((_silu(a) * u).astype(BF16), wo_ref[0], preferred_element_type=F32)

        @pl.when(j == 0)
        def _():
            acc_ref[rows, :] = part

        @pl.when(j > 0)
        def _():
            acc_ref[rows, :] = acc_ref[rows, :] + part

    def scatter_block(r0, rb):
        rows = pl.ds(r0, rb)

        def sub(c):
            hit, gate, c0 = one_hot(r0, rb, c)
            g_rows = jnp.sum(jnp.where(hit, gate, 0.0), axis=1, keepdims=True)
            out = (acc_ref[rows, :] * g_rows).astype(BF16)
            y_ref[pl.ds(c0, MOE_ST), :] += _dot_tn(jnp.where(hit, 1.0, 0.0).astype(BF16), out)

        for_subtiles(r0, rb, sub)

    @pl.when((e == 0) & (j == 0))
    def _():
        y_ref[...] = jnp.zeros_like(y_ref)

    @pl.when(j == 0)
    def _():
        for_blocks(gather_block)

    for_blocks(ffn_block)

    @pl.when(j == nj - 1)
    def _():
        for_blocks(scatter_block)


def moe_routed(xb, gate_t, w_in, w_out, *, tm, tf):
    n, d = xb.shape
    ne, d_ff = w_out.shape[0], w_out.shape[1]
    nf = d_ff // tf
    n_tiles, n_sub = n // tm, tm // MOE_ST
    assert n % tm == 0 and tm % MOE_ST == 0 and d_ff % tf == 0 and tm % MOE_RB == 0
    mask = (gate_t > 0.0).astype(jnp.int32).reshape(ne, n_tiles, tm)
    rank = (jnp.cumsum(mask, axis=-1) - mask).reshape(ne, n)
    cnt = mask.reshape(ne, n_tiles, n_sub, MOE_ST).sum(-1)
    cum = jnp.concatenate([jnp.zeros((ne, n_tiles, 1), jnp.int32), jnp.cumsum(cnt, axis=-1)], axis=-1)
    cum = cum.transpose(1, 0, 2).reshape(-1).astype(jnp.int32)
    w_in = w_in.astype(BF16)
    w_out = w_out.astype(BF16)
    kern = functools.partial(_moe_routed_kernel, n_sub=n_sub)
    return pl.pallas_call(
        kern,
        grid_spec=pltpu.PrefetchScalarGridSpec(
            num_scalar_prefetch=1,
            grid=(n_tiles, ne, nf),
            in_specs=[
                pl.BlockSpec((tm, d), lambda i, e, j, c: (i, 0)),
                pl.BlockSpec((ne, tm), lambda i, e, j, c: (0, i)),
                pl.BlockSpec((ne, tm), lambda i, e, j, c: (0, i)),
                pl.BlockSpec((1, d, tf), lambda i, e, j, c: (e, 0, j)),
                pl.BlockSpec((1, d, tf), lambda i, e, j, c: (e, 0, j + nf)),
                pl.BlockSpec((1, tf, d), lambda i, e, j, c: (e, j, 0)),
            ],
            out_specs=pl.BlockSpec((tm, d), lambda i, e, j, c: (i, 0)),
            scratch_shapes=[pltpu.VMEM((tm, d), BF16), pltpu.VMEM((tm, d), F32)]),
        out_shape=jax.ShapeDtypeStruct((n, d), F32),
        compiler_params=_params("parallel", "arbitrary", "arbitrary"),
        name="moe_routed",
    )(cum, xb, gate_t, rank, w_in, w_in, w_out)


def _add_ln_kernel(x_ref, y_ref, lng_ref, lnb_ref, o_ref):
    o_ref[...] = _layer_norm(ALPHA * x_ref[...] + y_ref[...], lng_ref[...], lnb_ref[...])


def add_ln(x, y, ln_g, ln_b, *, tm):
    n, d = x.shape
    row2 = lambda a: a.reshape(1, -1).astype(F32)
    blk = pl.BlockSpec((tm, d), lambda i: (i, 0))
    c2 = pl.BlockSpec((1, d), lambda i: (0, 0))
    return pl.pallas_call(
        _add_ln_kernel, grid=(n // tm,), in_specs=[blk, blk, c2, c2], out_specs=blk,
        out_shape=jax.ShapeDtypeStruct((n, d), F32), compiler_params=_params("parallel"), name="add_ln",
    )(x, y, row2(ln_g), row2(ln_b))


def moe_layer(x, w_router, b_router, w_in, w_out, ln_g, ln_b, *, tm, tf):
    n = x.shape[0]
    gate_t, xb = moe_router(x, w_router, b_router, tm=_tile(n, 1024))
    y = moe_routed(xb, gate_t, w_in, w_out, tm=tm, tf=tf)
    return add_ln(x, y, ln_g, ln_b, tm=_tile(n, 1024))


def _proj_kernel(x_ref, w_ref, o_ref):
    o_ref[...] = jnp.dot(x_ref[...].astype(BF16), w_ref[...], preferred_element_type=F32)


def proj(x, w, *, tm):
    n, d = x.shape
    m = w.shape[1]
    return pl.pallas_call(
        _proj_kernel,
        grid=(n // tm,),
        in_specs=[pl.BlockSpec((tm, d), lambda i: (i, 0)), pl.BlockSpec((d, m), lambda i: (0, 0))],
        out_specs=pl.BlockSpec((tm, m), lambda i: (i, 0)),
        out_shape=jax.ShapeDtypeStruct((n, m), F32),
        compiler_params=_params("parallel"),
        name="proj",
    )(x, w.astype(BF16))


def _proj_ln_kernel(o_ref, x_ref, w_ref, lng_ref, lnb_ref, h_ref):
    y = jnp.dot(o_ref[...].astype(BF16), w_ref[...], preferred_element_type=F32)
    h_ref[...] = _layer_norm(ALPHA * x_ref[...] + y, lng_ref[...], lnb_ref[...])


def proj_ln(o, x, w, ln_g, ln_b, *, tm):
    n, d = x.shape
    row2 = lambda a: a.reshape(1, -1).astype(F32)
    c2 = lambda i: (0, 0)
    return pl.pallas_call(
        _proj_ln_kernel,
        grid=(n // tm,),
        in_specs=[pl.BlockSpec((tm, o.shape[1]), lambda i: (i, 0)), pl.BlockSpec((tm, d), lambda i: (i, 0)),
                  pl.BlockSpec(w.shape, c2), pl.BlockSpec((1, d), c2), pl.BlockSpec((1, d), c2)],
        out_specs=pl.BlockSpec((tm, d), lambda i: (i, 0)),
        out_shape=jax.ShapeDtypeStruct((n, d), F32),
        compiler_params=_params("parallel"),
        name="proj_ln",
    )(o, x, w.astype(BF16), row2(ln_g), row2(ln_b))


N_KV_COLS = 6 * B_KV * HEAD_DIM
KV_SET = 2 * B_KV * HEAD_DIM
N_Q_COLS = B_HEADS * HEAD_DIM
GATE_LANES = 128
LOG2_E = 1.4426950408889634


def _nsa_proj_kernel(x_ref, w_ref, cmp_ref, sel_ref, win_ref, q_ref, ks_ref, vs_ref, kw_ref, vw_ref, g_ref):
    z = jnp.dot(x_ref[0].astype(BF16), w_ref[...], preferred_element_type=F32)
    cmp_ref[0] = z[:, 0:KV_SET]
    sel_ref[0] = z[:, KV_SET:2 * KV_SET]
    win_ref[0] = z[:, 2 * KV_SET:3 * KV_SET]
    half = B_KV * HEAD_DIM
    for g in range(B_KV):
        lo = KV_SET + g * HEAD_DIM
        ks_ref[0, g] = z[:, lo:lo + HEAD_DIM].astype(BF16)
        vs_ref[0, g] = z[:, lo + half:lo + half + HEAD_DIM].astype(BF16)
        lo = 2 * KV_SET + g * HEAD_DIM
        kw_ref[0, g] = z[:, lo:lo + HEAD_DIM].astype(BF16)
        vw_ref[0, g] = z[:, lo + half:lo + half + HEAD_DIM].astype(BF16)
    for h in range(B_HEADS):
        lo = N_KV_COLS + h * HEAD_DIM
        q_ref[0, h] = (z[:, lo:lo + HEAD_DIM] * (SCALE * LOG2_E)).astype(BF16)
    g_ref[0] = _sigmoid(z[:, N_KV_COLS + N_Q_COLS:N_KV_COLS + N_Q_COLS + GATE_LANES])


def nsa_proj(h, kv_w, w_qg, *, tm):
    bsz, t, d = h.shape
    w = jnp.concatenate([kv_w, w_qg], axis=1)
    w = jnp.pad(w, ((0, 0), (0, N_KV_COLS + N_Q_COLS + GATE_LANES - w.shape[1]))).astype(BF16)
    row = lambda: pl.BlockSpec((1, tm, KV_SET), lambda b, i: (b, i, 0))
    hm = lambda nh: pl.BlockSpec((1, nh, tm, HEAD_DIM), lambda b, i: (b, 0, i, 0))
    sds = jax.ShapeDtypeStruct
    return pl.pallas_call(
        _nsa_proj_kernel,
        grid=(bsz, t // tm),
        in_specs=[pl.BlockSpec((1, tm, d), lambda b, i: (b, i, 0)), pl.BlockSpec(w.shape, lambda b, i: (0, 0))],
        out_specs=[row(), row(), row(), hm(B_HEADS), hm(B_KV), hm(B_KV), hm(B_KV), hm(B_KV),
                   pl.BlockSpec((1, tm, GATE_LANES), lambda b, i: (b, i, 0))],
        out_shape=[sds((bsz, t, KV_SET), F32)] * 3 + [sds((bsz, B_HEADS, t, HEAD_DIM), BF16)]
        + [sds((bsz, B_KV, t, HEAD_DIM), BF16)] * 4 + [sds((bsz, t, GATE_LANES), F32)],
        compiler_params=_params("parallel", "parallel"),
        name="nsa_proj",
    )(h, w)


PAIR = 2 * HEAD_DIM


def _compress_kernel(*refs, n_in, head_major, paged):
    if paged:
        refs = refs[1:]
    x_refs = refs[:n_in]
    wp_ref, u_ref, b1_ref, w2_ref, kc_ref, vc_ref, carry_ref = refs[n_in:]
    i = pl.program_id(1)

    @pl.when(i == 0)
    def _():
        carry_ref[...] = jnp.zeros_like(carry_ref)

    cpp = PAGE_SIZE // CMP_STRIDE
    pr = lax.broadcasted_iota(jnp.int32, (PAGE_SIZE, PAGE_SIZE), 0)
    pc = lax.broadcasted_iota(jnp.int32, (PAGE_SIZE, PAGE_SIZE), 1)
    perm = jnp.where(pc == CMP_STRIDE * (pr % cpp) + pr // cpp, 1.0, 0.0).astype(BF16)
    pages = []
    for r in x_refs:
        if paged:
            pages.append(_dot_nt(perm, r[0]))
        else:
            for p0 in range(0, r.shape[1], PAGE_SIZE):
                pages.append(jnp.dot(perm, r[0, p0:p0 + PAGE_SIZE, :].astype(BF16), preferred_element_type=F32))
    ch = len(pages) * cpp
    row0 = lax.broadcasted_iota(jnp.int32, (ch, PAIR), 0) == 0

    def chunk_rows(s, lo):
        parts = [pg[s * cpp:(s + 1) * cpp, lo:lo + PAIR] for pg in pages]
        return parts[0] if len(parts) == 1 else jnp.concatenate(parts, axis=0)
    for k in range(2):
        wp = wp_ref[k]
        pbm = jnp.dot(u_ref[k], wp, preferred_element_type=F32)
        pb = pbm[0:1, 0:PAIR] + pbm[1:2, PAIR:2 * PAIR] + b1_ref[k]
        for gp in range(B_KV // 2):
            base = k * B_KV * HEAD_DIM + gp * PAIR
            lhs = jnp.concatenate([chunk_rows(s, base) for s in range(CMP_STRIDE)], axis=1).astype(BF16)
            r = jnp.dot(lhs, wp, preferred_element_type=F32)
            first, second = r[:, 0:PAIR], r[:, PAIR:2 * PAIR]
            slot = k * (B_KV // 2) + gp
            prev = carry_ref[slot]
            shifted = jnp.where(row0, prev[7:8, :], pltpu.roll(first, 1, axis=0))
            carry_ref[slot] = first[ch - 8:ch, :]
            hid = jax.nn.gelu(shifted + second + pb)
            out = jnp.dot(hid.astype(BF16), w2_ref[k], preferred_element_type=F32).astype(BF16)
            dst = kc_ref if k == 0 else vc_ref
            if head_major:
                for g2 in range(2):
                    dst[0, 2 * gp + g2] = out[:, g2 * HEAD_DIM:(g2 + 1) * HEAD_DIM]
            else:
                dst[0, :, gp * PAIR:(gp + 1) * PAIR] = out


def _compress_weights(cmp_pe, cmp_w1, cmp_b1, cmp_w2):
    eye2 = jnp.eye(2, dtype=F32)
    w1 = cmp_w1.reshape(2, CMP_STRIDE, 2, HEAD_DIM, HEAD_DIM)
    wp = jnp.einsum('fskdh,ab->ksadfbh', w1, eye2).reshape(2, CMP_STRIDE * PAIR, 2 * PAIR)
    pe = cmp_pe.reshape(2, CMP_STRIDE, 2, HEAD_DIM)
    u = jnp.broadcast_to(pe.transpose(2, 0, 1, 3)[:, :, :, None, :], (2, 2, CMP_STRIDE, 2, HEAD_DIM))
    u = jnp.pad(u.reshape(2, 2, CMP_STRIDE * PAIR), ((0, 0), (0, 6), (0, 0)))
    b1 = jnp.tile(cmp_b1, (1, 2)).reshape(2, 1, PAIR)
    w2 = jnp.einsum('khd,ab->kahbd', cmp_w2, eye2).reshape(2, PAIR, PAIR)
    return wp.astype(BF16), u.astype(BF16), b1.astype(F32), w2.astype(BF16)


def compress(x, page_table, cmp_pe, cmp_w1, cmp_b1, cmp_w2, *, ch, head_major):
    wp, u, b1, w2 = _compress_weights(cmp_pe, cmp_w1, cmp_b1, cmp_w2)
    paged = page_table is not None
    if paged:
        bsz, n_pages = page_table.shape
        n_in = ch * CMP_STRIDE // PAGE_SIZE
        n_chunks = n_pages * PAGE_SIZE // CMP_STRIDE
        x_specs = [pl.BlockSpec((1, KV_SET, PAGE_SIZE), functools.partial(
            lambda b, i, pt, j: (pt[b, i * n_in + j], 0, 0), j=j)) for j in range(n_in)]
        cm = lambda f: (lambda b, i, pt: f(b, i))
    else:
        bsz, t, _ = x.shape
        n_chunks = t // CMP_STRIDE
        n_in = 1
        x_specs = [pl.BlockSpec((1, ch * CMP_STRIDE, KV_SET), lambda b, i: (b, i, 0))]
        cm = lambda f: f
    c3 = cm(lambda b, i: (0, 0, 0))
    if head_major:
        o_spec = pl.BlockSpec((1, B_KV, ch, HEAD_DIM), cm(lambda b, i: (b, 0, i, 0)))
        o_shape = jax.ShapeDtypeStruct((bsz, B_KV, n_chunks, HEAD_DIM), BF16)
    else:
        o_spec = pl.BlockSpec((1, ch, B_KV * HEAD_DIM), cm(lambda b, i: (b, i, 0)))
        o_shape = jax.ShapeDtypeStruct((bsz, n_chunks, B_KV * HEAD_DIM), BF16)
    in_specs = x_specs + [pl.BlockSpec(wp.shape, c3), pl.BlockSpec(u.shape, c3),
                          pl.BlockSpec(b1.shape, c3), pl.BlockSpec(w2.shape, c3)]
    grid = (bsz, n_chunks // ch)
    scratch = [pltpu.VMEM((2 * (B_KV // 2), 8, PAIR), F32)]
    kern = functools.partial(_compress_kernel, n_in=n_in, head_major=head_major, paged=paged)
    if paged:
        grid_spec = pltpu.PrefetchScalarGridSpec(num_scalar_prefetch=1, grid=grid, in_specs=in_specs,
                                                 out_specs=[o_spec, o_spec], scratch_shapes=scratch)
        args = (page_table,) + (x,) * n_in
    else:
        grid_spec = pl.GridSpec(grid=grid, in_specs=in_specs, out_specs=[o_spec, o_spec], scratch_shapes=scratch)
        args = (x,)
    return pl.pallas_call(
        kern, grid_spec=grid_spec, out_shape=[o_shape, o_shape],
        compiler_params=_params("parallel", "arbitrary"), name="compress",
    )(*args, wp, u, b1, w2)


def _select_blocks(imp, blk, cur, n_sel, axis):
    valid = blk <= cur
    forced = (blk == 0) | (valid & (blk > cur - N_LOCAL))
    score = jnp.where(forced, FORCE, jnp.where(valid, imp, -FORCE))
    score = jnp.where(blk < n_sel, score, -jnp.inf)
    out = jnp.full(score.shape, NEG_BIG, F32)
    big = jnp.int32(2 ** 30)
    for _ in range(min(TOP_N, n_sel)):
        m = jnp.max(score, axis=axis, keepdims=True)
        first = jnp.min(jnp.where(score == m, blk, big), axis=axis, keepdims=True)
        pick = blk == first
        out = jnp.where(pick, 0.0, out)
        score = jnp.where(pick, -jnp.inf, score)
    return jnp.where(valid, out, NEG_BIG)


def _overlap(c1, j, n_sel):
    c0 = (c1 - 1) * CMP_STRIDE
    j0 = j * SEL_BLOCK
    return ((c1 >= 1) & (j < n_sel) & (c0 <= j0 + SEL_BLOCK - 1) & (c0 + CMP_BLOCK - 1 >= j0)).astype(F32)


def _rank_select(imp, blk, cur, n_sel):
    valid = blk <= cur
    forced = (blk == 0) | (valid & (blk > cur - N_LOCAL))
    score = jnp.where(forced, FORCE, jnp.where(valid, imp, -FORCE))
    rank = jnp.zeros(score.shape, F32)
    for jp in range(n_sel):
        row = score[jp:jp + 1, :]
        gt = jnp.where(row > score, 1.0, 0.0)
        ge = jnp.where(row >= score, 1.0, 0.0)
        rank = rank + jnp.where(blk > jp, ge, gt)
    keep = jnp.where(rank < TOP_N, 0.0, NEG_BIG)
    return jnp.where(valid, jnp.where(blk < n_sel, keep, NEG_BIG), NEG_BIG)


def _online_softmax_t(s, m, l, acc, v):
    m_new = jnp.maximum(m, jnp.max(s, axis=0, keepdims=True))
    alpha = jnp.exp2(m - m_new)
    p = jnp.exp2(s - m_new)
    l = alpha * l + jnp.sum(p, axis=0, keepdims=True)
    acc = alpha * acc + _dot_tn(v, p)
    return m_new, l, acc


def _prompt_attn_kernel(q_ref, kc_ref, vc_ref, ks_ref, vs_ref, kw_ref, vw_ref, g_ref, o_ref, gt_ref,
                        *, tq, tk, n_sel, nb):
    g = pl.program_id(2)
    t0 = pl.program_id(1) * tq
    cols = B_GROUP * tq
    q = q_ref[0].reshape(cols, HEAD_DIM)
    tpos = t0 + lax.broadcasted_iota(jnp.int32, (1, tq), 1)
    nch = kc_ref.shape[2]
    rep = lambda a: jnp.concatenate([a] * B_GROUP, axis=1)

    c1 = lax.broadcasted_iota(jnp.int32, (nch, 1), 0)
    cmask = rep((c1 >= 1) & (c1 * CMP_STRIDE + (CMP_BLOCK - CMP_STRIDE - 1) <= tpos))
    s = jnp.where(cmask, _dot_nt(kc_ref[0, 0], q), NEG_BIG)
    m = jnp.max(s, axis=0, keepdims=True)
    e = jnp.where(cmask, jnp.exp2(s - m), 0.0)
    l = jnp.sum(e, axis=0, keepdims=True)
    p = e * (1.0 / jnp.where(l == 0.0, 1.0, l))
    o_cmp = _dot_tn(vc_ref[0, 0], p)

    p4 = p[:, 0:tq]
    for r in range(1, B_GROUP):
        p4 = p4 + p[:, r * tq:(r + 1) * tq]
    ov_t = _overlap(lax.broadcasted_iota(jnp.int32, (nb, nch), 1),
                    lax.broadcasted_iota(jnp.int32, (nb, nch), 0), n_sel)
    imp = jnp.dot(ov_t, p4, preferred_element_type=F32, precision=lax.Precision.HIGHEST)
    blk = lax.broadcasted_iota(jnp.int32, (nb, tq), 0)
    bias = _rank_select(imp, blk, tpos // SEL_BLOCK, n_sel).astype(BF16)

    def sel_scores(kt):
        k0 = pl.multiple_of(kt * tk, tk)
        kpos = k0 + lax.broadcasted_iota(jnp.int32, (tk, 1), 0)
        blk_of_key = (k0 + lax.broadcasted_iota(jnp.int32, (tk, nb), 0)) // SEL_BLOCK
        onehot = jnp.where(blk_of_key == lax.broadcasted_iota(jnp.int32, (tk, nb), 1), 1.0, 0.0).astype(BF16)
        mb = jnp.dot(onehot, bias, preferred_element_type=F32)
        mb = jnp.where(kpos <= tpos, mb, NEG_BIG)
        return _dot_nt(ks_ref[0, 0, pl.ds(k0, tk), :], q) + rep(mb), vs_ref[0, 0, pl.ds(k0, tk), :]

    def win_scores(kt):
        k0 = pl.multiple_of(kt * tk, tk)
        kpos = k0 + lax.broadcasted_iota(jnp.int32, (tk, 1), 0)
        mb = jnp.where((kpos <= tpos) & (kpos >= tpos - WINDOW), 0.0, NEG_BIG)
        return _dot_nt(kw_ref[0, 0, pl.ds(k0, tk), :], q) + rep(mb), vw_ref[0, 0, pl.ds(k0, tk), :]

    def sel_only(kt, carry):
        s, v = sel_scores(kt)
        return _online_softmax_t(s, *carry, v)

    def both(kt, carry):
        s, v = sel_scores(kt)
        sw, vw = win_scores(kt)
        return _online_softmax_t(s, *carry[:3], v) + _online_softmax_t(sw, *carry[3:], vw)

    init = (jnp.full((1, cols), NEG_BIG, F32), jnp.zeros((1, cols), F32), jnp.zeros((HEAD_DIM, cols), F32))
    hi = (t0 + tq - 1) // tk + 1
    lo_w = jnp.maximum(t0 - WINDOW, 0) // tk
    sel_state = lax.fori_loop(0, lo_w, sel_only, init)
    _, l_s, acc_s, _, l_w, acc_w = lax.fori_loop(lo_w, hi, both, sel_state + init)

    gt_ref[...] = g_ref[0].T
    gate = lambda br: jnp.concatenate(
        [gt_ref[pl.ds(g * (3 * B_GROUP) + 3 * r + br, 1), :] for r in range(B_GROUP)], axis=1)
    o = gate(0) * o_cmp + (gate(1) * (1.0 / l_s)) * acc_s + (gate(2) * (1.0 / l_w)) * acc_w
    for r in range(B_GROUP):
        o_ref[0, :, r * HEAD_DIM:(r + 1) * HEAD_DIM] = o[:, r * tq:(r + 1) * tq].T.astype(o_ref.dtype)


def nsa_prompt_attn(q, kc, vc, ks, vs, kw, vw, gates, *, tq, tk):
    bsz, _, t, _ = q.shape
    nch = kc.shape[2]
    n_sel = -(-t // SEL_BLOCK)
    nb = -(-n_sel // 16) * 16
    assert t % tq == 0 and t % tk == 0 and tq == GATE_LANES
    seq = lambda n: pl.BlockSpec((1, 1, n, HEAD_DIM), lambda b, i, g: (b, g, 0, 0))
    kern = functools.partial(_prompt_attn_kernel, tq=tq, tk=tk, n_sel=n_sel, nb=nb)
    return pl.pallas_call(
        kern,
        grid=(bsz, t // tq, B_KV),
        in_specs=[pl.BlockSpec((1, B_GROUP, tq, HEAD_DIM), lambda b, i, g: (b, g, i, 0)),
                  seq(nch), seq(nch), seq(t), seq(t), seq(t), seq(t),
                  pl.BlockSpec((1, tq, GATE_LANES), lambda b, i, g: (b, i, 0))],
        out_specs=pl.BlockSpec((1, tq, B_GROUP * HEAD_DIM), lambda b, i, g: (b, i, g)),
        out_shape=jax.ShapeDtypeStruct((bsz, t, N_Q_COLS), BF16),
        scratch_shapes=[pltpu.VMEM((GATE_LANES, tq), F32)],
        compiler_params=_params("parallel", "parallel", "arbitrary"),
        name="nsa_prompt_attn",
    )(q, kc, vc, ks, vs, kw, vw, gates)


KV_HALF = B_KV * HEAD_DIM
Q_COLS = B_GROUP * B_KV * 8


def _sample_attn_kernel(*refs, n_pages_step, past, t_new, n_keep):
    pt_ref, qbd_ref, gl_ref, kc_ref, vc_ref, swin_ref, nsel_ref, nwin_ref = refs[:8]
    page_refs = refs[8:8 + n_pages_step]
    out_ref, bias_ref, m_ref, l_ref, acc_ref, oth_ref = refs[8 + n_pages_step:]
    del pt_ref
    i = pl.program_id(1)
    qbd = qbd_ref[0]
    ncol = qbd.shape[1]
    col = lax.broadcasted_iota(jnp.int32, (1, ncol), 1)
    tcol = col % t_new
    qpos = past + tcol
    sig = _sigmoid(gl_ref[0])
    n_sel = -(-(past + t_new) // SEL_BLOCK)
    blk_step = n_pages_step * PAGE_SIZE // SEL_BLOCK
    pad_rows = 8

    def scores(k):
        return jnp.dot(k.astype(BF16), qbd, preferred_element_type=F32) * SCALE

    @pl.when(i == 0)
    def _():
        nch = kc_ref.shape[1]
        c1 = lax.broadcasted_iota(jnp.int32, (nch, 1), 0)
        cmask = (c1 >= 1) & (c1 * CMP_STRIDE + (CMP_BLOCK - CMP_STRIDE - 1) <= qpos)
        s = jnp.where(cmask, scores(kc_ref[0]), NEG_BIG)
        m = jnp.max(s, axis=0, keepdims=True)
        e = jnp.where(cmask, jnp.exp(s - m), 0.0)
        l = jnp.sum(e, axis=0, keepdims=True)
        p = e / jnp.where(l == 0.0, 1.0, l)
        o_cmp = _dot_tn(vc_ref[0], p)
        nb = bias_ref.shape[0]
        ov_t = _overlap(lax.broadcasted_iota(jnp.int32, (nb, nch), 1),
                        lax.broadcasted_iota(jnp.int32, (nb, nch), 0), n_sel)
        imp = jnp.dot(ov_t, p, preferred_element_type=F32, precision=lax.Precision.HIGHEST)
        per = ncol // B_GROUP
        imp = imp + pltpu.roll(imp, per, axis=1) + pltpu.roll(imp, 2 * per, axis=1) + pltpu.roll(imp, 3 * per, axis=1)
        blk = lax.broadcasted_iota(jnp.int32, (nb, ncol), 0)
        bias_ref[...] = _select_blocks(imp, blk, qpos // SEL_BLOCK, n_sel, axis=0)
        kv_w = jnp.concatenate([swin_ref[0], nwin_ref[0], jnp.zeros((pad_rows, 2 * KV_HALF), F32)], axis=0)
        nw = kv_w.shape[0]
        wi = lax.broadcasted_iota(jnp.int32, (nw, 1), 0)
        wpos = past - n_keep + wi
        wmask = (wi < n_keep + t_new) & (wpos <= qpos) & (wpos >= qpos - WINDOW) & (wpos >= 0)
        s = jnp.where(wmask, scores(kv_w[:, :KV_HALF]), NEG_BIG)
        m = jnp.max(s, axis=0, keepdims=True)
        e = jnp.where(wmask, jnp.exp(s - m), 0.0)
        l = jnp.sum(e, axis=0, keepdims=True)
        o_win = _dot_tn(kv_w[:, KV_HALF:], e / jnp.where(l == 0.0, 1.0, l))
        oth_ref[...] = sig[0:1] * o_cmp + sig[2:3] * o_win
        m_ref[...] = jnp.full(m_ref.shape, NEG_BIG, F32)
        l_ref[...] = jnp.zeros(l_ref.shape, F32)
        acc_ref[...] = jnp.zeros(acc_ref.shape, F32)

    def update(s, v=None, v_t=None):
        m_new = jnp.maximum(m_ref[...], jnp.max(s, axis=0, keepdims=True))
        alpha = jnp.exp(m_ref[...] - m_new)
        p = jnp.exp(s - m_new)
        l_ref[...] = alpha * l_ref[...] + jnp.sum(p, axis=0, keepdims=True)
        pv = _dot_tn(v, p) if v_t is None else _dot(v_t, p)
        acc_ref[...] = alpha * acc_ref[...] + pv
        m_ref[...] = m_new

    bias = bias_ref[pl.ds(pl.multiple_of(i * blk_step, blk_step), blk_step), :]
    per_page = PAGE_SIZE // SEL_BLOCK
    s_parts, v_parts = [], []
    for p in range(n_pages_step):
        page = page_refs[p][0]
        mb = jnp.concatenate([jnp.broadcast_to(bias[per_page * p + j:per_page * p + j + 1, :], (SEL_BLOCK, ncol))
                              for j in range(per_page)], axis=0)
        s_parts.append(_dot_tn(page[:KV_HALF], qbd) * SCALE + mb)
        v_parts.append(page[KV_HALF:].astype(BF16))
    update(jnp.concatenate(s_parts, axis=0), v_t=jnp.concatenate(v_parts, axis=1))

    @pl.when(i == pl.num_programs(1) - 1)
    def _():
        kv_n = jnp.concatenate([nsel_ref[0], jnp.zeros((pad_rows, 2 * KV_HALF), F32)], axis=0)
        u = lax.broadcasted_iota(jnp.int32, (kv_n.shape[0], 1), 0)
        nb_new = past // SEL_BLOCK
        s = scores(kv_n[:, :KV_HALF]) + bias_ref[nb_new:nb_new + 1, :]
        update(jnp.where((u < t_new) & (past + u <= qpos), s, NEG_BIG), kv_n[:, KV_HALF:])
        o = oth_ref[...] + sig[1:2] * acc_ref[...] / l_ref[...]
        g_row = lax.broadcasted_iota(jnp.int32, o.shape, 0) // HEAD_DIM
        g_col = (lax.broadcasted_iota(jnp.int32, o.shape, 1) // t_new) % B_KV
        o = jnp.where(g_row == g_col, o, 0.0)
        out_ref[0] = o[0:HEAD_DIM] + o[HEAD_DIM:2 * HEAD_DIM] + o[2 * HEAD_DIM:3 * HEAD_DIM] + o[3 * HEAD_DIM:]


def nsa_sample_attn(q, gate_logits, kc, vc, cache_kv_sel, state_kv_win, kv_sel_new, kv_win_new, page_table,
                    *, n_pages_step):
    bsz, t, _ = q.shape
    n_pages = page_table.shape[1]
    past = n_pages * PAGE_SIZE
    n_keep = state_kv_win.shape[1]
    assert t * B_GROUP * B_KV == Q_COLS and past % SEL_BLOCK == 0 and n_pages % n_pages_step == 0
    qt = q.reshape(bsz, t, B_KV, B_GROUP, HEAD_DIM).transpose(0, 2, 4, 3, 1)
    qbd = jnp.einsum('bgdrt,gh->bgdrht', qt, jnp.eye(B_KV, dtype=F32)).reshape(bsz, KV_HALF, Q_COLS).astype(BF16)
    gl = gate_logits.reshape(bsz, t, B_KV, B_GROUP, 3).transpose(0, 4, 3, 2, 1).reshape(bsz, 3, Q_COLS)
    gl = jnp.pad(gl, ((0, 0), (0, 5), (0, 0)))
    n_blk = past // SEL_BLOCK + 8
    per_b = lambda shape: pl.BlockSpec((1,) + shape, lambda b, i, pt: (b, 0, 0))
    page_specs = [pl.BlockSpec((1, 2 * KV_HALF, PAGE_SIZE), functools.partial(
        lambda b, i, pt, j: (pt[b, i * n_pages_step + j], 0, 0), j=j)) for j in range(n_pages_step)]
    kern = functools.partial(_sample_attn_kernel, n_pages_step=n_pages_step, past=past, t_new=t, n_keep=n_keep)
    out = pl.pallas_call(
        kern,
        grid_spec=pltpu.PrefetchScalarGridSpec(
            num_scalar_prefetch=1,
            grid=(bsz, n_pages // n_pages_step),
            in_specs=[per_b((KV_HALF, Q_COLS)), per_b((8, Q_COLS)), per_b(kc.shape[1:]), per_b(vc.shape[1:]),
                      per_b((n_keep, 2 * KV_HALF)), per_b((t, 2 * KV_HALF)), per_b((t, 2 * KV_HALF))] + page_specs,
            out_specs=pl.BlockSpec((1, HEAD_DIM, Q_COLS), lambda b, i, pt: (b, 0, 0)),
            scratch_shapes=[pltpu.VMEM((n_blk, Q_COLS), F32), pltpu.VMEM((1, Q_COLS), F32),
                            pltpu.VMEM((1, Q_COLS), F32), pltpu.VMEM((KV_HALF, Q_COLS), F32),
                            pltpu.VMEM((KV_HALF, Q_COLS), F32)]),
        out_shape=jax.ShapeDtypeStruct((bsz, HEAD_DIM, Q_COLS), F32),
        compiler_params=_params("parallel", "arbitrary"),
        name="nsa_sample_attn",
    )(page_table, qbd, gl, kc, vc, state_kv_win, kv_sel_new, kv_win_new, *([cache_kv_sel] * n_pages_step))
    return out.reshape(bsz, HEAD_DIM, B_GROUP, B_KV, t).transpose(0, 4, 3, 2, 1).reshape(bsz, t, N_Q_COLS)


def _tile(n, pref):
    return pref if n % pref == 0 else n


def _run_group(x, s0, is_prompt, caches, a_w_in, a_lb_logits, a_norm_g, a_w_out, b_w_qg, b_w_out, kv_w,
               cmp_pe, cmp_w1, cmp_b1, cmp_w2, ffn_w_in, ffn_w_out, moe_w_router, moe_b_router,
               moe_w_in, moe_w_out, ln_g, ln_b):
    bsz, t, d = x.shape
    n = bsz * t
    if is_prompt:
        h, s_out = hgrn_layer(x, s0, a_w_in[0], a_lb_logits, a_norm_g[0], a_w_out[0], ln_g[0, 0], ln_b[0, 0],
                              layer=0, seg=A_CHUNK, n_seg=4, carry=True)
    else:
        h, s_out = hgrn_layer(x, s0, a_w_in[0], a_lb_logits, a_norm_g[0], a_w_out[0], ln_g[0, 0], ln_b[0, 0],
                              layer=0, seg=t, n_seg=8, carry=False)
    h = h.reshape(n, d)
    h = ffn_layer(h, ffn_w_in[0], ffn_w_out[0], ln_g[0, 1], ln_b[0, 1], tm=_tile(n, 1024), tf=256)
    nq = B_HEADS * HEAD_DIM
    if is_prompt:
        kv_cmp, kv_sel, kv_win, qh, ksh, vsh, kwh, vwh, gts = nsa_proj(h.reshape(bsz, t, d), kv_w, b_w_qg[0], tm=512)
        n_ch = t // CMP_STRIDE
        kc, vc = compress(kv_cmp, None, cmp_pe, cmp_w1, cmp_b1, cmp_w2,
                          ch=n_ch, head_major=True)
        o = nsa_prompt_attn(qh, kc, vc, ksh, vsh, kwh, vwh, gts, tq=128, tk=256)
        h = proj_ln(o.reshape(n, nq), h, b_w_out[0], ln_g[1, 0], ln_b[1, 0], tm=_tile(n, 512))
        h = moe_layer(h, moe_w_router[0], moe_b_router[0], moe_w_in[0], moe_w_out[0], ln_g[1, 1], ln_b[1, 1],
                      tm=_tile(n, 2048), tf=896)
        kvshape = (bsz, t, 2, B_KV, HEAD_DIM)
        return (h.reshape(bsz, t, d), s_out[None], kv_cmp.reshape(kvshape), kv_sel.reshape(kvshape),
                kv_win.reshape(kvshape)[:, t - min(WINDOW, t):])
    cache_kv_cmp, cache_kv_sel, state_kv_win, page_table = caches
    n_pool = cache_kv_cmp.shape[0]
    n_keep = state_kv_win.shape[1]
    w_cat = jnp.concatenate([kv_w, b_w_qg[0]], axis=1)
    w_cat = jnp.pad(w_cat, ((0, 0), (0, (-w_cat.shape[1]) % 128)))
    z = proj(h, w_cat, tm=_tile(n, 512))
    kv = z[:, :N_KV_COLS].reshape(bsz, t, 3, KV_SET)
    q = z[:, N_KV_COLS:N_KV_COLS + nq].reshape(bsz, t, nq)
    gate_logits = z[:, N_KV_COLS + nq:N_KV_COLS + nq + 3 * B_HEADS].reshape(bsz, t, 3 * B_HEADS)
    assert (page_table.shape[1] * PAGE_SIZE + t) // CMP_STRIDE == page_table.shape[1] * PAGE_SIZE // CMP_STRIDE
    token_minor = lambda c: c.transpose(0, 2, 3, 4, 1).reshape(n_pool, KV_SET, PAGE_SIZE)
    kc, vc = compress(token_minor(cache_kv_cmp), page_table,
                      cmp_pe, cmp_w1, cmp_b1, cmp_w2, ch=256, head_major=False)
    o = nsa_sample_attn(q, gate_logits, kc, vc, token_minor(cache_kv_sel),
                        state_kv_win.reshape(bsz, n_keep, KV_SET), kv[:, :, 1], kv[:, :, 2], page_table,
                        n_pages_step=16)
    h = proj_ln(o.reshape(n, nq), h, b_w_out[0], ln_g[1, 0], ln_b[1, 0], tm=_tile(n, 512))
    h = moe_layer(h, moe_w_router[0], moe_b_router[0], moe_w_in[0], moe_w_out[0], ln_g[1, 1], ln_b[1, 1],
                  tm=_tile(n, 2048), tf=896)
    kvshape = (bsz, t, 2, B_KV, HEAD_DIM)
    win_all = jnp.concatenate([state_kv_win, kv[:, :, 2].reshape(kvshape)], axis=1)
    return (h.reshape(bsz, t, d), s_out[None], kv[:, :, 0].reshape(kvshape), kv[:, :, 1].reshape(kvshape),
            win_all[:, -n_keep:])


def kernel(x_prompt, x_sample, state_hgrn, cache_kv_cmp, cache_kv_sel, state_kv_win, page_table,
           a_w_in, a_lb_logits, a_norm_g, a_w_out, b_w_qg, b_w_out, kv_w,
           cmp_pe, cmp_w1, cmp_b1, cmp_w2, ffn_w_in, ffn_w_out,
           moe_w_router, moe_b_router, moe_w_in, moe_w_out, ln_g, ln_b):
    weights = (a_w_in, a_lb_logits, a_norm_g, a_w_out, b_w_qg, b_w_out, kv_w, cmp_pe, cmp_w1, cmp_b1, cmp_w2,
               ffn_w_in, ffn_w_out, moe_w_router, moe_b_router, moe_w_in, moe_w_out, ln_g, ln_b)
    hgrn0 = jnp.zeros((x_prompt.shape[0], A_HEADS, A_DK, A_DV), F32)
    y_p, hg_p, cmp_p, sel_p, win_p = _run_group(x_prompt, hgrn0, True, None, *weights)
    y_s, hg_s, cmp_s, sel_s, win_s = _run_group(
        x_sample, state_hgrn[0], False, (cache_kv_cmp, cache_kv_sel, state_kv_win, page_table), *weights)
    return (y_p, y_s, hg_p, cmp_p, sel_p, win_p, hg_s, cmp_s, sel_s, win_s)
```

```python
import functools

import jax
import jax.numpy as jnp
from jax import lax
from jax.experimental import pallas as pl
from jax.experimental.pallas import tpu as pltpu

F32 = jnp.float32
BF16 = jnp.bfloat16

D_MODEL = 1024
DEPTH = 2
ALPHA = (2.0 * DEPTH) ** 0.25
LN_EPS = 1e-5
RMS_EPS = 1e-6
NEG_BIG = -1e30
FORCE = 1e6
PAGE_SIZE = 128

A_DK = 128
A_HEADS = D_MODEL // A_DK
A_DV = D_MODEL // A_HEADS
A_WIDTH = A_HEADS * A_DK
A_CHUNK = 64

B_HEADS = 16
B_KV = 4
B_GROUP = B_HEADS // B_KV
HEAD_DIM = D_MODEL // B_HEADS
SCALE = HEAD_DIM ** -0.5
CMP_STRIDE = 16
CMP_BLOCK = 2 * CMP_STRIDE
SEL_BLOCK = 64
TOP_N = 8
N_LOCAL = 2
WINDOW = 512

D_FF = 256 * ((8 * D_MODEL // 3 + 255) // 256)
N_EXPERTS = 8
MOE_TOP_K = 2
D_FF_E = 7 * D_MODEL // 2

VMEM_LIMIT_BYTES = 56 * 1024 * 1024


def _params(*sem):
    return pltpu.CompilerParams(dimension_semantics=sem, vmem_limit_bytes=VMEM_LIMIT_BYTES)


def _silu(x):
    return x * (1.0 / (1.0 + jnp.exp(-x)))


def _sigmoid(x):
    return 1.0 / (1.0 + jnp.exp(-x))


def _layer_norm(x, g, b):
    xc = x - jnp.mean(x, -1, keepdims=True)
    var = jnp.mean(xc * xc, -1, keepdims=True)
    return xc * lax.rsqrt(var + LN_EPS) * g + b


def _dot(a, b):
    return jnp.dot(a.astype(BF16), b.astype(BF16), preferred_element_type=F32)


def _dot_nt(a, b):
    return lax.dot_general(a.astype(BF16), b.astype(BF16), (((1,), (1,)), ((), ())),
                           preferred_element_type=F32)


def _dot_tn(a, b):
    return lax.dot_general(a.astype(BF16), b.astype(BF16), (((0,), (0,)), ((), ())),
                           preferred_element_type=F32)


def _hgrn_kernel(x_ref, s0_ref, win_ref, lbl_ref, ng_ref, wout_ref, lng_ref, lnb_ref,
                 h_ref, sout_ref, st_ref, *, layer, seg, n_seg, carry):
    rows = seg * n_seg
    x = x_ref[...].reshape(rows, D_MODEL)
    z = jnp.dot(x.astype(BF16), win_ref[...], preferred_element_type=F32)
    zq = z[:, 0 * A_WIDTH:1 * A_WIDTH]
    zf = z[:, 1 * A_WIDTH:2 * A_WIDTH]
    v = z[:, 2 * A_WIDTH:3 * A_WIDTH]
    zg = z[:, 3 * A_WIDTH:4 * A_WIDTH]

    lbl = lbl_ref[...]
    e = jnp.exp(lbl - jnp.max(lbl, axis=0, keepdims=True))
    lb = jnp.sum(e[:layer + 1], axis=0, keepdims=True) / jnp.sum(e, axis=0, keepdims=True)

    q = _silu(zq)
    f = lb + (1.0 - lb) * _sigmoid(zf)
    logf = jnp.log(f)
    k = 1.0 - f

    r_i = lax.broadcasted_iota(jnp.int32, (seg, seg), 0)
    c_i = lax.broadcasted_iota(jnp.int32, (seg, seg), 1)
    causal = c_i <= r_i
    tri = jnp.where(causal, 1.0, 0.0).astype(BF16)

    if carry:
        @pl.when(pl.program_id(1) == 0)
        def _():
            for h in range(A_HEADS):
                st_ref[h] = s0_ref[0, h].T

    o_parts = []
    for s in range(n_seg):
        sl = slice(s * seg, (s + 1) * seg)
        lf = logf[sl]
        lf_hi = lf.astype(BF16)
        lf_lo = (lf - lf_hi.astype(F32)).astype(BF16)
        g = jnp.dot(tri, lf_hi, preferred_element_type=F32) + jnp.dot(tri, lf_lo, preferred_element_type=F32)
        glast = g[seg - 1:seg, :]
        qg = q[sl] * jnp.exp(g)
        kg = k[sl] * jnp.exp(-g)
        kd = k[sl] * jnp.exp(glast - g)
        eg = jnp.exp(glast)
        vs = v[sl]
        heads = []
        for h in range(A_HEADS):
            cl = slice(h * A_DK, (h + 1) * A_DK)
            if carry:
                st = st_ref[h]
            else:
                st = s0_ref[s, h].T
            att = jnp.where(causal, _dot_nt(qg[:, cl], kg[:, cl]), 0.0)
            o = _dot_nt(qg[:, cl], st) + _dot(att, vs[:, cl])
            st_new = eg[:, cl] * st + _dot_tn(vs[:, cl], kd[:, cl])
            if carry:
                st_ref[h] = st_new
            else:
                sout_ref[s, h] = st_new.T
            o = o * lax.rsqrt(jnp.mean(o * o, -1, keepdims=True) + RMS_EPS)
            heads.append(o)
        o_parts.append(jnp.concatenate(heads, axis=1))
    o = o_parts[0] if n_seg == 1 else jnp.concatenate(o_parts, axis=0)
    o = o * ng_ref[...] * _silu(zg)
    y = jnp.dot(o.astype(BF16), wout_ref[...], preferred_element_type=F32)
    hh = _layer_norm(ALPHA * x + y, lng_ref[...], lnb_ref[...])
    h_ref[...] = hh.reshape(h_ref.shape)

    if carry:
        @pl.when(pl.program_id(1) == pl.num_programs(1) - 1)
        def _():
            for h in range(A_HEADS):
                sout_ref[0, h] = st_ref[h].T


def hgrn_layer(x, s0, w_in, lb_logits, norm_g, w_out, ln_g, ln_b, *, layer, seg, n_seg, carry):
    bsz, t, _ = x.shape
    row2 = lambda a: a.reshape(1, -1).astype(F32)
    w_in = w_in.astype(BF16)
    w_out = w_out.astype(BF16)
    const = lambda *_: (0, 0)
    if carry:
        tile = seg * n_seg
        grid = (bsz, t // tile)
        x_spec = pl.BlockSpec((1, tile, D_MODEL), lambda b, c: (b, c, 0))
        s_spec = pl.BlockSpec((1, A_HEADS, A_DK, A_DV), lambda b, c: (b, 0, 0, 0))
        sem = ("parallel", "arbitrary")
    else:
        assert t == seg
        grid = (bsz // n_seg, 1)
        x_spec = pl.BlockSpec((n_seg, seg, D_MODEL), lambda b, c: (b, 0, 0))
        s_spec = pl.BlockSpec((n_seg, A_HEADS, A_DK, A_DV), lambda b, c: (b, 0, 0, 0))
        sem = ("parallel", "arbitrary")
    kern = functools.partial(_hgrn_kernel, layer=layer, seg=seg, n_seg=n_seg, carry=carry)
    return pl.pallas_call(
        kern,
        grid=grid,
        in_specs=[
            x_spec, s_spec,
            pl.BlockSpec((D_MODEL, 4 * A_WIDTH), const),
            pl.BlockSpec(lb_logits.shape, const),
            pl.BlockSpec((1, A_WIDTH), const),
            pl.BlockSpec((A_WIDTH, D_MODEL), const),
            pl.BlockSpec((1, D_MODEL), const),
            pl.BlockSpec((1, D_MODEL), const),
        ],
        out_specs=[x_spec, s_spec],
        out_shape=[jax.ShapeDtypeStruct(x.shape, F32), jax.ShapeDtypeStruct(s0.shape, F32)],
        scratch_shapes=[pltpu.VMEM((A_HEADS, A_DV, A_DK), F32)],
        compiler_params=_params(*sem),
        name="hgrn_layer",
    )(x, s0, w_in, lb_logits.astype(F32), row2(norm_g), w_out, row2(ln_g), row2(ln_b))


def _ffn_kernel(x_ref, wa_ref, wu_ref, wo_ref, lng_ref, lnb_ref, o_ref, acc_ref, xb_ref):
    j = pl.program_id(1)

    @pl.when(j == 0)
    def _():
        xb_ref[...] = x_ref[...].astype(BF16)

    xb = xb_ref[...]
    a = jnp.dot(xb, wa_ref[...], preferred_element_type=F32)
    u = jnp.dot(xb, wu_ref[...], preferred_element_type=F32)
    part = jnp.dot((_silu(a) * u).astype(BF16), wo_ref[...], preferred_element_type=F32)

    @pl.when(j == 0)
    def _():
        acc_ref[...] = part

    @pl.when(j > 0)
    def _():
        acc_ref[...] += part

    @pl.when(j == pl.num_programs(1) - 1)
    def _():
        o_ref[...] = _layer_norm(ALPHA * x_ref[...] + acc_ref[...], lng_ref[...], lnb_ref[...])


def ffn_layer(x, w_in, w_out, ln_g, ln_b, *, tm, tf):
    n, d = x.shape
    d_ff = w_out.shape[0]
    nf = d_ff // tf
    assert n % tm == 0 and d_ff % tf == 0
    w_in = w_in.astype(BF16)
    w_out = w_out.astype(BF16)
    row2 = lambda a: a.reshape(1, -1).astype(F32)
    return pl.pallas_call(
        _ffn_kernel,
        grid=(n // tm, nf),
        in_specs=[
            pl.BlockSpec((tm, d), lambda i, j: (i, 0)),
            pl.BlockSpec((d, tf), lambda i, j: (0, j)),
            pl.BlockSpec((d, tf), lambda i, j: (0, j + nf)),
            pl.BlockSpec((tf, d), lambda i, j: (j, 0)),
            pl.BlockSpec((1, d), lambda i, j: (0, 0)),
            pl.BlockSpec((1, d), lambda i, j: (0, 0)),
        ],
        out_specs=pl.BlockSpec((tm, d), lambda i, j: (i, 0)),
        out_shape=jax.ShapeDtypeStruct((n, d), F32),
        scratch_shapes=[pltpu.VMEM((tm, d), F32), pltpu.VMEM((tm, d), BF16)],
        compiler_params=_params("parallel", "arbitrary"),
        name="ffn_layer",
    )(x, w_in, w_in, w_out, row2(ln_g), row2(ln_b))


def _router_gate(x, wr, br):
    logits = jnp.dot(x, wr, preferred_element_type=F32, precision=lax.Precision.HIGHEST) + br
    lane = lax.broadcasted_iota(jnp.int32, logits.shape, 1)
    m1 = jnp.max(logits, axis=-1, keepdims=True)
    i1 = jnp.min(jnp.where(logits == m1, lane, N_EXPERTS), axis=-1, keepdims=True)
    rest = jnp.where(lane == i1, -jnp.inf, logits)
    m2 = jnp.max(rest, axis=-1, keepdims=True)
    i2 = jnp.min(jnp.where(rest == m2, lane, N_EXPERTS), axis=-1, keepdims=True)
    e2 = jnp.exp(m2 - m1)
    den = 1.0 + e2
    return jnp.where(lane == i1, 1.0 / den, 0.0) + jnp.where(lane == i2, e2 / den, 0.0)


def _moe_kernel(x_ref, wr_ref, br_ref, wa_ref, wu_ref, wo_ref, lng_ref, lnb_ref, o_ref,
                gate_ref, acc_ref, tot_ref):
    e = pl.program_id(1)
    j = pl.program_id(2)
    nj = pl.num_programs(2)

    @pl.when((e == 0) & (j == 0))
    def _():
        gate_ref[...] = _router_gate(x_ref[...], wr_ref[...], br_ref[...])
        tot_ref[...] = jnp.zeros_like(tot_ref)

    xb = x_ref[...].astype(BF16)
    a = jnp.dot(xb, wa_ref[0], preferred_element_type=F32)
    u = jnp.dot(xb, wu_ref[0], preferred_element_type=F32)
    part = jnp.dot((_silu(a) * u).astype(BF16), wo_ref[0], preferred_element_type=F32)

    @pl.when(j == 0)
    def _():
        acc_ref[...] = part

    @pl.when(j > 0)
    def _():
        acc_ref[...] += part

    @pl.when(j == nj - 1)
    def _():
        gate = gate_ref[...]
        lane = lax.broadcasted_iota(jnp.int32, gate.shape, 1)
        ge = jnp.sum(jnp.where(lane == e, gate, 0.0), axis=-1, keepdims=True)
        tot_ref[...] += ge * acc_ref[...]

    @pl.when((j == nj - 1) & (e == pl.num_programs(1) - 1))
    def _():
        o_ref[...] = _layer_norm(ALPHA * x_ref[...] + tot_ref[...], lng_ref[...], lnb_ref[...])


def moe_layer_dense(x, w_router, b_router, w_in, w_out, ln_g, ln_b, *, tm, tf):
    n, d = x.shape
    ne, d_ff = w_out.shape[0], w_out.shape[1]
    nf = d_ff // tf
    assert n % tm == 0 and d_ff % tf == 0
    w_in = w_in.astype(BF16)
    w_out = w_out.astype(BF16)
    row2 = lambda a: a.reshape(1, -1).astype(F32)
    c2 = lambda i, e, j: (0, 0)
    return pl.pallas_call(
        _moe_kernel,
        grid=(n // tm, ne, nf),
        in_specs=[
            pl.BlockSpec((tm, d), lambda i, e, j: (i, 0)),
            pl.BlockSpec((d, ne), c2),
            pl.BlockSpec((1, ne), c2),
            pl.BlockSpec((1, d, tf), lambda i, e, j: (e, 0, j)),
            pl.BlockSpec((1, d, tf), lambda i, e, j: (e, 0, j + nf)),
            pl.BlockSpec((1, tf, d), lambda i, e, j: (e, j, 0)),
            pl.BlockSpec((1, d), c2),
            pl.BlockSpec((1, d), c2),
        ],
        out_specs=pl.BlockSpec((tm, d), lambda i, e, j: (i, 0)),
        out_shape=jax.ShapeDtypeStruct((n, d), F32),
        scratch_shapes=[pltpu.VMEM((tm, ne), F32), pltpu.VMEM((tm, d), F32), pltpu.VMEM((tm, d), F32)],
        compiler_params=_params("parallel", "arbitrary", "arbitrary"),
        name="moe_layer",
    )(x, w_router.astype(F32), row2(b_router), w_in, w_in, w_out, row2(ln_g), row2(ln_b))


MOE_RB = 256
MOE_TAIL = 128
MOE_ST = 256


def _router_kernel(x_ref, wrt_ref, br_ref, gate_ref, xb_ref):
    x = x_ref[...]
    logits = lax.dot_general(wrt_ref[...], x, (((1,), (1,)), ((), ())), preferred_element_type=F32,
                             precision=lax.Precision.HIGHEST) + br_ref[...]
    eidx = lax.broadcasted_iota(jnp.int32, logits.shape, 0)
    m1 = jnp.max(logits, axis=0, keepdims=True)
    i1 = jnp.min(jnp.where(logits == m1, eidx, N_EXPERTS), axis=0, keepdims=True)
    rest = jnp.where(eidx == i1, -jnp.inf, logits)
    m2 = jnp.max(rest, axis=0, keepdims=True)
    i2 = jnp.min(jnp.where(rest == m2, eidx, N_EXPERTS), axis=0, keepdims=True)
    e2 = jnp.exp(m2 - m1)
    den = 1.0 + e2
    gate_ref[...] = jnp.where(eidx == i1, 1.0 / den, 0.0) + jnp.where(eidx == i2, e2 / den, 0.0)
    xb_ref[...] = x.astype(BF16)


def moe_router(x, w_router, b_router, *, tm):
    n, d = x.shape
    return pl.pallas_call(
        _router_kernel,
        grid=(n // tm,),
        in_specs=[pl.BlockSpec((tm, d), lambda i: (i, 0)), pl.BlockSpec((N_EXPERTS, d), lambda i: (0, 0)),
                  pl.BlockSpec((N_EXPERTS, 1), lambda i: (0, 0))],
        out_specs=[pl.BlockSpec((N_EXPERTS, tm), lambda i: (0, i)), pl.BlockSpec((tm, d), lambda i: (i, 0))],
        out_shape=[jax.ShapeDtypeStruct((N_EXPERTS, n), F32), jax.ShapeDtypeStruct((n, d), BF16)],
        compiler_params=_params("parallel"),
        name="moe_router",
    )(x, w_router.T.astype(F32), b_router.reshape(N_EXPERTS, 1).astype(F32))


def _moe_routed_kernel(cum_ref, xb_ref, gate_ref, rank_ref, wa_ref, wu_ref, wo_ref, y_ref, xs_ref, acc_ref,
                       *, n_sub):
    i = pl.program_id(0)
    e = pl.program_id(1)
    j = pl.program_id(2)
    nj = pl.num_programs(2)
    base = (i * N_EXPERTS + e) * (n_sub + 1)
    count = cum_ref[base + n_sub]
    n_full = (count + (MOE_RB - MOE_TAIL - 1)) // MOE_RB

    def for_blocks(fn):
        def body(b, _):
            fn(pl.multiple_of(b * MOE_RB, MOE_RB), MOE_RB)
            return 0

        lax.fori_loop(0, n_full, body, 0)

        @pl.when(count > n_full * MOE_RB)
        def _():
            fn(pl.multiple_of(n_full * MOE_RB, MOE_RB), MOE_TAIL)

    def for_subtiles(r0, rb, fn):
        def body(c, _):
            @pl.when((cum_ref[base + c] < r0 + rb) & (cum_ref[base + c + 1] > r0))
            def _():
                fn(c)
            return 0

        lax.fori_loop(0, n_sub, body, 0)

    def one_hot(r0, rb, c):
        c0 = pl.multiple_of(c * MOE_ST, MOE_ST)
        rank = rank_ref[pl.ds(e, 1), pl.ds(c0, MOE_ST)]
        gate = gate_ref[pl.ds(e, 1), pl.ds(c0, MOE_ST)]
        hit = (rank == r0 + lax.broadcasted_iota(jnp.int32, (rb, MOE_ST), 0)) & (gate > 0.0)
        return hit, gate, c0

    def gather_block(r0, rb):
        rows = pl.ds(r0, rb)
        xs_ref[rows, :] = jnp.zeros((rb, D_MODEL), BF16)

        def sub(c):
            hit, _, c0 = one_hot(r0, rb, c)
            part = jnp.dot(jnp.where(hit, 1.0, 0.0).astype(BF16), xb_ref[pl.ds(c0, MOE_ST), :],
                           preferred_element_type=F32)
            xs_ref[rows, :] = xs_ref[rows, :] + part.astype(BF16)

        for_subtiles(r0, rb, sub)

    def ffn_block(r0, rb):
        rows = pl.ds(r0, rb)
        xs = xs_ref[rows, :]
        a = jnp.dot(xs, wa_ref[0], preferred_element_type=F32)
        u = jnp.dot(xs, wu_ref[0], preferred_element_type=F32)
        part = jnp.dot((_silu(a) * u).astype(BF16), wo_ref[0], preferred_element_type=F32)

        @pl.when(j == 0)
        def _():
            acc_ref[rows, :] = part

        @pl.when(j > 0)
        def _():
            acc_ref[rows, :] = acc_ref[rows, :] + part

    def scatter_block(r0, rb):
        rows = pl.ds(r0, rb)

        def sub(c):
            hit, gate, c0 = one_hot(r0, rb, c)
            g_rows = jnp.sum(jnp.where(hit, gate, 0.0), axis=1, keepdims=True)
            out = (acc_ref[rows, :] * g_rows).astype(BF16)
            y_ref[pl.ds(c0, MOE_ST), :] += _dot_tn(jnp.where(hit, 1.0, 0.0).astype(BF16), out)

        for_subtiles(r0, rb, sub)

    @pl.when((e == 0) & (j == 0))
    def _():
        y_ref[...] = jnp.zeros_like(y_ref)

    @pl.when(j == 0)
    def _():
        for_blocks(gather_block)

    for_blocks(ffn_block)

    @pl.when(j == nj - 1)
    def _():
        for_blocks(scatter_block)


def moe_routed(xb, gate_t, w_in, w_out, *, tm, tf):
    n, d = xb.shape
    ne, d_ff = w_out.shape[0], w_out.shape[1]
    nf = d_ff // tf
    n_tiles, n_sub = n // tm, tm // MOE_ST
    assert n % tm == 0 and tm % MOE_ST == 0 and d_ff % tf == 0 and tm % MOE_RB == 0
    mask = (gate_t > 0.0).astype(jnp.int32).reshape(ne, n_tiles, tm)
    rank = (jnp.cumsum(mask, axis=-1) - mask).reshape(ne, n)
    cnt = mask.reshape(ne, n_tiles, n_sub, MOE_ST).sum(-1)
    cum = jnp.concatenate([jnp.zeros((ne, n_tiles, 1), jnp.int32), jnp.cumsum(cnt, axis=-1)], axis=-1)
    cum = cum.transpose(1, 0, 2).reshape(-1).astype(jnp.int32)
    w_in = w_in.astype(BF16)
    w_out = w_out.astype(BF16)
    kern = functools.partial(_moe_routed_kernel, n_sub=n_sub)
    return pl.pallas_call(
        kern,
        grid_spec=pltpu.PrefetchScalarGridSpec(
            num_scalar_prefetch=1,
            grid=(n_tiles, ne, nf),
            in_specs=[
                pl.BlockSpec((tm, d), lambda i, e, j, c: (i, 0)),
                pl.BlockSpec((ne, tm), lambda i, e, j, c: (0, i)),
                pl.BlockSpec((ne, tm), lambda i, e, j, c: (0, i)),
                pl.BlockSpec((1, d, tf), lambda i, e, j, c: (e, 0, j)),
                pl.BlockSpec((1, d, tf), lambda i, e, j, c: (e, 0, j + nf)),
                pl.BlockSpec((1, tf, d), lambda i, e, j, c: (e, j, 0)),
            ],
            out_specs=pl.BlockSpec((tm, d), lambda i, e, j, c: (i, 0)),
            scratch_shapes=[pltpu.VMEM((tm, d), BF16), pltpu.VMEM((tm, d), F32)]),
        out_shape=jax.ShapeDtypeStruct((n, d), F32),
        compiler_params=_params("parallel", "arbitrary", "arbitrary"),
        name="moe_routed",
    )(cum, xb, gate_t, rank, w_in, w_in, w_out)


def _add_ln_kernel(x_ref, y_ref, lng_ref, lnb_ref, o_ref):
    o_ref[...] = _layer_norm(ALPHA * x_ref[...] + y_ref[...], lng_ref[...], lnb_ref[...])


def add_ln(x, y, ln_g, ln_b, *, tm):
    n, d = x.shape
    row2 = lambda a: a.reshape(1, -1).astype(F32)
    blk = pl.BlockSpec((tm, d), lambda i: (i, 0))
    c2 = pl.BlockSpec((1, d), lambda i: (0, 0))
    return pl.pallas_call(
        _add_ln_kernel, grid=(n // tm,), in_specs=[blk, blk, c2, c2], out_specs=blk,
        out_shape=jax.ShapeDtypeStruct((n, d), F32), compiler_params=_params("parallel"), name="add_ln",
    )(x, y, row2(ln_g), row2(ln_b))


def moe_layer(x, w_router, b_router, w_in, w_out, ln_g, ln_b, *, tm, tf):
    n = x.shape[0]
    gate_t, xb = moe_router(x, w_router, b_router, tm=_tile(n, 1024))
    y = moe_routed(xb, gate_t, w_in, w_out, tm=tm, tf=tf)
    return add_ln(x, y, ln_g, ln_b, tm=_tile(n, 1024))


def _proj_kernel(x_ref, w_ref, o_ref):
    o_ref[...] = jnp.dot(x_ref[...].astype(BF16), w_ref[...], preferred_element_type=F32)


def proj(x, w, *, tm):
    n, d = x.shape
    m = w.shape[1]
    return pl.pallas_call(
        _proj_kernel,
        grid=(n // tm,),
        in_specs=[pl.BlockSpec((tm, d), lambda i: (i, 0)), pl.BlockSpec((d, m), lambda i: (0, 0))],
        out_specs=pl.BlockSpec((tm, m), lambda i: (i, 0)),
        out_shape=jax.ShapeDtypeStruct((n, m), F32),
        compiler_params=_params("parallel"),
        name="proj",
    )(x, w.astype(BF16))


def _proj_ln_kernel(o_ref, x_ref, w_ref, lng_ref, lnb_ref, h_ref):
    y = jnp.dot(o_ref[...].astype(BF16), w_ref[...], preferred_element_type=F32)
    h_ref[...] = _layer_norm(ALPHA * x_ref[...] + y, lng_ref[...], lnb_ref[...])


def proj_ln(o, x, w, ln_g, ln_b, *, tm):
    n, d = x.shape
    row2 = lambda a: a.reshape(1, -1).astype(F32)
    c2 = lambda i: (0, 0)
    return pl.pallas_call(
        _proj_ln_kernel,
        grid=(n // tm,),
        in_specs=[pl.BlockSpec((tm, o.shape[1]), lambda i: (i, 0)), pl.BlockSpec((tm, d), lambda i: (i, 0)),
                  pl.BlockSpec(w.shape, c2), pl.BlockSpec((1, d), c2), pl.BlockSpec((1, d), c2)],
        out_specs=pl.BlockSpec((tm, d), lambda i: (i, 0)),
        out_shape=jax.ShapeDtypeStruct((n, d), F32),
        compiler_params=_params("parallel"),
        name="proj_ln",
    )(o, x, w.astype(BF16), row2(ln_g), row2(ln_b))


N_KV_COLS = 6 * B_KV * HEAD_DIM
KV_SET = 2 * B_KV * HEAD_DIM
N_Q_COLS = B_HEADS * HEAD_DIM
GATE_LANES = 128
LOG2_E = 1.4426950408889634
V_AUG = 2 * HEAD_DIM
ACC_ROWS = HEAD_DIM + 8


def _nsa_proj_kernel(x_ref, w_ref, cmp_ref, sel_ref, win_ref, q_ref, ks_ref, vs_ref, kw_ref, vw_ref, g_ref):
    z = jnp.dot(x_ref[0].astype(BF16), w_ref[...], preferred_element_type=F32)
    cmp_ref[0] = z[:, 0:KV_SET]
    sel_ref[0] = z[:, KV_SET:2 * KV_SET]
    win_ref[0] = z[:, 2 * KV_SET:3 * KV_SET]
    half = B_KV * HEAD_DIM
    lane = lax.broadcasted_iota(jnp.int32, (z.shape[0], V_AUG), 1)

    def v_aug(lo):
        return jnp.where(lane < HEAD_DIM, z[:, lo:lo + V_AUG], jnp.where(lane == HEAD_DIM, 1.0, 0.0)).astype(BF16)

    for g in range(B_KV):
        lo = KV_SET + g * HEAD_DIM
        ks_ref[0, g] = z[:, lo:lo + HEAD_DIM].astype(BF16)
        vs_ref[0, g] = v_aug(lo + half)
        lo = 2 * KV_SET + g * HEAD_DIM
        kw_ref[0, g] = z[:, lo:lo + HEAD_DIM].astype(BF16)
        vw_ref[0, g] = v_aug(lo + half)
    for h in range(B_HEADS):
        lo = N_KV_COLS + h * HEAD_DIM
        q_ref[0, h] = (z[:, lo:lo + HEAD_DIM] * (SCALE * LOG2_E)).astype(BF16)
    g_ref[0] = _sigmoid(z[:, N_KV_COLS + N_Q_COLS:N_KV_COLS + N_Q_COLS + GATE_LANES])


def nsa_proj(h, kv_w, w_qg, *, tm):
    bsz, t, d = h.shape
    w = jnp.concatenate([kv_w, w_qg], axis=1)
    w = jnp.pad(w, ((0, 0), (0, N_KV_COLS + N_Q_COLS + GATE_LANES - w.shape[1]))).astype(BF16)
    row = lambda: pl.BlockSpec((1, tm, KV_SET), lambda b, i: (b, i, 0))
    hm = lambda nh, w=HEAD_DIM: pl.BlockSpec((1, nh, tm, w), lambda b, i: (b, 0, i, 0))
    sds = jax.ShapeDtypeStruct
    kshape, vshape = sds((bsz, B_KV, t, HEAD_DIM), BF16), sds((bsz, B_KV, t, V_AUG), BF16)
    return pl.pallas_call(
        _nsa_proj_kernel,
        grid=(bsz, t // tm),
        in_specs=[pl.BlockSpec((1, tm, d), lambda b, i: (b, i, 0)), pl.BlockSpec(w.shape, lambda b, i: (0, 0))],
        out_specs=[row(), row(), row(), hm(B_HEADS), hm(B_KV), hm(B_KV, V_AUG), hm(B_KV), hm(B_KV, V_AUG),
                   pl.BlockSpec((1, tm, GATE_LANES), lambda b, i: (b, i, 0))],
        out_shape=[sds((bsz, t, KV_SET), F32)] * 3 + [sds((bsz, B_HEADS, t, HEAD_DIM), BF16)]
        + [kshape, vshape, kshape, vshape] + [sds((bsz, t, GATE_LANES), F32)],
        compiler_params=_params("parallel", "parallel"),
        name="nsa_proj",
    )(h, w)


PAIR = 2 * HEAD_DIM


def _compress_kernel(*refs, n_in, head_major, paged):
    if paged:
        refs = refs[1:]
    x_refs = refs[:n_in]
    wp_ref, u_ref, b1_ref, w2_ref, kc_ref, vc_ref, carry_ref = refs[n_in:]
    i = pl.program_id(1)

    @pl.when(i == 0)
    def _():
        carry_ref[...] = jnp.zeros_like(carry_ref)

    cpp = PAGE_SIZE // CMP_STRIDE
    pr = lax.broadcasted_iota(jnp.int32, (PAGE_SIZE, PAGE_SIZE), 0)
    pc = lax.broadcasted_iota(jnp.int32, (PAGE_SIZE, PAGE_SIZE), 1)
    perm = jnp.where(pc == CMP_STRIDE * (pr % cpp) + pr // cpp, 1.0, 0.0).astype(BF16)
    pages = []
    for r in x_refs:
        if paged:
            pages.append(_dot_nt(perm, r[0]))
        else:
            for p0 in range(0, r.shape[1], PAGE_SIZE):
                pages.append(jnp.dot(perm, r[0, p0:p0 + PAGE_SIZE, :].astype(BF16), preferred_element_type=F32))
    ch = len(pages) * cpp
    row0 = lax.broadcasted_iota(jnp.int32, (ch, PAIR), 0) == 0

    def chunk_rows(s, lo):
        parts = [pg[s * cpp:(s + 1) * cpp, lo:lo + PAIR] for pg in pages]
        return parts[0] if len(parts) == 1 else jnp.concatenate(parts, axis=0)
    for k in range(2):
        wp = wp_ref[k]
        pbm = jnp.dot(u_ref[k], wp, preferred_element_type=F32)
        pb = pbm[0:1, 0:PAIR] + pbm[1:2, PAIR:2 * PAIR] + b1_ref[k]
        for gp in range(B_KV // 2):
            base = k * B_KV * HEAD_DIM + gp * PAIR
            lhs = jnp.concatenate([chunk_rows(s, base) for s in range(CMP_STRIDE)], axis=1).astype(BF16)
            r = jnp.dot(lhs, wp, preferred_element_type=F32)
            first, second = r[:, 0:PAIR], r[:, PAIR:2 * PAIR]
            slot = k * (B_KV // 2) + gp
            prev = carry_ref[slot]
            shifted = jnp.where(row0, prev[7:8, :], pltpu.roll(first, 1, axis=0))
            carry_ref[slot] = first[ch - 8:ch, :]
            hid = jax.nn.gelu(shifted + second + pb)
            out = jnp.dot(hid.astype(BF16), w2_ref[k], preferred_element_type=F32).astype(BF16)
            dst = kc_ref if k == 0 else vc_ref
            if head_major:
                for g2 in range(2):
                    dst[0, 2 * gp + g2] = out[:, g2 * HEAD_DIM:(g2 + 1) * HEAD_DIM]
            else:
                dst[0, :, gp * PAIR:(gp + 1) * PAIR] = out


def _compress_weights(cmp_pe, cmp_w1, cmp_b1, cmp_w2):
    eye2 = jnp.eye(2, dtype=F32)
    w1 = cmp_w1.reshape(2, CMP_STRIDE, 2, HEAD_DIM, HEAD_DIM)
    wp = jnp.einsum('fskdh,ab->ksadfbh', w1, eye2).reshape(2, CMP_STRIDE * PAIR, 2 * PAIR)
    pe = cmp_pe.reshape(2, CMP_STRIDE, 2, HEAD_DIM)
    u = jnp.broadcast_to(pe.transpose(2, 0, 1, 3)[:, :, :, None, :], (2, 2, CMP_STRIDE, 2, HEAD_DIM))
    u = jnp.pad(u.reshape(2, 2, CMP_STRIDE * PAIR), ((0, 0), (0, 6), (0, 0)))
    b1 = jnp.tile(cmp_b1, (1, 2)).reshape(2, 1, PAIR)
    w2 = jnp.einsum('khd,ab->kahbd', cmp_w2, eye2).reshape(2, PAIR, PAIR)
    return wp.astype(BF16), u.astype(BF16), b1.astype(F32), w2.astype(BF16)


def compress(x, page_table, cmp_pe, cmp_w1, cmp_b1, cmp_w2, *, ch, head_major):
    wp, u, b1, w2 = _compress_weights(cmp_pe, cmp_w1, cmp_b1, cmp_w2)
    paged = page_table is not None
    if paged:
        bsz, n_pages = page_table.shape
        n_in = ch * CMP_STRIDE // PAGE_SIZE
        n_chunks = n_pages * PAGE_SIZE // CMP_STRIDE
        x_specs = [pl.BlockSpec((1, KV_SET, PAGE_SIZE), functools.partial(
            lambda b, i, pt, j: (pt[b, i * n_in + j], 0, 0), j=j)) for j in range(n_in)]
        cm = lambda f: (lambda b, i, pt: f(b, i))
    else:
        bsz, t, _ = x.shape
        n_chunks = t // CMP_STRIDE
        n_in = 1
        x_specs = [pl.BlockSpec((1, ch * CMP_STRIDE, KV_SET), lambda b, i: (b, i, 0))]
        cm = lambda f: f
    c3 = cm(lambda b, i: (0, 0, 0))
    if head_major:
        o_spec = pl.BlockSpec((1, B_KV, ch, HEAD_DIM), cm(lambda b, i: (b, 0, i, 0)))
        o_shape = jax.ShapeDtypeStruct((bsz, B_KV, n_chunks, HEAD_DIM), BF16)
    else:
        o_spec = pl.BlockSpec((1, ch, B_KV * HEAD_DIM), cm(lambda b, i: (b, i, 0)))
        o_shape = jax.ShapeDtypeStruct((bsz, n_chunks, B_KV * HEAD_DIM), BF16)
    in_specs = x_specs + [pl.BlockSpec(wp.shape, c3), pl.BlockSpec(u.shape, c3),
                          pl.BlockSpec(b1.shape, c3), pl.BlockSpec(w2.shape, c3)]
    grid = (bsz, n_chunks // ch)
    scratch = [pltpu.VMEM((2 * (B_KV // 2), 8, PAIR), F32)]
    kern = functools.partial(_compress_kernel, n_in=n_in, head_major=head_major, paged=paged)
    if paged:
        grid_spec = pltpu.PrefetchScalarGridSpec(num_scalar_prefetch=1, grid=grid, in_specs=in_specs,
                                                 out_specs=[o_spec, o_spec], scratch_shapes=scratch)
        args = (page_table,) + (x,) * n_in
    else:
        grid_spec = pl.GridSpec(grid=grid, in_specs=in_specs, out_specs=[o_spec, o_spec], scratch_shapes=scratch)
        args = (x,)
    return pl.pallas_call(
        kern, grid_spec=grid_spec, out_shape=[o_shape, o_shape],
        compiler_params=_params("parallel", "arbitrary"), name="compress",
    )(*args, wp, u, b1, w2)


def _select_blocks(imp, blk, cur, n_sel, axis):
    valid = blk <= cur
    forced = (blk == 0) | (valid & (blk > cur - N_LOCAL))
    score = jnp.where(forced, FORCE, jnp.where(valid, imp, -FORCE))
    score = jnp.where(blk < n_sel, score, -jnp.inf)
    out = jnp.full(score.shape, NEG_BIG, F32)
    big = jnp.int32(2 ** 30)
    for _ in range(min(TOP_N, n_sel)):
        m = jnp.max(score, axis=axis, keepdims=True)
        first = jnp.min(jnp.where(score == m, blk, big), axis=axis, keepdims=True)
        pick = blk == first
        out = jnp.where(pick, 0.0, out)
        score = jnp.where(pick, -jnp.inf, score)
    return jnp.where(valid, out, NEG_BIG)


def _overlap(c1, j, n_sel):
    c0 = (c1 - 1) * CMP_STRIDE
    j0 = j * SEL_BLOCK
    return ((c1 >= 1) & (j < n_sel) & (c0 <= j0 + SEL_BLOCK - 1) & (c0 + CMP_BLOCK - 1 >= j0)).astype(F32)


def _rank_select(imp, blk, cur, n_sel):
    valid = blk <= cur
    forced = (blk == 0) | (valid & (blk > cur - N_LOCAL))
    score = jnp.where(forced, FORCE, jnp.where(valid, imp, -FORCE))
    rank = jnp.zeros(score.shape, F32)
    for jp in range(n_sel):
        row = score[jp:jp + 1, :]
        gt = jnp.where(row > score, 1.0, 0.0)
        ge = jnp.where(row >= score, 1.0, 0.0)
        rank = rank + jnp.where(blk > jp, ge, gt)
    keep = jnp.where(rank < TOP_N, 0.0, NEG_BIG)
    return jnp.where(valid, jnp.where(blk < n_sel, keep, NEG_BIG), NEG_BIG)


def _online_softmax_t(s, m, acc, v):
    m_new = jnp.maximum(m, jnp.max(s, axis=0, keepdims=True))
    p = jnp.exp2((s - m_new).astype(BF16))
    acc = jnp.exp2(m - m_new) * acc + _dot_tn(v, p)[:ACC_ROWS]
    return m_new, acc


def _prompt_attn_kernel(q_ref, kc_ref, vc_ref, ks_ref, vs_ref, kw_ref, vw_ref, g_ref, o_ref, gt_ref,
                        *, tq, tk, n_sel, nb):
    g = pl.program_id(2)
    t0 = pl.program_id(1) * tq
    cols = B_GROUP * tq
    q = q_ref[0].reshape(cols, HEAD_DIM)
    tpos = t0 + lax.broadcasted_iota(jnp.int32, (1, tq), 1)
    nch = kc_ref.shape[2]
    rep = lambda a: jnp.concatenate([a] * B_GROUP, axis=1)

    c1 = lax.broadcasted_iota(jnp.int32, (nch, 1), 0)
    cmask = rep((c1 >= 1) & (c1 * CMP_STRIDE + (CMP_BLOCK - CMP_STRIDE - 1) <= tpos))
    s = jnp.where(cmask, _dot_nt(kc_ref[0, 0], q), NEG_BIG)
    m = jnp.max(s, axis=0, keepdims=True)
    e = jnp.where(cmask, jnp.exp2(s - m), 0.0)
    l = jnp.sum(e, axis=0, keepdims=True)
    p = e * (1.0 / jnp.where(l == 0.0, 1.0, l))
    o_cmp = _dot_tn(vc_ref[0, 0], p)

    p4 = p[:, 0:tq]
    for r in range(1, B_GROUP):
        p4 = p4 + p[:, r * tq:(r + 1) * tq]
    ov_t = _overlap(lax.broadcasted_iota(jnp.int32, (nb, nch), 1),
                    lax.broadcasted_iota(jnp.int32, (nb, nch), 0), n_sel)
    imp = jnp.dot(ov_t, p4, preferred_element_type=F32, precision=lax.Precision.HIGHEST)
    blk = lax.broadcasted_iota(jnp.int32, (nb, tq), 0)
    bias = _rank_select(imp, blk, tpos // SEL_BLOCK, n_sel).astype(BF16)

    def sel_scores(kt):
        k0 = pl.multiple_of(kt * tk, tk)
        kpos = k0 + lax.broadcasted_iota(jnp.int32, (tk, 1), 0)
        blk_of_key = (k0 + lax.broadcasted_iota(jnp.int32, (tk, nb), 0)) // SEL_BLOCK
        onehot = jnp.where(blk_of_key == lax.broadcasted_iota(jnp.int32, (tk, nb), 1), 1.0, 0.0).astype(BF16)
        mb = jnp.dot(onehot, bias, preferred_element_type=F32)
        mb = jnp.where(kpos <= tpos, mb, NEG_BIG)
        return _dot_nt(ks_ref[0, 0, pl.ds(k0, tk), :], q) + rep(mb), vs_ref[0, 0, pl.ds(k0, tk), :]

    def win_scores(kt):
        k0 = pl.multiple_of(kt * tk, tk)
        kpos = k0 + lax.broadcasted_iota(jnp.int32, (tk, 1), 0)
        mb = jnp.where((kpos <= tpos) & (kpos >= tpos - WINDOW), 0.0, NEG_BIG)
        return _dot_nt(kw_ref[0, 0, pl.ds(k0, tk), :], q) + rep(mb), vw_ref[0, 0, pl.ds(k0, tk), :]

    def sel_only(kt, carry):
        s, v = sel_scores(kt)
        return _online_softmax_t(s, *carry, v)

    init = (jnp.full((1, cols), NEG_BIG, F32), jnp.zeros((ACC_ROWS, cols), F32))
    hi = (t0 + tq - 1) // tk + 1
    lo_w = jnp.maximum(t0 - WINDOW, 0) // tk

    def both(kt, carry):
        s, v = sel_scores(kt)
        sw, vw = win_scores(kt)
        return _online_softmax_t(s, *carry[:2], v) + _online_softmax_t(sw, *carry[2:], vw)

    sel_state = lax.fori_loop(0, lo_w, sel_only, init)
    _, acc_s, _, acc_w = lax.fori_loop(lo_w, hi, both, sel_state + init)

    gt_ref[...] = g_ref[0].T
    gate = lambda br: jnp.concatenate(
        [gt_ref[pl.ds(g * (3 * B_GROUP) + 3 * r + br, 1), :] for r in range(B_GROUP)], axis=1)
    norm = lambda acc: acc[:HEAD_DIM] * (1.0 / acc[HEAD_DIM:HEAD_DIM + 1])
    o = gate(0) * o_cmp + gate(1) * norm(acc_s) + gate(2) * norm(acc_w)
    for r in range(B_GROUP):
        o_ref[0, :, r * HEAD_DIM:(r + 1) * HEAD_DIM] = o[:, r * tq:(r + 1) * tq].T.astype(o_ref.dtype)


def nsa_prompt_attn(q, kc, vc, ks, vs, kw, vw, gates, *, tq, tk):
    bsz, _, t, _ = q.shape
    nch = kc.shape[2]
    n_sel = -(-t // SEL_BLOCK)
    nb = -(-n_sel // 16) * 16
    assert t % tq == 0 and t % tk == 0 and tq % 128 == 0
    seq = lambda n, w=HEAD_DIM: pl.BlockSpec((1, 1, n, w), lambda b, i, g: (b, g, 0, 0))
    kern = functools.partial(_prompt_attn_kernel, tq=tq, tk=tk, n_sel=n_sel, nb=nb)
    return pl.pallas_call(
        kern,
        grid=(bsz, t // tq, B_KV),
        in_specs=[pl.BlockSpec((1, B_GROUP, tq, HEAD_DIM), lambda b, i, g: (b, g, i, 0)),
                  seq(nch), seq(nch), seq(t), seq(t, V_AUG), seq(t), seq(t, V_AUG),
                  pl.BlockSpec((1, tq, GATE_LANES), lambda b, i, g: (b, i, 0))],
        out_specs=pl.BlockSpec((1, tq, B_GROUP * HEAD_DIM), lambda b, i, g: (b, i, g)),
        out_shape=jax.ShapeDtypeStruct((bsz, t, N_Q_COLS), BF16),
        scratch_shapes=[pltpu.VMEM((GATE_LANES, tq), F32)],
        compiler_params=_params("parallel", "parallel", "arbitrary"),
        name="nsa_prompt_attn",
    )(q, kc, vc, ks, vs, kw, vw, gates)


KV_HALF = B_KV * HEAD_DIM
Q_COLS = B_GROUP * B_KV * 8


def _sample_attn_kernel(*refs, n_pages_step, past, t_new, n_keep):
    pt_ref, qbd_ref, gl_ref, kc_ref, vc_ref, swin_ref, nsel_ref, nwin_ref = refs[:8]
    page_refs = refs[8:8 + n_pages_step]
    out_ref, bias_ref, m_ref, l_ref, acc_ref, oth_ref = refs[8 + n_pages_step:]
    del pt_ref
    i = pl.program_id(1)
    qbd = qbd_ref[0]
    ncol = qbd.shape[1]
    col = lax.broadcasted_iota(jnp.int32, (1, ncol), 1)
    tcol = col % t_new
    qpos = past + tcol
    sig = _sigmoid(gl_ref[0])
    n_sel = -(-(past + t_new) // SEL_BLOCK)
    blk_step = n_pages_step * PAGE_SIZE // SEL_BLOCK
    pad_rows = 8

    def scores(k):
        return jnp.dot(k.astype(BF16), qbd, preferred_element_type=F32) * SCALE

    @pl.when(i == 0)
    def _():
        nch = kc_ref.shape[1]
        c1 = lax.broadcasted_iota(jnp.int32, (nch, 1), 0)
        cmask = (c1 >= 1) & (c1 * CMP_STRIDE + (CMP_BLOCK - CMP_STRIDE - 1) <= qpos)
        s = jnp.where(cmask, scores(kc_ref[0]), NEG_BIG)
        m = jnp.max(s, axis=0, keepdims=True)
        e = jnp.where(cmask, jnp.exp(s - m), 0.0)
        l = jnp.sum(e, axis=0, keepdims=True)
        p = e / jnp.where(l == 0.0, 1.0, l)
        o_cmp = _dot_tn(vc_ref[0], p)
        nb = bias_ref.shape[0]
        ov_t = _overlap(lax.broadcasted_iota(jnp.int32, (nb, nch), 1),
                        lax.broadcasted_iota(jnp.int32, (nb, nch), 0), n_sel)
        imp = jnp.dot(ov_t, p, preferred_element_type=F32, precision=lax.Precision.HIGHEST)
        per = ncol // B_GROUP
        imp = imp + pltpu.roll(imp, per, axis=1) + pltpu.roll(imp, 2 * per, axis=1) + pltpu.roll(imp, 3 * per, axis=1)
        blk = lax.broadcasted_iota(jnp.int32, (nb, ncol), 0)
        bias_ref[...] = _select_blocks(imp, blk, qpos // SEL_BLOCK, n_sel, axis=0)
        kv_w = jnp.concatenate([swin_ref[0], nwin_ref[0], jnp.zeros((pad_rows, 2 * KV_HALF), F32)], axis=0)
        nw = kv_w.shape[0]
        wi = lax.broadcasted_iota(jnp.int32, (nw, 1), 0)
        wpos = past - n_keep + wi
        wmask = (wi < n_keep + t_new) & (wpos <= qpos) & (wpos >= qpos - WINDOW) & (wpos >= 0)
        s = jnp.where(wmask, scores(kv_w[:, :KV_HALF]), NEG_BIG)
        m = jnp.max(s, axis=0, keepdims=True)
        e = jnp.where(wmask, jnp.exp(s - m), 0.0)
        l = jnp.sum(e, axis=0, keepdims=True)
        o_win = _dot_tn(kv_w[:, KV_HALF:], e / jnp.where(l == 0.0, 1.0, l))
        oth_ref[...] = sig[0:1] * o_cmp + sig[2:3] * o_win
        m_ref[...] = jnp.full(m_ref.shape, NEG_BIG, F32)
        l_ref[...] = jnp.zeros(l_ref.shape, F32)
        acc_ref[...] = jnp.zeros(acc_ref.shape, F32)

    def update(s, v=None, v_t=None):
        m_new = jnp.maximum(m_ref[...], jnp.max(s, axis=0, keepdims=True))
        alpha = jnp.exp(m_ref[...] - m_new)
        p = jnp.exp(s - m_new)
        l_ref[...] = alpha * l_ref[...] + jnp.sum(p, axis=0, keepdims=True)
        pv = _dot_tn(v, p) if v_t is None else _dot(v_t, p)
        acc_ref[...] = alpha * acc_ref[...] + pv
        m_ref[...] = m_new

    bias = bias_ref[pl.ds(pl.multiple_of(i * blk_step, blk_step), blk_step), :]
    per_page = PAGE_SIZE // SEL_BLOCK
    s_parts, v_parts = [], []
    for p in range(n_pages_step):
        page = page_refs[p][0]
        mb = jnp.concatenate([jnp.broadcast_to(bias[per_page * p + j:per_page * p + j + 1, :], (SEL_BLOCK, ncol))
                              for j in range(per_page)], axis=0)
        s_parts.append(_dot_tn(page[:KV_HALF], qbd) * SCALE + mb)
        v_parts.append(page[KV_HALF:].astype(BF16))
    update(jnp.concatenate(s_parts, axis=0), v_t=jnp.concatenate(v_parts, axis=1))

    @pl.when(i == pl.num_programs(1) - 1)
    def _():
        kv_n = jnp.concatenate([nsel_ref[0], jnp.zeros((pad_rows, 2 * KV_HALF), F32)], axis=0)
        u = lax.broadcasted_iota(jnp.int32, (kv_n.shape[0], 1), 0)
        nb_new = past // SEL_BLOCK
        s = scores(kv_n[:, :KV_HALF]) + bias_ref[nb_new:nb_new + 1, :]
        update(jnp.where((u < t_new) & (past + u <= qpos), s, NEG_BIG), kv_n[:, KV_HALF:])
        o = oth_ref[...] + sig[1:2] * acc_ref[...] / l_ref[...]
        g_row = lax.broadcasted_iota(jnp.int32, o.shape, 0) // HEAD_DIM
        g_col = (lax.broadcasted_iota(jnp.int32, o.shape, 1) // t_new) % B_KV
        o = jnp.where(g_row == g_col, o, 0.0)
        out_ref[0] = o[0:HEAD_DIM] + o[HEAD_DIM:2 * HEAD_DIM] + o[2 * HEAD_DIM:3 * HEAD_DIM] + o[3 * HEAD_DIM:]


def nsa_sample_attn(q, gate_logits, kc, vc, cache_kv_sel, state_kv_win, kv_sel_new, kv_win_new, page_table,
                    *, n_pages_step):
    bsz, t, _ = q.shape
    n_pages = page_table.shape[1]
    past = n_pages * PAGE_SIZE
    n_keep = state_kv_win.shape[1]
    assert t * B_GROUP * B_KV == Q_COLS and past % SEL_BLOCK == 0 and n_pages % n_pages_step == 0
    qt = q.reshape(bsz, t, B_KV, B_GROUP, HEAD_DIM).transpose(0, 2, 4, 3, 1)
    qbd = jnp.einsum('bgdrt,gh->bgdrht', qt, jnp.eye(B_KV, dtype=F32)).reshape(bsz, KV_HALF, Q_COLS).astype(BF16)
    gl = gate_logits.reshape(bsz, t, B_KV, B_GROUP, 3).transpose(0, 4, 3, 2, 1).reshape(bsz, 3, Q_COLS)
    gl = jnp.pad(gl, ((0, 0), (0, 5), (0, 0)))
    n_blk = past // SEL_BLOCK + 8
    per_b = lambda shape: pl.BlockSpec((1,) + shape, lambda b, i, pt: (b, 0, 0))
    page_specs = [pl.BlockSpec((1, 2 * KV_HALF, PAGE_SIZE), functools.partial(
        lambda b, i, pt, j: (pt[b, i * n_pages_step + j], 0, 0), j=j)) for j in range(n_pages_step)]
    kern = functools.partial(_sample_attn_kernel, n_pages_step=n_pages_step, past=past, t_new=t, n_keep=n_keep)
    out = pl.pallas_call(
        kern,
        grid_spec=pltpu.PrefetchScalarGridSpec(
            num_scalar_prefetch=1,
            grid=(bsz, n_pages // n_pages_step),
            in_specs=[per_b((KV_HALF, Q_COLS)), per_b((8, Q_COLS)), per_b(kc.shape[1:]), per_b(vc.shape[1:]),
                      per_b((n_keep, 2 * KV_HALF)), per_b((t, 2 * KV_HALF)), per_b((t, 2 * KV_HALF))] + page_specs,
            out_specs=pl.BlockSpec((1, HEAD_DIM, Q_COLS), lambda b, i, pt: (b, 0, 0)),
            scratch_shapes=[pltpu.VMEM((n_blk, Q_COLS), F32), pltpu.VMEM((1, Q_COLS), F32),
                            pltpu.VMEM((1, Q_COLS), F32), pltpu.VMEM((KV_HALF, Q_COLS), F32),
                            pltpu.VMEM((KV_HALF, Q_COLS), F32)]),
        out_shape=jax.ShapeDtypeStruct((bsz, HEAD_DIM, Q_COLS), F32),
        compiler_params=_params("parallel", "arbitrary"),
        name="nsa_sample_attn",
    )(page_table, qbd, gl, kc, vc, state_kv_win, kv_sel_new, kv_win_new, *([cache_kv_sel] * n_pages_step))
    return out.reshape(bsz, HEAD_DIM, B_GROUP, B_KV, t).transpose(0, 4, 3, 2, 1).reshape(bsz, t, N_Q_COLS)


def _tile(n, pref):
    return pref if n % pref == 0 else n


def _run_group(x, s0, is_prompt, caches, a_w_in, a_lb_logits, a_norm_g, a_w_out, b_w_qg, b_w_out, kv_w,
               cmp_pe, cmp_w1, cmp_b1, cmp_w2, ffn_w_in, ffn_w_out, moe_w_router, moe_b_router,
               moe_w_in, moe_w_out, ln_g, ln_b):
    bsz, t, d = x.shape
    n = bsz * t
    if is_prompt:
        h, s_out = hgrn_layer(x, s0, a_w_in[0], a_lb_logits, a_norm_g[0], a_w_out[0], ln_g[0, 0], ln_b[0, 0],
                              layer=0, seg=A_CHUNK, n_seg=8, carry=True)
    else:
        h, s_out = hgrn_layer(x, s0, a_w_in[0], a_lb_logits, a_norm_g[0], a_w_out[0], ln_g[0, 0], ln_b[0, 0],
                              layer=0, seg=t, n_seg=8, carry=False)
    h = h.reshape(n, d)
    h = ffn_layer(h, ffn_w_in[0], ffn_w_out[0], ln_g[0, 1], ln_b[0, 1], tm=_tile(n, 1024), tf=256)
    nq = B_HEADS * HEAD_DIM
    if is_prompt:
        kv_cmp, kv_sel, kv_win, qh, ksh, vsh, kwh, vwh, gts = nsa_proj(h.reshape(bsz, t, d), kv_w, b_w_qg[0], tm=512)
        n_ch = t // CMP_STRIDE
        kc, vc = compress(kv_cmp, None, cmp_pe, cmp_w1, cmp_b1, cmp_w2,
                          ch=n_ch, head_major=True)
        o = nsa_prompt_attn(qh, kc, vc, ksh, vsh, kwh, vwh, gts, tq=256, tk=256)
        h = proj_ln(o.reshape(n, nq), h, b_w_out[0], ln_g[1, 0], ln_b[1, 0], tm=_tile(n, 512))
        h = moe_layer(h, moe_w_router[0], moe_b_router[0], moe_w_in[0], moe_w_out[0], ln_g[1, 1], ln_b[1, 1],
                      tm=_tile(n, 2048), tf=896)
        kvshape = (bsz, t, 2, B_KV, HEAD_DIM)
        return (h.reshape(bsz, t, d), s_out[None], kv_cmp.reshape(kvshape), kv_sel.reshape(kvshape),
                kv_win.reshape(kvshape)[:, t - min(WINDOW, t):])
    cache_kv_cmp, cache_kv_sel, state_kv_win, page_table = caches
    n_pool = cache_kv_cmp.shape[0]
    n_keep = state_kv_win.shape[1]
    w_cat = jnp.concatenate([kv_w, b_w_qg[0]], axis=1)
    w_cat = jnp.pad(w_cat, ((0, 0), (0, (-w_cat.shape[1]) % 128)))
    z = proj(h, w_cat, tm=_tile(n, 512))
    kv = z[:, :N_KV_COLS].reshape(bsz, t, 3, KV_SET)
    q = z[:, N_KV_COLS:N_KV_COLS + nq].reshape(bsz, t, nq)
    gate_logits = z[:, N_KV_COLS + nq:N_KV_COLS + nq + 3 * B_HEADS].reshape(bsz, t, 3 * B_HEADS)
    assert (page_table.shape[1] * PAGE_SIZE + t) // CMP_STRIDE == page_table.shape[1] * PAGE_SIZE // CMP_STRIDE
    token_minor = lambda c: c.transpose(0, 2, 3, 4, 1).reshape(n_pool, KV_SET, PAGE_SIZE)
    kc, vc = compress(token_minor(cache_kv_cmp), page_table,
                      cmp_pe, cmp_w1, cmp_b1, cmp_w2, ch=256, head_major=False)
    o = nsa_sample_attn(q, gate_logits, kc, vc, token_minor(cache_kv_sel),
                        state_kv_win.reshape(bsz, n_keep, KV_SET), kv[:, :, 1], kv[:, :, 2], page_table,
                        n_pages_step=16)
    h = proj_ln(o.reshape(n, nq), h, b_w_out[0], ln_g[1, 0], ln_b[1, 0], tm=_tile(n, 512))
    h = moe_layer(h, moe_w_router[0], moe_b_router[0], moe_w_in[0], moe_w_out[0], ln_g[1, 1], ln_b[1, 1],
                  tm=_tile(n, 2048), tf=896)
    kvshape = (bsz, t, 2, B_KV, HEAD_DIM)
    win_all = jnp.concatenate([state_kv_win, kv[:, :, 2].reshape(kvshape)], axis=1)
    return (h.reshape(bsz, t, d), s_out[None], kv[:, :, 0].reshape(kvshape), kv[:, :, 1].reshape(kvshape),
            win_all[:, -n_keep:])


def kernel(x_prompt, x_sample, state_hgrn, cache_kv_cmp, cache_kv_sel, state_kv_win, page_table,
           a_w_in, a_lb_logits, a_norm_g, a_w_out, b_w_qg, b_w_out, kv_w,
           cmp_pe, cmp_w1, cmp_b1, cmp_w2, ffn_w_in, ffn_w_out,
           moe_w_router, moe_b_router, moe_w_in, moe_w_out, ln_g, ln_b):
    weights = (a_w_in, a_lb_logits, a_norm_g, a_w_out, b_w_qg, b_w_out, kv_w, cmp_pe, cmp_w1, cmp_b1, cmp_w2,
               ffn_w_in, ffn_w_out, moe_w_router, moe_b_router, moe_w_in, moe_w_out, ln_g, ln_b)
    hgrn0 = jnp.zeros((x_prompt.shape[0], A_HEADS, A_DK, A_DV), F32)
    y_p, hg_p, cmp_p, sel_p, win_p = _run_group(x_prompt, hgrn0, True, None, *weights)
    y_s, hg_s, cmp_s, sel_s, win_s = _run_group(
        x_sample, state_hgrn[0], False, (cache_kv_cmp, cache_kv_sel, state_kv_win, page_table), *weights)
    return (y_p, y_s, hg_p, cmp_p, sel_p, win_p, hg_s, cmp_s, sel_s, win_s)
```

```python
import functools

import jax
import jax.numpy as jnp
from jax import lax
from jax.experimental import pallas as pl
from jax.experimental.pallas import tpu as pltpu

F32 = jnp.float32
BF16 = jnp.bfloat16

D_MODEL = 1024
DEPTH = 2
ALPHA = (2.0 * DEPTH) ** 0.25
LN_EPS = 1e-5
RMS_EPS = 1e-6
NEG_BIG = -1e30
FORCE = 1e6
PAGE_SIZE = 128

A_DK = 128
A_HEADS = D_MODEL // A_DK
A_DV = D_MODEL // A_HEADS
A_WIDTH = A_HEADS * A_DK
A_CHUNK = 64

B_HEADS = 16
B_KV = 4
B_GROUP = B_HEADS // B_KV
HEAD_DIM = D_MODEL // B_HEADS
SCALE = HEAD_DIM ** -0.5
CMP_STRIDE = 16
CMP_BLOCK = 2 * CMP_STRIDE
SEL_BLOCK = 64
TOP_N = 8
N_LOCAL = 2
WINDOW = 512

D_FF = 256 * ((8 * D_MODEL // 3 + 255) // 256)
N_EXPERTS = 8
MOE_TOP_K = 2
D_FF_E = 7 * D_MODEL // 2

VMEM_LIMIT_BYTES = 56 * 1024 * 1024


def _params(*sem):
    return pltpu.CompilerParams(dimension_semantics=sem, vmem_limit_bytes=VMEM_LIMIT_BYTES)


def _silu(x):
    return x * (1.0 / (1.0 + jnp.exp(-x)))


def _sigmoid(x):
    return 1.0 / (1.0 + jnp.exp(-x))


def _layer_norm(x, g, b):
    xc = x - jnp.mean(x, -1, keepdims=True)
    var = jnp.mean(xc * xc, -1, keepdims=True)
    return xc * lax.rsqrt(var + LN_EPS) * g + b


def _dot(a, b):
    return jnp.dot(a.astype(BF16), b.astype(BF16), preferred_element_type=F32)


def _dot_nt(a, b):
    return lax.dot_general(a.astype(BF16), b.astype(BF16), (((1,), (1,)), ((), ())),
                           preferred_element_type=F32)


def _dot_tn(a, b):
    return lax.dot_general(a.astype(BF16), b.astype(BF16), (((0,), (0,)), ((), ())),
                           preferred_element_type=F32)


def _hgrn_kernel(x_ref, s0_ref, win_ref, lbl_ref, ng_ref, wout_ref, lng_ref, lnb_ref,
                 h_ref, sout_ref, st_ref, *, layer, seg, n_seg, carry):
    rows = seg * n_seg
    x = x_ref[...].reshape(rows, D_MODEL)
    z = jnp.dot(x.astype(BF16), win_ref[...], preferred_element_type=F32)
    zq = z[:, 0 * A_WIDTH:1 * A_WIDTH]
    zf = z[:, 1 * A_WIDTH:2 * A_WIDTH]
    v = z[:, 2 * A_WIDTH:3 * A_WIDTH]
    zg = z[:, 3 * A_WIDTH:4 * A_WIDTH]

    lbl = lbl_ref[...]
    e = jnp.exp(lbl - jnp.max(lbl, axis=0, keepdims=True))
    lb = jnp.sum(e[:layer + 1], axis=0, keepdims=True) / jnp.sum(e, axis=0, keepdims=True)

    q = _silu(zq)
    f = lb + (1.0 - lb) * _sigmoid(zf)
    logf = jnp.log(f)
    k = 1.0 - f

    r_i = lax.broadcasted_iota(jnp.int32, (seg, seg), 0)
    c_i = lax.broadcasted_iota(jnp.int32, (seg, seg), 1)
    causal = c_i <= r_i
    tri = jnp.where(causal, 1.0, 0.0).astype(BF16)

    if carry:
        @pl.when(pl.program_id(1) == 0)
        def _():
            for h in range(A_HEADS):
                st_ref[h] = s0_ref[0, h].T

    o_parts = []
    for s in range(n_seg):
        sl = slice(s * seg, (s + 1) * seg)
        lf = logf[sl]
        lf_hi = lf.astype(BF16)
        lf_lo = (lf - lf_hi.astype(F32)).astype(BF16)
        g = jnp.dot(tri, lf_hi, preferred_element_type=F32) + jnp.dot(tri, lf_lo, preferred_element_type=F32)
        glast = g[seg - 1:seg, :]
        qg = q[sl] * jnp.exp(g)
        kg = k[sl] * jnp.exp(-g)
        kd = k[sl] * jnp.exp(glast - g)
        eg = jnp.exp(glast)
        vs = v[sl]
        heads = []
        for h in range(A_HEADS):
            cl = slice(h * A_DK, (h + 1) * A_DK)
            if carry:
                st = st_ref[h]
            else:
                st = s0_ref[s, h].T
            att = jnp.where(causal, _dot_nt(qg[:, cl], kg[:, cl]), 0.0)
            o = _dot_nt(qg[:, cl], st) + _dot(att, vs[:, cl])
            st_new = eg[:, cl] * st + _dot_tn(vs[:, cl], kd[:, cl])
            if carry:
                st_ref[h] = st_new
            else:
                sout_ref[s, h] = st_new.T
            o = o * lax.rsqrt(jnp.mean(o * o, -1, keepdims=True) + RMS_EPS)
            heads.append(o)
        o_parts.append(jnp.concatenate(heads, axis=1))
    o = o_parts[0] if n_seg == 1 else jnp.concatenate(o_parts, axis=0)
    o = o * ng_ref[...] * _silu(zg)
    y = jnp.dot(o.astype(BF16), wout_ref[...], preferred_element_type=F32)
    hh = _layer_norm(ALPHA * x + y, lng_ref[...], lnb_ref[...])
    h_ref[...] = hh.reshape(h_ref.shape)

    if carry:
        @pl.when(pl.program_id(1) == pl.num_programs(1) - 1)
        def _():
            for h in range(A_HEADS):
                sout_ref[0, h] = st_ref[h].T


def hgrn_layer(x, s0, w_in, lb_logits, norm_g, w_out, ln_g, ln_b, *, layer, seg, n_seg, carry):
    bsz, t, _ = x.shape
    row2 = lambda a: a.reshape(1, -1).astype(F32)
    w_in = w_in.astype(BF16)
    w_out = w_out.astype(BF16)
    const = lambda *_: (0, 0)
    if carry:
        tile = seg * n_seg
        grid = (bsz, t // tile)
        x_spec = pl.BlockSpec((1, tile, D_MODEL), lambda b, c: (b, c, 0))
        s_spec = pl.BlockSpec((1, A_HEADS, A_DK, A_DV), lambda b, c: (b, 0, 0, 0))
        sem = ("parallel", "arbitrary")
    else:
        assert t == seg
        grid = (bsz // n_seg, 1)
        x_spec = pl.BlockSpec((n_seg, seg, D_MODEL), lambda b, c: (b, 0, 0))
        s_spec = pl.BlockSpec((n_seg, A_HEADS, A_DK, A_DV), lambda b, c: (b, 0, 0, 0))
        sem = ("parallel", "arbitrary")
    kern = functools.partial(_hgrn_kernel, layer=layer, seg=seg, n_seg=n_seg, carry=carry)
    return pl.pallas_call(
        kern,
        grid=grid,
        in_specs=[
            x_spec, s_spec,
            pl.BlockSpec((D_MODEL, 4 * A_WIDTH), const),
            pl.BlockSpec(lb_logits.shape, const),
            pl.BlockSpec((1, A_WIDTH), const),
            pl.BlockSpec((A_WIDTH, D_MODEL), const),
            pl.BlockSpec((1, D_MODEL), const),
            pl.BlockSpec((1, D_MODEL), const),
        ],
        out_specs=[x_spec, s_spec],
        out_shape=[jax.ShapeDtypeStruct(x.shape, F32), jax.ShapeDtypeStruct(s0.shape, F32)],
        scratch_shapes=[pltpu.VMEM((A_HEADS, A_DV, A_DK), F32)],
        compiler_params=_params(*sem),
        name="hgrn_layer",
    )(x, s0, w_in, lb_logits.astype(F32), row2(norm_g), w_out, row2(ln_g), row2(ln_b))


def _ffn_kernel(x_ref, wa_ref, wu_ref, wo_ref, lng_ref, lnb_ref, o_ref, acc_ref, xb_ref, *, rb):
    j = pl.program_id(1)

    @pl.when(j == 0)
    def _():
        xb_ref[...] = x_ref[...].astype(BF16)

    def row_block(b, _):
        rows = pl.ds(pl.multiple_of(b * rb, rb), rb)
        xb = xb_ref[rows, :]
        a = jnp.dot(xb, wa_ref[...], preferred_element_type=F32)
        u = jnp.dot(xb, wu_ref[...], preferred_element_type=F32)
        part = jnp.dot((_silu(a) * u).astype(BF16), wo_ref[...], preferred_element_type=F32)

        @pl.when(j == 0)
        def _():
            acc_ref[rows, :] = part

        @pl.when(j > 0)
        def _():
            acc_ref[rows, :] = acc_ref[rows, :] + part
        return 0

    lax.fori_loop(0, x_ref.shape[0] // rb, row_block, 0)

    @pl.when(j == pl.num_programs(1) - 1)
    def _():
        o_ref[...] = _layer_norm(ALPHA * x_ref[...] + acc_ref[...], lng_ref[...], lnb_ref[...])


def ffn_layer(x, w_in, w_out, ln_g, ln_b, *, tm, tf):
    n, d = x.shape
    d_ff = w_out.shape[0]
    nf = d_ff // tf
    rb = min(tm, 256)
    assert n % tm == 0 and d_ff % tf == 0 and tm % rb == 0
    w_in = w_in.astype(BF16)
    w_out = w_out.astype(BF16)
    row2 = lambda a: a.reshape(1, -1).astype(F32)
    return pl.pallas_call(
        functools.partial(_ffn_kernel, rb=rb),
        grid=(n // tm, nf),
        in_specs=[
            pl.BlockSpec((tm, d), lambda i, j: (i, 0)),
            pl.BlockSpec((d, tf), lambda i, j: (0, j)),
            pl.BlockSpec((d, tf), lambda i, j: (0, j + nf)),
            pl.BlockSpec((tf, d), lambda i, j: (j, 0)),
            pl.BlockSpec((1, d), lambda i, j: (0, 0)),
            pl.BlockSpec((1, d), lambda i, j: (0, 0)),
        ],
        out_specs=pl.BlockSpec((tm, d), lambda i, j: (i, 0)),
        out_shape=jax.ShapeDtypeStruct((n, d), F32),
        scratch_shapes=[pltpu.VMEM((tm, d), F32), pltpu.VMEM((tm, d), BF16)],
        compiler_params=_params("parallel", "arbitrary"),
        name="ffn_layer",
    )(x, w_in, w_in, w_out, row2(ln_g), row2(ln_b))


def _router_gate(x, wr, br):
    logits = jnp.dot(x, wr, preferred_element_type=F32, precision=lax.Precision.HIGHEST) + br
    lane = lax.broadcasted_iota(jnp.int32, logits.shape, 1)
    m1 = jnp.max(logits, axis=-1, keepdims=True)
    i1 = jnp.min(jnp.where(logits == m1, lane, N_EXPERTS), axis=-1, keepdims=True)
    rest = jnp.where(lane == i1, -jnp.inf, logits)
    m2 = jnp.max(rest, axis=-1, keepdims=True)
    i2 = jnp.min(jnp.where(rest == m2, lane, N_EXPERTS), axis=-1, keepdims=True)
    e2 = jnp.exp(m2 - m1)
    den = 1.0 + e2
    return jnp.where(lane == i1, 1.0 / den, 0.0) + jnp.where(lane == i2, e2 / den, 0.0)


def _moe_kernel(x_ref, wr_ref, br_ref, wa_ref, wu_ref, wo_ref, lng_ref, lnb_ref, o_ref,
                gate_ref, acc_ref, tot_ref):
    e = pl.program_id(1)
    j = pl.program_id(2)
    nj = pl.num_programs(2)

    @pl.when((e == 0) & (j == 0))
    def _():
        gate_ref[...] = _router_gate(x_ref[...], wr_ref[...], br_ref[...])
        tot_ref[...] = jnp.zeros_like(tot_ref)

    xb = x_ref[...].astype(BF16)
    a = jnp.dot(xb, wa_ref[0], preferred_element_type=F32)
    u = jnp.dot(xb, wu_ref[0], preferred_element_type=F32)
    part = jnp.dot((_silu(a) * u).astype(BF16), wo_ref[0], preferred_element_type=F32)

    @pl.when(j == 0)
    def _():
        acc_ref[...] = part

    @pl.when(j > 0)
    def _():
        acc_ref[...] += part

    @pl.when(j == nj - 1)
    def _():
        gate = gate_ref[...]
        lane = lax.broadcasted_iota(jnp.int32, gate.shape, 1)
        ge = jnp.sum(jnp.where(lane == e, gate, 0.0), axis=-1, keepdims=True)
        tot_ref[...] += ge * acc_ref[...]

    @pl.when((j == nj - 1) & (e == pl.num_programs(1) - 1))
    def _():
        o_ref[...] = _layer_norm(ALPHA * x_ref[...] + tot_ref[...], lng_ref[...], lnb_ref[...])


def moe_layer_dense(x, w_router, b_router, w_in, w_out, ln_g, ln_b, *, tm, tf):
    n, d = x.shape
    ne, d_ff = w_out.shape[0], w_out.shape[1]
    nf = d_ff // tf
    assert n % tm == 0 and d_ff % tf == 0
    w_in = w_in.astype(BF16)
    w_out = w_out.astype(BF16)
    row2 = lambda a: a.reshape(1, -1).astype(F32)
    c2 = lambda i, e, j: (0, 0)
    return pl.pallas_call(
        _moe_kernel,
        grid=(n // tm, ne, nf),
        in_specs=[
            pl.BlockSpec((tm, d), lambda i, e, j: (i, 0)),
            pl.BlockSpec((d, ne), c2),
            pl.BlockSpec((1, ne), c2),
            pl.BlockSpec((1, d, tf), lambda i, e, j: (e, 0, j)),
            pl.BlockSpec((1, d, tf), lambda i, e, j: (e, 0, j + nf)),
            pl.BlockSpec((1, tf, d), lambda i, e, j: (e, j, 0)),
            pl.BlockSpec((1, d), c2),
            pl.BlockSpec((1, d), c2),
        ],
        out_specs=pl.BlockSpec((tm, d), lambda i, e, j: (i, 0)),
        out_shape=jax.ShapeDtypeStruct((n, d), F32),
        scratch_shapes=[pltpu.VMEM((tm, ne), F32), pltpu.VMEM((tm, d), F32), pltpu.VMEM((tm, d), F32)],
        compiler_params=_params("parallel", "arbitrary", "arbitrary"),
        name="moe_layer",
    )(x, w_router.astype(F32), row2(b_router), w_in, w_in, w_out, row2(ln_g), row2(ln_b))


MOE_RB = 256
MOE_TAIL = 128
MOE_ST = 512


def _router_kernel(x_ref, wrt_ref, br_ref, gate_ref, xb_ref):
    x = x_ref[...]
    logits = lax.dot_general(wrt_ref[...], x, (((1,), (1,)), ((), ())), preferred_element_type=F32,
                             precision=lax.Precision.HIGHEST) + br_ref[...]
    eidx = lax.broadcasted_iota(jnp.int32, logits.shape, 0)
    m1 = jnp.max(logits, axis=0, keepdims=True)
    i1 = jnp.min(jnp.where(logits == m1, eidx, N_EXPERTS), axis=0, keepdims=True)
    rest = jnp.where(eidx == i1, -jnp.inf, logits)
    m2 = jnp.max(rest, axis=0, keepdims=True)
    i2 = jnp.min(jnp.where(rest == m2, eidx, N_EXPERTS), axis=0, keepdims=True)
    e2 = jnp.exp(m2 - m1)
    den = 1.0 + e2
    gate_ref[...] = jnp.where(eidx == i1, 1.0 / den, 0.0) + jnp.where(eidx == i2, e2 / den, 0.0)
    xb_ref[...] = x.astype(BF16)


def moe_router(x, w_router, b_router, *, tm):
    n, d = x.shape
    return pl.pallas_call(
        _router_kernel,
        grid=(n // tm,),
        in_specs=[pl.BlockSpec((tm, d), lambda i: (i, 0)), pl.BlockSpec((N_EXPERTS, d), lambda i: (0, 0)),
                  pl.BlockSpec((N_EXPERTS, 1), lambda i: (0, 0))],
        out_specs=[pl.BlockSpec((N_EXPERTS, tm), lambda i: (0, i)), pl.BlockSpec((tm, d), lambda i: (i, 0))],
        out_shape=[jax.ShapeDtypeStruct((N_EXPERTS, n), F32), jax.ShapeDtypeStruct((n, d), BF16)],
        compiler_params=_params("parallel"),
        name="moe_router",
    )(x, w_router.T.astype(F32), b_router.reshape(N_EXPERTS, 1).astype(F32))


def _moe_routed_kernel(cum_ref, xb_ref, gate_ref, rank_ref, wa_ref, wu_ref, wo_ref, y_ref, xs_ref, acc_ref,
                       *, n_sub, st):
    i = pl.program_id(0)
    e = pl.program_id(1)
    j = pl.program_id(2)
    nj = pl.num_programs(2)
    base = (i * N_EXPERTS + e) * (n_sub + 1)
    count = cum_ref[base + n_sub]
    n_full = (count + (MOE_RB - MOE_TAIL - 1)) // MOE_RB

    def for_blocks(fn):
        def body(b, _):
            fn(pl.multiple_of(b * MOE_RB, MOE_RB), MOE_RB)
            return 0

        lax.fori_loop(0, n_full, body, 0)

        @pl.when(count > n_full * MOE_RB)
        def _():
            fn(pl.multiple_of(n_full * MOE_RB, MOE_RB), MOE_TAIL)

    def for_subtiles(r0, rb, fn):
        def body(c, _):
            @pl.when((cum_ref[base + c] < r0 + rb) & (cum_ref[base + c + 1] > r0))
            def _():
                fn(c)
            return 0

        lax.fori_loop(0, n_sub, body, 0)

    def one_hot(r0, rb, c):
        c0 = pl.multiple_of(c * st, st)
        rank = rank_ref[pl.ds(e, 1), pl.ds(c0, st)]
        gate = gate_ref[pl.ds(e, 1), pl.ds(c0, st)]
        hit = (rank == r0 + lax.broadcasted_iota(jnp.int32, (rb, st), 0)) & (gate > 0.0)
        return hit, gate, c0

    def gather_block(r0, rb):
        rows = pl.ds(r0, rb)
        xs_ref[rows, :] = jnp.zeros((rb, D_MODEL), BF16)

        def sub(c):
            hit, _, c0 = one_hot(r0, rb, c)
            part = jnp.dot(jnp.where(hit, 1.0, 0.0).astype(BF16), xb_ref[pl.ds(c0, st), :],
                           preferred_element_type=F32)
            xs_ref[rows, :] = xs_ref[rows, :] + part.astype(BF16)

        for_subtiles(r0, rb, sub)

    def ffn_block(r0, rb):
        rows = pl.ds(r0, rb)
        xs = xs_ref[rows, :]
        a = jnp.dot(xs, wa_ref[0], preferred_element_type=F32)
        u = jnp.dot(xs, wu_ref[0], preferred_element_type=F32)
        part = jnp.dot((_silu(a) * u).astype(BF16), wo_ref[0], preferred_element_type=F32)

        @pl.when(j == 0)
        def _():
            acc_ref[rows, :] = part

        @pl.when(j > 0)
        def _():
            acc_ref[rows, :] = acc_ref[rows, :] + part

    def scatter_block(r0, rb):
        rows = pl.ds(r0, rb)

        def sub(c):
            hit, gate, c0 = one_hot(r0, rb, c)
            g_rows = jnp.sum(jnp.where(hit, gate, 0.0), axis=1, keepdims=True)
            out = (acc_ref[rows, :] * g_rows).astype(BF16)
            y_ref[pl.ds(c0, st), :] += _dot_tn(jnp.where(hit, 1.0, 0.0).astype(BF16), out)

        for_subtiles(r0, rb, sub)

    @pl.when((e == 0) & (j == 0))
    def _():
        y_ref[...] = jnp.zeros_like(y_ref)

    @pl.when(j == 0)
    def _():
        for_blocks(gather_block)

    for_blocks(ffn_block)

    @pl.when(j == nj - 1)
    def _():
        for_blocks(scatter_block)


def moe_routed(xb, gate_t, w_in, w_out, *, tm, tf):
    n, d = xb.shape
    ne, d_ff = w_out.shape[0], w_out.shape[1]
    nf = d_ff // tf
    st = min(MOE_ST, tm)
    n_tiles, n_sub = n // tm, tm // st
    assert n % tm == 0 and tm % st == 0 and d_ff % tf == 0 and tm % MOE_RB == 0
    mask = (gate_t > 0.0).astype(jnp.int32).reshape(ne, n_tiles, tm)
    rank = (jnp.cumsum(mask, axis=-1) - mask).reshape(ne, n)
    cnt = mask.reshape(ne, n_tiles, n_sub, st).sum(-1)
    cum = jnp.concatenate([jnp.zeros((ne, n_tiles, 1), jnp.int32), jnp.cumsum(cnt, axis=-1)], axis=-1)
    cum = cum.transpose(1, 0, 2).reshape(-1).astype(jnp.int32)
    w_in = w_in.astype(BF16)
    w_out = w_out.astype(BF16)
    kern = functools.partial(_moe_routed_kernel, n_sub=n_sub, st=st)
    return pl.pallas_call(
        kern,
        grid_spec=pltpu.PrefetchScalarGridSpec(
            num_scalar_prefetch=1,
            grid=(n_tiles, ne, nf),
            in_specs=[
                pl.BlockSpec((tm, d), lambda i, e, j, c: (i, 0)),
                pl.BlockSpec((ne, tm), lambda i, e, j, c: (0, i)),
                pl.BlockSpec((ne, tm), lambda i, e, j, c: (0, i)),
                pl.BlockSpec((1, d, tf), lambda i, e, j, c: (e, 0, j)),
                pl.BlockSpec((1, d, tf), lambda i, e, j, c: (e, 0, j + nf)),
                pl.BlockSpec((1, tf, d), lambda i, e, j, c: (e, j, 0)),
            ],
            out_specs=pl.BlockSpec((tm, d), lambda i, e, j, c: (i, 0)),
            scratch_shapes=[pltpu.VMEM((tm, d), BF16), pltpu.VMEM((tm, d), F32)]),
        out_shape=jax.ShapeDtypeStruct((n, d), F32),
        compiler_params=_params("parallel", "arbitrary", "arbitrary"),
        name="moe_routed",
    )(cum, xb, gate_t, rank, w_in, w_in, w_out)


def _add_ln_kernel(x_ref, y_ref, lng_ref, lnb_ref, o_ref):
    o_ref[...] = _layer_norm(ALPHA * x_ref[...] + y_ref[...], lng_ref[...], lnb_ref[...])


def add_ln(x, y, ln_g, ln_b, *, tm):
    n, d = x.shape
    row2 = lambda a: a.reshape(1, -1).astype(F32)
    blk = pl.BlockSpec((tm, d), lambda i: (i, 0))
    c2 = pl.BlockSpec((1, d), lambda i: (0, 0))
    return pl.pallas_call(
        _add_ln_kernel, grid=(n // tm,), in_specs=[blk, blk, c2, c2], out_specs=blk,
        out_shape=jax.ShapeDtypeStruct((n, d), F32), compiler_params=_params("parallel"), name="add_ln",
    )(x, y, row2(ln_g), row2(ln_b))


def moe_layer(x, w_router, b_router, w_in, w_out, ln_g, ln_b, *, tm, tf):
    n = x.shape[0]
    gate_t, xb = moe_router(x, w_router, b_router, tm=_tile(n, 1024))
    y = moe_routed(xb, gate_t, w_in, w_out, tm=tm, tf=tf)
    return add_ln(x, y, ln_g, ln_b, tm=_tile(n, 1024))


def _proj_kernel(x_ref, w_ref, o_ref):
    o_ref[...] = jnp.dot(x_ref[...].astype(BF16), w_ref[...], preferred_element_type=F32)


def proj(x, w, *, tm):
    n, d = x.shape
    m = w.shape[1]
    return pl.pallas_call(
        _proj_kernel,
        grid=(n // tm,),
        in_specs=[pl.BlockSpec((tm, d), lambda i: (i, 0)), pl.BlockSpec((d, m), lambda i: (0, 0))],
        out_specs=pl.BlockSpec((tm, m), lambda i: (i, 0)),
        out_shape=jax.ShapeDtypeStruct((n, m), F32),
        compiler_params=_params("parallel"),
        name="proj",
    )(x, w.astype(BF16))


def _proj_ln_kernel(o_ref, x_ref, w_ref, lng_ref, lnb_ref, h_ref):
    y = jnp.dot(o_ref[...].astype(BF16), w_ref[...], preferred_element_type=F32)
    h_ref[...] = _layer_norm(ALPHA * x_ref[...] + y, lng_ref[...], lnb_ref[...])


def proj_ln(o, x, w, ln_g, ln_b, *, tm):
    n, d = x.shape
    row2 = lambda a: a.reshape(1, -1).astype(F32)
    c2 = lambda i: (0, 0)
    return pl.pallas_call(
        _proj_ln_kernel,
        grid=(n // tm,),
        in_specs=[pl.BlockSpec((tm, o.shape[1]), lambda i: (i, 0)), pl.BlockSpec((tm, d), lambda i: (i, 0)),
                  pl.BlockSpec(w.shape, c2), pl.BlockSpec((1, d), c2), pl.BlockSpec((1, d), c2)],
        out_specs=pl.BlockSpec((tm, d), lambda i: (i, 0)),
        out_shape=jax.ShapeDtypeStruct((n, d), F32),
        compiler_params=_params("parallel"),
        name="proj_ln",
    )(o, x, w.astype(BF16), row2(ln_g), row2(ln_b))


N_KV_COLS = 6 * B_KV * HEAD_DIM
KV_SET = 2 * B_KV * HEAD_DIM
N_Q_COLS = B_HEADS * HEAD_DIM
GATE_LANES = 128
LOG2_E = 1.4426950408889634
V_AUG = 2 * HEAD_DIM
ACC_ROWS = HEAD_DIM + 8


def _nsa_proj_kernel(x_ref, w_ref, cmp_ref, sel_ref, win_ref, q_ref, ks_ref, vs_ref, kw_ref, vw_ref, g_ref):
    z = jnp.dot(x_ref[0].astype(BF16), w_ref[...], preferred_element_type=F32)
    cmp_ref[0] = z[:, 0:KV_SET]
    sel_ref[0] = z[:, KV_SET:2 * KV_SET]
    win_ref[0] = z[:, 2 * KV_SET:3 * KV_SET]
    half = B_KV * HEAD_DIM
    lane = lax.broadcasted_iota(jnp.int32, (z.shape[0], V_AUG), 1)

    def v_aug(lo):
        return jnp.where(lane < HEAD_DIM, z[:, lo:lo + V_AUG], jnp.where(lane == HEAD_DIM, 1.0, 0.0)).astype(BF16)

    for g in range(B_KV):
        lo = KV_SET + g * HEAD_DIM
        ks_ref[0, g] = z[:, lo:lo + HEAD_DIM].astype(BF16)
        vs_ref[0, g] = v_aug(lo + half)
        lo = 2 * KV_SET + g * HEAD_DIM
        kw_ref[0, g] = z[:, lo:lo + HEAD_DIM].astype(BF16)
        vw_ref[0, g] = v_aug(lo + half)
    for h in range(B_HEADS):
        lo = N_KV_COLS + h * HEAD_DIM
        q_ref[0, h] = (z[:, lo:lo + HEAD_DIM] * (SCALE * LOG2_E)).astype(BF16)
    g_ref[0] = _sigmoid(z[:, N_KV_COLS + N_Q_COLS:N_KV_COLS + N_Q_COLS + GATE_LANES])


def nsa_proj(h, kv_w, w_qg, *, tm):
    bsz, t, d = h.shape
    w = jnp.concatenate([kv_w, w_qg], axis=1)
    w = jnp.pad(w, ((0, 0), (0, N_KV_COLS + N_Q_COLS + GATE_LANES - w.shape[1]))).astype(BF16)
    row = lambda: pl.BlockSpec((1, tm, KV_SET), lambda b, i: (b, i, 0))
    hm = lambda nh, w=HEAD_DIM: pl.BlockSpec((1, nh, tm, w), lambda b, i: (b, 0, i, 0))
    sds = jax.ShapeDtypeStruct
    kshape, vshape = sds((bsz, B_KV, t, HEAD_DIM), BF16), sds((bsz, B_KV, t, V_AUG), BF16)
    return pl.pallas_call(
        _nsa_proj_kernel,
        grid=(bsz, t // tm),
        in_specs=[pl.BlockSpec((1, tm, d), lambda b, i: (b, i, 0)), pl.BlockSpec(w.shape, lambda b, i: (0, 0))],
        out_specs=[row(), row(), row(), hm(B_HEADS), hm(B_KV), hm(B_KV, V_AUG), hm(B_KV), hm(B_KV, V_AUG),
                   pl.BlockSpec((1, tm, GATE_LANES), lambda b, i: (b, i, 0))],
        out_shape=[sds((bsz, t, KV_SET), F32)] * 3 + [sds((bsz, B_HEADS, t, HEAD_DIM), BF16)]
        + [kshape, vshape, kshape, vshape] + [sds((bsz, t, GATE_LANES), F32)],
        compiler_params=_params("parallel", "parallel"),
        name="nsa_proj",
    )(h, w)


PAIR = 2 * HEAD_DIM


def _compress_kernel(*refs, n_in, head_major, paged):
    if paged:
        refs = refs[1:]
    x_refs = refs[:n_in]
    wp_ref, u_ref, b1_ref, w2_ref, kc_ref, vc_ref, carry_ref = refs[n_in:]
    i = pl.program_id(1)

    @pl.when(i == 0)
    def _():
        carry_ref[...] = jnp.zeros_like(carry_ref)

    cpp = PAGE_SIZE // CMP_STRIDE
    pr = lax.broadcasted_iota(jnp.int32, (PAGE_SIZE, PAGE_SIZE), 0)
    pc = lax.broadcasted_iota(jnp.int32, (PAGE_SIZE, PAGE_SIZE), 1)
    perm = jnp.where(pc == CMP_STRIDE * (pr % cpp) + pr // cpp, 1.0, 0.0).astype(BF16)
    pages = []
    for r in x_refs:
        if paged:
            pages.append(_dot_nt(perm, r[0]))
        else:
            for p0 in range(0, r.shape[1], PAGE_SIZE):
                pages.append(jnp.dot(perm, r[0, p0:p0 + PAGE_SIZE, :].astype(BF16), preferred_element_type=F32))
    ch = len(pages) * cpp
    row0 = lax.broadcasted_iota(jnp.int32, (ch, PAIR), 0) == 0

    def chunk_rows(s, lo):
        parts = [pg[s * cpp:(s + 1) * cpp, lo:lo + PAIR] for pg in pages]
        return parts[0] if len(parts) == 1 else jnp.concatenate(parts, axis=0)
    for k in range(2):
        wp = wp_ref[k]
        pbm = jnp.dot(u_ref[k], wp, preferred_element_type=F32)
        pb = pbm[0:1, 0:PAIR] + pbm[1:2, PAIR:2 * PAIR] + b1_ref[k]
        for gp in range(B_KV // 2):
            base = k * B_KV * HEAD_DIM + gp * PAIR
            lhs = jnp.concatenate([chunk_rows(s, base) for s in range(CMP_STRIDE)], axis=1).astype(BF16)
            r = jnp.dot(lhs, wp, preferred_element_type=F32)
            first, second = r[:, 0:PAIR], r[:, PAIR:2 * PAIR]
            slot = k * (B_KV // 2) + gp
            prev = carry_ref[slot]
            shifted = jnp.where(row0, prev[7:8, :], pltpu.roll(first, 1, axis=0))
            carry_ref[slot] = first[ch - 8:ch, :]
            hid = jax.nn.gelu(shifted + second + pb)
            out = jnp.dot(hid.astype(BF16), w2_ref[k], preferred_element_type=F32).astype(BF16)
            dst = kc_ref if k == 0 else vc_ref
            if head_major:
                for g2 in range(2):
                    dst[0, 2 * gp + g2] = out[:, g2 * HEAD_DIM:(g2 + 1) * HEAD_DIM]
            else:
                dst[0, :, gp * PAIR:(gp + 1) * PAIR] = out


def _compress_weights(cmp_pe, cmp_w1, cmp_b1, cmp_w2):
    eye2 = jnp.eye(2, dtype=F32)
    w1 = cmp_w1.reshape(2, CMP_STRIDE, 2, HEAD_DIM, HEAD_DIM)
    wp = jnp.einsum('fskdh,ab->ksadfbh', w1, eye2).reshape(2, CMP_STRIDE * PAIR, 2 * PAIR)
    pe = cmp_pe.reshape(2, CMP_STRIDE, 2, HEAD_DIM)
    u = jnp.broadcast_to(pe.transpose(2, 0, 1, 3)[:, :, :, None, :], (2, 2, CMP_STRIDE, 2, HEAD_DIM))
    u = jnp.pad(u.reshape(2, 2, CMP_STRIDE * PAIR), ((0, 0), (0, 6), (0, 0)))
    b1 = jnp.tile(cmp_b1, (1, 2)).reshape(2, 1, PAIR)
    w2 = jnp.einsum('khd,ab->kahbd', cmp_w2, eye2).reshape(2, PAIR, PAIR)
    return wp.astype(BF16), u.astype(BF16), b1.astype(F32), w2.astype(BF16)


def compress(x, page_table, cmp_pe, cmp_w1, cmp_b1, cmp_w2, *, ch, head_major):
    wp, u, b1, w2 = _compress_weights(cmp_pe, cmp_w1, cmp_b1, cmp_w2)
    paged = page_table is not None
    if paged:
        bsz, n_pages = page_table.shape
        n_in = ch * CMP_STRIDE // PAGE_SIZE
        n_chunks = n_pages * PAGE_SIZE // CMP_STRIDE
        x_specs = [pl.BlockSpec((1, KV_SET, PAGE_SIZE), functools.partial(
            lambda b, i, pt, j: (pt[b, i * n_in + j], 0, 0), j=j)) for j in range(n_in)]
        cm = lambda f: (lambda b, i, pt: f(b, i))
    else:
        bsz, t, _ = x.shape
        n_chunks = t // CMP_STRIDE
        n_in = 1
        x_specs = [pl.BlockSpec((1, ch * CMP_STRIDE, KV_SET), lambda b, i: (b, i, 0))]
        cm = lambda f: f
    c3 = cm(lambda b, i: (0, 0, 0))
    if head_major:
        o_spec = pl.BlockSpec((1, B_KV, ch, HEAD_DIM), cm(lambda b, i: (b, 0, i, 0)))
        o_shape = jax.ShapeDtypeStruct((bsz, B_KV, n_chunks, HEAD_DIM), BF16)
    else:
        o_spec = pl.BlockSpec((1, ch, B_KV * HEAD_DIM), cm(lambda b, i: (b, i, 0)))
        o_shape = jax.ShapeDtypeStruct((bsz, n_chunks, B_KV * HEAD_DIM), BF16)
    in_specs = x_specs + [pl.BlockSpec(wp.shape, c3), pl.BlockSpec(u.shape, c3),
                          pl.BlockSpec(b1.shape, c3), pl.BlockSpec(w2.shape, c3)]
    grid = (bsz, n_chunks // ch)
    scratch = [pltpu.VMEM((2 * (B_KV // 2), 8, PAIR), F32)]
    kern = functools.partial(_compress_kernel, n_in=n_in, head_major=head_major, paged=paged)
    if paged:
        grid_spec = pltpu.PrefetchScalarGridSpec(num_scalar_prefetch=1, grid=grid, in_specs=in_specs,
                                                 out_specs=[o_spec, o_spec], scratch_shapes=scratch)
        args = (page_table,) + (x,) * n_in
    else:
        grid_spec = pl.GridSpec(grid=grid, in_specs=in_specs, out_specs=[o_spec, o_spec], scratch_shapes=scratch)
        args = (x,)
    return pl.pallas_call(
        kern, grid_spec=grid_spec, out_shape=[o_shape, o_shape],
        compiler_params=_params("parallel", "arbitrary"), name="compress",
    )(*args, wp, u, b1, w2)


def _select_blocks(imp, blk, cur, n_sel, axis):
    valid = blk <= cur
    forced = (blk == 0) | (valid & (blk > cur - N_LOCAL))
    score = jnp.where(forced, FORCE, jnp.where(valid, imp, -FORCE))
    score = jnp.where(blk < n_sel, score, -jnp.inf)
    out = jnp.full(score.shape, NEG_BIG, F32)
    big = jnp.int32(2 ** 30)
    for _ in range(min(TOP_N, n_sel)):
        m = jnp.max(score, axis=axis, keepdims=True)
        first = jnp.min(jnp.where(score == m, blk, big), axis=axis, keepdims=True)
        pick = blk == first
        out = jnp.where(pick, 0.0, out)
        score = jnp.where(pick, -jnp.inf, score)
    return jnp.where(valid, out, NEG_BIG)


def _overlap(c1, j, n_sel):
    c0 = (c1 - 1) * CMP_STRIDE
    j0 = j * SEL_BLOCK
    return ((c1 >= 1) & (j < n_sel) & (c0 <= j0 + SEL_BLOCK - 1) & (c0 + CMP_BLOCK - 1 >= j0)).astype(F32)


def _rank_select(imp, blk, cur, n_sel):
    valid = blk <= cur
    forced = (blk == 0) | (valid & (blk > cur - N_LOCAL))
    score = jnp.where(forced, FORCE, jnp.where(valid, imp, -FORCE))
    rank = jnp.zeros(score.shape, F32)
    for jp in range(n_sel):
        row = score[jp:jp + 1, :]
        gt = jnp.where(row > score, 1.0, 0.0)
        ge = jnp.where(row >= score, 1.0, 0.0)
        rank = rank + jnp.where(blk > jp, ge, gt)
    keep = jnp.where(rank < TOP_N, 0.0, NEG_BIG)
    return jnp.where(valid, jnp.where(blk < n_sel, keep, NEG_BIG), NEG_BIG)


def _online_softmax_t(s, m, acc, v):
    m_new = jnp.maximum(m, jnp.max(s, axis=0, keepdims=True))
    p = jnp.exp2((s - m_new).astype(BF16))
    acc = jnp.exp2(m - m_new) * acc + _dot_tn(v, p)[:ACC_ROWS]
    return m_new, acc


def _prompt_attn_kernel(q_ref, kc_ref, vc_ref, ks_ref, vs_ref, kw_ref, vw_ref, g_ref, o_ref, gt_ref,
                        *, tq, tk, n_sel, nb):
    g = pl.program_id(2)
    t0 = pl.program_id(1) * tq
    cols = B_GROUP * tq
    q = q_ref[0].reshape(cols, HEAD_DIM)
    tpos = t0 + lax.broadcasted_iota(jnp.int32, (1, tq), 1)
    nch = kc_ref.shape[2]
    rep = lambda a: jnp.concatenate([a] * B_GROUP, axis=1)

    c1 = lax.broadcasted_iota(jnp.int32, (nch, 1), 0)
    cmask = rep((c1 >= 1) & (c1 * CMP_STRIDE + (CMP_BLOCK - CMP_STRIDE - 1) <= tpos))
    s = jnp.where(cmask, _dot_nt(kc_ref[0, 0], q), NEG_BIG)
    m = jnp.max(s, axis=0, keepdims=True)
    e = jnp.where(cmask, jnp.exp2(s - m), 0.0)
    l = jnp.sum(e, axis=0, keepdims=True)
    p = e * (1.0 / jnp.where(l == 0.0, 1.0, l))
    o_cmp = _dot_tn(vc_ref[0, 0], p)

    p4 = p[:, 0:tq]
    for r in range(1, B_GROUP):
        p4 = p4 + p[:, r * tq:(r + 1) * tq]
    ov_t = _overlap(lax.broadcasted_iota(jnp.int32, (nb, nch), 1),
                    lax.broadcasted_iota(jnp.int32, (nb, nch), 0), n_sel)
    imp = jnp.dot(ov_t, p4, preferred_element_type=F32, precision=lax.Precision.HIGHEST)
    blk = lax.broadcasted_iota(jnp.int32, (nb, tq), 0)
    bias = _rank_select(imp, blk, tpos // SEL_BLOCK, n_sel).astype(BF16)

    def sel_scores(kt):
        k0 = pl.multiple_of(kt * tk, tk)
        kpos = k0 + lax.broadcasted_iota(jnp.int32, (tk, 1), 0)
        blk_of_key = (k0 + lax.broadcasted_iota(jnp.int32, (tk, nb), 0)) // SEL_BLOCK
        onehot = jnp.where(blk_of_key == lax.broadcasted_iota(jnp.int32, (tk, nb), 1), 1.0, 0.0).astype(BF16)
        mb = jnp.dot(onehot, bias, preferred_element_type=F32)
        mb = jnp.where(kpos <= tpos, mb, NEG_BIG)
        return _dot_nt(ks_ref[0, 0, pl.ds(k0, tk), :], q) + rep(mb), vs_ref[0, 0, pl.ds(k0, tk), :]

    def win_scores(kt):
        k0 = pl.multiple_of(kt * tk, tk)
        kpos = k0 + lax.broadcasted_iota(jnp.int32, (tk, 1), 0)
        mb = jnp.where((kpos <= tpos) & (kpos >= tpos - WINDOW), 0.0, NEG_BIG)
        return _dot_nt(kw_ref[0, 0, pl.ds(k0, tk), :], q) + rep(mb), vw_ref[0, 0, pl.ds(k0, tk), :]

    def sel_only(kt, carry):
        s, v = sel_scores(kt)
        return _online_softmax_t(s, *carry, v)

    init = (jnp.full((1, cols), NEG_BIG, F32), jnp.zeros((ACC_ROWS, cols), F32))
    hi = (t0 + tq - 1) // tk + 1
    lo_w = jnp.maximum(t0 - WINDOW, 0) // tk

    def both(kt, carry):
        s, v = sel_scores(kt)
        sw, vw = win_scores(kt)
        return _online_softmax_t(s, *carry[:2], v) + _online_softmax_t(sw, *carry[2:], vw)

    sel_state = lax.fori_loop(0, lo_w, sel_only, init)
    _, acc_s, _, acc_w = lax.fori_loop(lo_w, hi, both, sel_state + init)

    gt_ref[...] = g_ref[0].T
    gate = lambda br: jnp.concatenate(
        [gt_ref[pl.ds(g * (3 * B_GROUP) + 3 * r + br, 1), :] for r in range(B_GROUP)], axis=1)
    norm = lambda acc: acc[:HEAD_DIM] * (1.0 / acc[HEAD_DIM:HEAD_DIM + 1])
    o = gate(0) * o_cmp + gate(1) * norm(acc_s) + gate(2) * norm(acc_w)
    for r in range(B_GROUP):
        o_ref[0, :, r * HEAD_DIM:(r + 1) * HEAD_DIM] = o[:, r * tq:(r + 1) * tq].T.astype(o_ref.dtype)


def nsa_prompt_attn(q, kc, vc, ks, vs, kw, vw, gates, *, tq, tk):
    bsz, _, t, _ = q.shape
    nch = kc.shape[2]
    n_sel = -(-t // SEL_BLOCK)
    nb = -(-n_sel // 16) * 16
    assert t % tq == 0 and t % tk == 0 and tq % 128 == 0
    seq = lambda n, w=HEAD_DIM: pl.BlockSpec((1, 1, n, w), lambda b, i, g: (b, g, 0, 0))
    kern = functools.partial(_prompt_attn_kernel, tq=tq, tk=tk, n_sel=n_sel, nb=nb)
    return pl.pallas_call(
        kern,
        grid=(bsz, t // tq, B_KV),
        in_specs=[pl.BlockSpec((1, B_GROUP, tq, HEAD_DIM), lambda b, i, g: (b, g, i, 0)),
                  seq(nch), seq(nch), seq(t), seq(t, V_AUG), seq(t), seq(t, V_AUG),
                  pl.BlockSpec((1, tq, GATE_LANES), lambda b, i, g: (b, i, 0))],
        out_specs=pl.BlockSpec((1, tq, B_GROUP * HEAD_DIM), lambda b, i, g: (b, i, g)),
        out_shape=jax.ShapeDtypeStruct((bsz, t, N_Q_COLS), BF16),
        scratch_shapes=[pltpu.VMEM((GATE_LANES, tq), F32)],
        compiler_params=_params("parallel", "parallel", "arbitrary"),
        name="nsa_prompt_attn",
    )(q, kc, vc, ks, vs, kw, vw, gates)


KV_HALF = B_KV * HEAD_DIM
Q_COLS = B_GROUP * B_KV * 8


def _sample_attn_kernel(*refs, n_pages_step, past, t_new, n_keep):
    pt_ref, qbd_ref, gl_ref, kc_ref, vc_ref, swin_ref, nsel_ref, nwin_ref = refs[:8]
    page_refs = refs[8:8 + n_pages_step]
    out_ref, bias_ref, m_ref, l_ref, acc_ref, oth_ref = refs[8 + n_pages_step:]
    del pt_ref
    i = pl.program_id(1)
    qbd = qbd_ref[0]
    ncol = qbd.shape[1]
    col = lax.broadcasted_iota(jnp.int32, (1, ncol), 1)
    tcol = col % t_new
    qpos = past + tcol
    sig = _sigmoid(gl_ref[0])
    n_sel = -(-(past + t_new) // SEL_BLOCK)
    blk_step = n_pages_step * PAGE_SIZE // SEL_BLOCK
    pad_rows = 8

    def scores(k):
        return jnp.dot(k.astype(BF16), qbd, preferred_element_type=F32) * SCALE

    @pl.when(i == 0)
    def _():
        nch = kc_ref.shape[1]
        c1 = lax.broadcasted_iota(jnp.int32, (nch, 1), 0)
        cmask = (c1 >= 1) & (c1 * CMP_STRIDE + (CMP_BLOCK - CMP_STRIDE - 1) <= qpos)
        s = jnp.where(cmask, scores(kc_ref[0]), NEG_BIG)
        m = jnp.max(s, axis=0, keepdims=True)
        e = jnp.where(cmask, jnp.exp(s - m), 0.0)
        l = jnp.sum(e, axis=0, keepdims=True)
        p = e / jnp.where(l == 0.0, 1.0, l)
        o_cmp = _dot_tn(vc_ref[0], p)
        nb = bias_ref.shape[0]
        ov_t = _overlap(lax.broadcasted_iota(jnp.int32, (nb, nch), 1),
                        lax.broadcasted_iota(jnp.int32, (nb, nch), 0), n_sel)
        imp = jnp.dot(ov_t, p, preferred_element_type=F32, precision=lax.Precision.HIGHEST)
        per = ncol // B_GROUP
        imp = imp + pltpu.roll(imp, per, axis=1) + pltpu.roll(imp, 2 * per, axis=1) + pltpu.roll(imp, 3 * per, axis=1)
        blk = lax.broadcasted_iota(jnp.int32, (nb, ncol), 0)
        bias_ref[...] = _select_blocks(imp, blk, qpos // SEL_BLOCK, n_sel, axis=0)
        kv_w = jnp.concatenate([swin_ref[0], nwin_ref[0], jnp.zeros((pad_rows, 2 * KV_HALF), F32)], axis=0)
        nw = kv_w.shape[0]
        wi = lax.broadcasted_iota(jnp.int32, (nw, 1), 0)
        wpos = past - n_keep + wi
        wmask = (wi < n_keep + t_new) & (wpos <= qpos) & (wpos >= qpos - WINDOW) & (wpos >= 0)
        s = jnp.where(wmask, scores(kv_w[:, :KV_HALF]), NEG_BIG)
        m = jnp.max(s, axis=0, keepdims=True)
        e = jnp.where(wmask, jnp.exp(s - m), 0.0)
        l = jnp.sum(e, axis=0, keepdims=True)
        o_win = _dot_tn(kv_w[:, KV_HALF:], e / jnp.where(l == 0.0, 1.0, l))
        oth_ref[...] = sig[0:1] * o_cmp + sig[2:3] * o_win
        m_ref[...] = jnp.full(m_ref.shape, NEG_BIG, F32)
        l_ref[...] = jnp.zeros(l_ref.shape, F32)
        acc_ref[...] = jnp.zeros(acc_ref.shape, F32)

    def update(s, v=None, v_t=None):
        m_new = jnp.maximum(m_ref[...], jnp.max(s, axis=0, keepdims=True))
        alpha = jnp.exp(m_ref[...] - m_new)
        p = jnp.exp(s - m_new)
        l_ref[...] = alpha * l_ref[...] + jnp.sum(p, axis=0, keepdims=True)
        pv = _dot_tn(v, p) if v_t is None else _dot(v_t, p)
        acc_ref[...] = alpha * acc_ref[...] + pv
        m_ref[...] = m_new

    bias = bias_ref[pl.ds(pl.multiple_of(i * blk_step, blk_step), blk_step), :]
    per_page = PAGE_SIZE // SEL_BLOCK
    s_parts, v_parts = [], []
    for p in range(n_pages_step):
        page = page_refs[p][0]
        mb = jnp.concatenate([jnp.broadcast_to(bias[per_page * p + j:per_page * p + j + 1, :], (SEL_BLOCK, ncol))
                              for j in range(per_page)], axis=0)
        s_parts.append(_dot_tn(page[:KV_HALF], qbd) * SCALE + mb)
        v_parts.append(page[KV_HALF:].astype(BF16))
    update(jnp.concatenate(s_parts, axis=0), v_t=jnp.concatenate(v_parts, axis=1))

    @pl.when(i == pl.num_programs(1) - 1)
    def _():
        kv_n = jnp.concatenate([nsel_ref[0], jnp.zeros((pad_rows, 2 * KV_HALF), F32)], axis=0)
        u = lax.broadcasted_iota(jnp.int32, (kv_n.shape[0], 1), 0)
        nb_new = past // SEL_BLOCK
        s = scores(kv_n[:, :KV_HALF]) + bias_ref[nb_new:nb_new + 1, :]
        update(jnp.where((u < t_new) & (past + u <= qpos), s, NEG_BIG), kv_n[:, KV_HALF:])
        o = oth_ref[...] + sig[1:2] * acc_ref[...] / l_ref[...]
        g_row = lax.broadcasted_iota(jnp.int32, o.shape, 0) // HEAD_DIM
        g_col = (lax.broadcasted_iota(jnp.int32, o.shape, 1) // t_new) % B_KV
        o = jnp.where(g_row == g_col, o, 0.0)
        out_ref[0] = o[0:HEAD_DIM] + o[HEAD_DIM:2 * HEAD_DIM] + o[2 * HEAD_DIM:3 * HEAD_DIM] + o[3 * HEAD_DIM:]


def nsa_sample_attn(q, gate_logits, kc, vc, cache_kv_sel, state_kv_win, kv_sel_new, kv_win_new, page_table,
                    *, n_pages_step):
    bsz, t, _ = q.shape
    n_pages = page_table.shape[1]
    past = n_pages * PAGE_SIZE
    n_keep = state_kv_win.shape[1]
    assert t * B_GROUP * B_KV == Q_COLS and past % SEL_BLOCK == 0 and n_pages % n_pages_step == 0
    qt = q.reshape(bsz, t, B_KV, B_GROUP, HEAD_DIM).transpose(0, 2, 4, 3, 1)
    qbd = jnp.einsum('bgdrt,gh->bgdrht', qt, jnp.eye(B_KV, dtype=F32)).reshape(bsz, KV_HALF, Q_COLS).astype(BF16)
    gl = gate_logits.reshape(bsz, t, B_KV, B_GROUP, 3).transpose(0, 4, 3, 2, 1).reshape(bsz, 3, Q_COLS)
    gl = jnp.pad(gl, ((0, 0), (0, 5), (0, 0)))
    n_blk = past // SEL_BLOCK + 8
    per_b = lambda shape: pl.BlockSpec((1,) + shape, lambda b, i, pt: (b, 0, 0))
    page_specs = [pl.BlockSpec((1, 2 * KV_HALF, PAGE_SIZE), functools.partial(
        lambda b, i, pt, j: (pt[b, i * n_pages_step + j], 0, 0), j=j)) for j in range(n_pages_step)]
    kern = functools.partial(_sample_attn_kernel, n_pages_step=n_pages_step, past=past, t_new=t, n_keep=n_keep)
    out = pl.pallas_call(
        kern,
        grid_spec=pltpu.PrefetchScalarGridSpec(
            num_scalar_prefetch=1,
            grid=(bsz, n_pages // n_pages_step),
            in_specs=[per_b((KV_HALF, Q_COLS)), per_b((8, Q_COLS)), per_b(kc.shape[1:]), per_b(vc.shape[1:]),
                      per_b((n_keep, 2 * KV_HALF)), per_b((t, 2 * KV_HALF)), per_b((t, 2 * KV_HALF))] + page_specs,
            out_specs=pl.BlockSpec((1, HEAD_DIM, Q_COLS), lambda b, i, pt: (b, 0, 0)),
            scratch_shapes=[pltpu.VMEM((n_blk, Q_COLS), F32), pltpu.VMEM((1, Q_COLS), F32),
                            pltpu.VMEM((1, Q_COLS), F32), pltpu.VMEM((KV_HALF, Q_COLS), F32),
                            pltpu.VMEM((KV_HALF, Q_COLS), F32)]),
        out_shape=jax.ShapeDtypeStruct((bsz, HEAD_DIM, Q_COLS), F32),
        compiler_params=_params("parallel", "arbitrary"),
        name="nsa_sample_attn",
    )(page_table, qbd, gl, kc, vc, state_kv_win, kv_sel_new, kv_win_new, *([cache_kv_sel] * n_pages_step))
    return out.reshape(bsz, HEAD_DIM, B_GROUP, B_KV, t).transpose(0, 4, 3, 2, 1).reshape(bsz, t, N_Q_COLS)


def _tile(n, pref):
    return pref if n % pref == 0 else n


def _run_group(x, s0, is_prompt, caches, a_w_in, a_lb_logits, a_norm_g, a_w_out, b_w_qg, b_w_out, kv_w,
               cmp_pe, cmp_w1, cmp_b1, cmp_w2, ffn_w_in, ffn_w_out, moe_w_router, moe_b_router,
               moe_w_in, moe_w_out, ln_g, ln_b):
    bsz, t, d = x.shape
    n = bsz * t
    if is_prompt:
        h, s_out = hgrn_layer(x, s0, a_w_in[0], a_lb_logits, a_norm_g[0], a_w_out[0], ln_g[0, 0], ln_b[0, 0],
                              layer=0, seg=A_CHUNK, n_seg=8, carry=True)
    else:
        h, s_out = hgrn_layer(x, s0, a_w_in[0], a_lb_logits, a_norm_g[0], a_w_out[0], ln_g[0, 0], ln_b[0, 0],
                              layer=0, seg=t, n_seg=8, carry=False)
    h = h.reshape(n, d)
    h = ffn_layer(h, ffn_w_in[0], ffn_w_out[0], ln_g[0, 1], ln_b[0, 1], tm=_tile(n, 1024), tf=D_FF // 2)
    nq = B_HEADS * HEAD_DIM
    if is_prompt:
        kv_cmp, kv_sel, kv_win, qh, ksh, vsh, kwh, vwh, gts = nsa_proj(h.reshape(bsz, t, d), kv_w, b_w_qg[0], tm=512)
        n_ch = t // CMP_STRIDE
        kc, vc = compress(kv_cmp, None, cmp_pe, cmp_w1, cmp_b1, cmp_w2,
                          ch=n_ch, head_major=True)
        o = nsa_prompt_attn(qh, kc, vc, ksh, vsh, kwh, vwh, gts, tq=256, tk=256)
        h = proj_ln(o.reshape(n, nq), h, b_w_out[0], ln_g[1, 0], ln_b[1, 0], tm=_tile(n, 512))
        h = moe_layer(h, moe_w_router[0], moe_b_router[0], moe_w_in[0], moe_w_out[0], ln_g[1, 1], ln_b[1, 1],
                      tm=_tile(n, 2048), tf=896)
        kvshape = (bsz, t, 2, B_KV, HEAD_DIM)
        return (h.reshape(bsz, t, d), s_out[None], kv_cmp.reshape(kvshape), kv_sel.reshape(kvshape),
                kv_win.reshape(kvshape)[:, t - min(WINDOW, t):])
    cache_kv_cmp, cache_kv_sel, state_kv_win, page_table = caches
    n_pool = cache_kv_cmp.shape[0]
    n_keep = state_kv_win.shape[1]
    w_cat = jnp.concatenate([kv_w, b_w_qg[0]], axis=1)
    w_cat = jnp.pad(w_cat, ((0, 0), (0, (-w_cat.shape[1]) % 128)))
    z = proj(h, w_cat, tm=_tile(n, 512))
    kv = z[:, :N_KV_COLS].reshape(bsz, t, 3, KV_SET)
    q = z[:, N_KV_COLS:N_KV_COLS + nq].reshape(bsz, t, nq)
    gate_logits = z[:, N_KV_COLS + nq:N_KV_COLS + nq + 3 * B_HEADS].reshape(bsz, t, 3 * B_HEADS)
    assert (page_table.shape[1] * PAGE_SIZE + t) // CMP_STRIDE == page_table.shape[1] * PAGE_SIZE // CMP_STRIDE
    token_minor = lambda c: c.transpose(0, 2, 3, 4, 1).reshape(n_pool, KV_SET, PAGE_SIZE)
    kc, vc = compress(token_minor(cache_kv_cmp), page_table,
                      cmp_pe, cmp_w1, cmp_b1, cmp_w2, ch=256, head_major=False)
    o = nsa_sample_attn(q, gate_logits, kc, vc, token_minor(cache_kv_sel),
                        state_kv_win.reshape(bsz, n_keep, KV_SET), kv[:, :, 1], kv[:, :, 2], page_table,
                        n_pages_step=32)
    h = proj_ln(o.reshape(n, nq), h, b_w_out[0], ln_g[1, 0], ln_b[1, 0], tm=_tile(n, 512))
    h = moe_layer(h, moe_w_router[0], moe_b_router[0], moe_w_in[0], moe_w_out[0], ln_g[1, 1], ln_b[1, 1],
                  tm=_tile(n, 2048), tf=896)
    kvshape = (bsz, t, 2, B_KV, HEAD_DIM)
    win_all = jnp.concatenate([state_kv_win, kv[:, :, 2].reshape(kvshape)], axis=1)
    return (h.reshape(bsz, t, d), s_out[None], kv[:, :, 0].reshape(kvshape), kv[:, :, 1].reshape(kvshape),
            win_all[:, -n_keep:])


def kernel(x_prompt, x_sample, state_hgrn, cache_kv_cmp, cache_kv_sel, state_kv_win, page_table,
           a_w_in, a_lb_logits, a_norm_g, a_w_out, b_w_qg, b_w_out, kv_w,
           cmp_pe, cmp_w1, cmp_b1, cmp_w2, ffn_w_in, ffn_w_out,
           moe_w_router, moe_b_router, moe_w_in, moe_w_out, ln_g, ln_b):
    weights = (a_w_in, a_lb_logits, a_norm_g, a_w_out, b_w_qg, b_w_out, kv_w, cmp_pe, cmp_w1, cmp_b1, cmp_w2,
               ffn_w_in, ffn_w_out, moe_w_router, moe_b_router, moe_w_in, moe_w_out, ln_g, ln_b)
    hgrn0 = jnp.zeros((x_prompt.shape[0], A_HEADS, A_DK, A_DV), F32)
    y_p, hg_p, cmp_p, sel_p, win_p = _run_group(x_prompt, hgrn0, True, None, *weights)
    y_s, hg_s, cmp_s, sel_s, win_s = _run_group(
        x_sample, state_hgrn[0], False, (cache_kv_cmp, cache_kv_sel, state_kv_win, page_table), *weights)
    return (y_p, y_s, hg_p, cmp_p, sel_p, win_p, hg_s, cmp_s, sel_s, win_s)
```

```python
import functools

import jax
import jax.numpy as jnp
from jax import lax
from jax.experimental import pallas as pl
from jax.experimental.pallas import tpu as pltpu

F32 = jnp.float32
BF16 = jnp.bfloat16

D_MODEL = 1024
DEPTH = 2
ALPHA = (2.0 * DEPTH) ** 0.25
LN_EPS = 1e-5
RMS_EPS = 1e-6
NEG_BIG = -1e30
FORCE = 1e6
PAGE_SIZE = 128

A_DK = 128
A_HEADS = D_MODEL // A_DK
A_DV = D_MODEL // A_HEADS
A_WIDTH = A_HEADS * A_DK
A_CHUNK = 64

B_HEADS = 16
B_KV = 4
B_GROUP = B_HEADS // B_KV
HEAD_DIM = D_MODEL // B_HEADS
SCALE = HEAD_DIM ** -0.5
CMP_STRIDE = 16
CMP_BLOCK = 2 * CMP_STRIDE
SEL_BLOCK = 64
TOP_N = 8
N_LOCAL = 2
WINDOW = 512

D_FF = 256 * ((8 * D_MODEL // 3 + 255) // 256)
N_EXPERTS = 8
MOE_TOP_K = 2
D_FF_E = 7 * D_MODEL // 2

VMEM_LIMIT_BYTES = 56 * 1024 * 1024


def _params(*sem):
    return pltpu.CompilerParams(dimension_semantics=sem, vmem_limit_bytes=VMEM_LIMIT_BYTES)


def _silu(x):
    return x * (1.0 / (1.0 + jnp.exp(-x)))


def _sigmoid(x):
    return 1.0 / (1.0 + jnp.exp(-x))


def _layer_norm(x, g, b):
    xc = x - jnp.mean(x, -1, keepdims=True)
    var = jnp.mean(xc * xc, -1, keepdims=True)
    return xc * lax.rsqrt(var + LN_EPS) * g + b


def _dot(a, b):
    return jnp.dot(a.astype(BF16), b.astype(BF16), preferred_element_type=F32)


def _dot_nt(a, b):
    return lax.dot_general(a.astype(BF16), b.astype(BF16), (((1,), (1,)), ((), ())),
                           preferred_element_type=F32)


def _dot_01(a01, b):
    hi = b.astype(BF16)
    lo = (b - hi.astype(F32)).astype(BF16)
    a01 = a01.astype(BF16)
    return jnp.dot(a01, hi, preferred_element_type=F32) + jnp.dot(a01, lo, preferred_element_type=F32)


def _dot_tn(a, b):
    return lax.dot_general(a.astype(BF16), b.astype(BF16), (((0,), (0,)), ((), ())),
                           preferred_element_type=F32)


def _hgrn_kernel(x_ref, s0_ref, win_ref, lbl_ref, ng_ref, wout_ref, lng_ref, lnb_ref,
                 h_ref, sout_ref, st_ref, *, layer, seg, n_seg, carry):
    rows = seg * n_seg
    x = x_ref[...].reshape(rows, D_MODEL)
    z = jnp.dot(x.astype(BF16), win_ref[...], preferred_element_type=F32)
    zq = z[:, 0 * A_WIDTH:1 * A_WIDTH]
    zf = z[:, 1 * A_WIDTH:2 * A_WIDTH]
    v = z[:, 2 * A_WIDTH:3 * A_WIDTH]
    zg = z[:, 3 * A_WIDTH:4 * A_WIDTH]

    lbl = lbl_ref[...]
    e = jnp.exp(lbl - jnp.max(lbl, axis=0, keepdims=True))
    lb = jnp.sum(e[:layer + 1], axis=0, keepdims=True) / jnp.sum(e, axis=0, keepdims=True)

    q = _silu(zq)
    f = lb + (1.0 - lb) * _sigmoid(zf)
    logf = jnp.log(f)
    k = 1.0 - f

    r_i = lax.broadcasted_iota(jnp.int32, (seg, seg), 0)
    c_i = lax.broadcasted_iota(jnp.int32, (seg, seg), 1)
    causal = c_i <= r_i
    tri = jnp.where(causal, 1.0, 0.0).astype(BF16)

    if carry:
        @pl.when(pl.program_id(1) == 0)
        def _():
            for h in range(A_HEADS):
                st_ref[h] = s0_ref[0, h].T

    o_parts = []
    for s in range(n_seg):
        sl = slice(s * seg, (s + 1) * seg)
        g = _dot_01(tri, logf[sl])
        glast = g[seg - 1:seg, :]
        qg = q[sl] * jnp.exp(g)
        kg = k[sl] * jnp.exp(-g)
        kd = k[sl] * jnp.exp(glast - g)
        eg = jnp.exp(glast)
        vs = v[sl]
        heads = []
        for h in range(A_HEADS):
            cl = slice(h * A_DK, (h + 1) * A_DK)
            if carry:
                st = st_ref[h]
            else:
                st = s0_ref[s, h].T
            att = jnp.where(causal, _dot_nt(qg[:, cl], kg[:, cl]), 0.0)
            o = _dot_nt(qg[:, cl], st) + _dot(att, vs[:, cl])
            st_new = eg[:, cl] * st + _dot_tn(vs[:, cl], kd[:, cl])
            if carry:
                st_ref[h] = st_new
            else:
                sout_ref[s, h] = st_new.T
            o = o * lax.rsqrt(jnp.mean(o * o, -1, keepdims=True) + RMS_EPS)
            heads.append(o)
        o_parts.append(jnp.concatenate(heads, axis=1))
    o = o_parts[0] if n_seg == 1 else jnp.concatenate(o_parts, axis=0)
    o = o * ng_ref[...] * _silu(zg)
    y = jnp.dot(o.astype(BF16), wout_ref[...], preferred_element_type=F32)
    hh = _layer_norm(ALPHA * x + y, lng_ref[...], lnb_ref[...])
    h_ref[...] = hh.reshape(h_ref.shape)

    if carry:
        @pl.when(pl.program_id(1) == pl.num_programs(1) - 1)
        def _():
            for h in range(A_HEADS):
                sout_ref[0, h] = st_ref[h].T


def hgrn_layer(x, s0, w_in, lb_logits, norm_g, w_out, ln_g, ln_b, *, layer, seg, n_seg, carry):
    bsz, t, _ = x.shape
    row2 = lambda a: a.reshape(1, -1).astype(F32)
    w_in = w_in.astype(BF16)
    w_out = w_out.astype(BF16)
    const = lambda *_: (0, 0)
    if carry:
        tile = seg * n_seg
        grid = (bsz, t // tile)
        x_spec = pl.BlockSpec((1, tile, D_MODEL), lambda b, c: (b, c, 0))
        s_spec = pl.BlockSpec((1, A_HEADS, A_DK, A_DV), lambda b, c: (b, 0, 0, 0))
        sem = ("parallel", "arbitrary")
    else:
        assert t == seg
        grid = (bsz // n_seg, 1)
        x_spec = pl.BlockSpec((n_seg, seg, D_MODEL), lambda b, c: (b, 0, 0))
        s_spec = pl.BlockSpec((n_seg, A_HEADS, A_DK, A_DV), lambda b, c: (b, 0, 0, 0))
        sem = ("parallel", "arbitrary")
    kern = functools.partial(_hgrn_kernel, layer=layer, seg=seg, n_seg=n_seg, carry=carry)
    return pl.pallas_call(
        kern,
        grid=grid,
        in_specs=[
            x_spec, s_spec,
            pl.BlockSpec((D_MODEL, 4 * A_WIDTH), const),
            pl.BlockSpec(lb_logits.shape, const),
            pl.BlockSpec((1, A_WIDTH), const),
            pl.BlockSpec((A_WIDTH, D_MODEL), const),
            pl.BlockSpec((1, D_MODEL), const),
            pl.BlockSpec((1, D_MODEL), const),
        ],
        out_specs=[x_spec, s_spec],
        out_shape=[jax.ShapeDtypeStruct(x.shape, F32), jax.ShapeDtypeStruct(s0.shape, F32)],
        scratch_shapes=[pltpu.VMEM((A_HEADS, A_DV, A_DK), F32)],
        compiler_params=_params(*sem),
        name="hgrn_layer",
    )(x, s0, w_in, lb_logits.astype(F32), row2(norm_g), w_out, row2(ln_g), row2(ln_b))


def _ffn_kernel(x_ref, wa_ref, wu_ref, wo_ref, lng_ref, lnb_ref, o_ref, acc_ref, xb_ref, *, rb):
    j = pl.program_id(1)

    @pl.when(j == 0)
    def _():
        xb_ref[...] = x_ref[...].astype(BF16)

    def row_block(b, _):
        rows = pl.ds(pl.multiple_of(b * rb, rb), rb)
        xb = xb_ref[rows, :]
        a = jnp.dot(xb, wa_ref[...], preferred_element_type=F32)
        u = jnp.dot(xb, wu_ref[...], preferred_element_type=F32)
        part = jnp.dot((_silu(a) * u).astype(BF16), wo_ref[...], preferred_element_type=F32)

        @pl.when(j == 0)
        def _():
            acc_ref[rows, :] = part

        @pl.when(j > 0)
        def _():
            acc_ref[rows, :] = acc_ref[rows, :] + part
        return 0

    lax.fori_loop(0, x_ref.shape[0] // rb, row_block, 0)

    @pl.when(j == pl.num_programs(1) - 1)
    def _():
        o_ref[...] = _layer_norm(ALPHA * x_ref[...] + acc_ref[...], lng_ref[...], lnb_ref[...])


def ffn_layer(x, w_in, w_out, ln_g, ln_b, *, tm, tf):
    n, d = x.shape
    d_ff = w_out.shape[0]
    nf = d_ff // tf
    rb = min(tm, 256)
    assert n % tm == 0 and d_ff % tf == 0 and tm % rb == 0
    w_in = w_in.astype(BF16)
    w_out = w_out.astype(BF16)
    row2 = lambda a: a.reshape(1, -1).astype(F32)
    return pl.pallas_call(
        functools.partial(_ffn_kernel, rb=rb),
        grid=(n // tm, nf),
        in_specs=[
            pl.BlockSpec((tm, d), lambda i, j: (i, 0)),
            pl.BlockSpec((d, tf), lambda i, j: (0, j)),
            pl.BlockSpec((d, tf), lambda i, j: (0, j + nf)),
            pl.BlockSpec((tf, d), lambda i, j: (j, 0)),
            pl.BlockSpec((1, d), lambda i, j: (0, 0)),
            pl.BlockSpec((1, d), lambda i, j: (0, 0)),
        ],
        out_specs=pl.BlockSpec((tm, d), lambda i, j: (i, 0)),
        out_shape=jax.ShapeDtypeStruct((n, d), F32),
        scratch_shapes=[pltpu.VMEM((tm, d), F32), pltpu.VMEM((tm, d), BF16)],
        compiler_params=_params("parallel", "arbitrary"),
        name="ffn_layer",
    )(x, w_in, w_in, w_out, row2(ln_g), row2(ln_b))


MOE_RB = 256
MOE_TAIL = 128
MOE_ST = 512


def _router_gates_t(x, wrt, br):
    logits = lax.dot_general(wrt, x, (((1,), (1,)), ((), ())), preferred_element_type=F32,
                             precision=lax.Precision.HIGHEST) + br
    eidx = lax.broadcasted_iota(jnp.int32, logits.shape, 0)
    m1 = jnp.max(logits, axis=0, keepdims=True)
    i1 = jnp.min(jnp.where(logits == m1, eidx, N_EXPERTS), axis=0, keepdims=True)
    rest = jnp.where(eidx == i1, -jnp.inf, logits)
    m2 = jnp.max(rest, axis=0, keepdims=True)
    i2 = jnp.min(jnp.where(rest == m2, eidx, N_EXPERTS), axis=0, keepdims=True)
    e2 = jnp.exp(m2 - m1)
    den = 1.0 + e2
    return jnp.where(eidx == i1, 1.0 / den, 0.0) + jnp.where(eidx == i2, e2 / den, 0.0)


def _moe_routed_kernel(cum_ref, xb_ref, gate_ref, rank_ref, wa_ref, wu_ref, wo_ref, y_ref, xs_ref, acc_ref,
                       *, n_sub, st):
    i = pl.program_id(0)
    e = pl.program_id(1)
    j = pl.program_id(2)
    nj = pl.num_programs(2)
    base = (i * N_EXPERTS + e) * (n_sub + 1)
    count = cum_ref[base + n_sub]
    n_full = (count + (MOE_RB - MOE_TAIL - 1)) // MOE_RB

    def for_blocks(fn):
        def body(b, _):
            fn(pl.multiple_of(b * MOE_RB, MOE_RB), MOE_RB)
            return 0

        lax.fori_loop(0, n_full, body, 0)

        @pl.when(count > n_full * MOE_RB)
        def _():
            fn(pl.multiple_of(n_full * MOE_RB, MOE_RB), MOE_TAIL)

    def for_subtiles(r0, rb, fn):
        def body(c, _):
            @pl.when((cum_ref[base + c] < r0 + rb) & (cum_ref[base + c + 1] > r0))
            def _():
                fn(c)
            return 0

        lax.fori_loop(0, n_sub, body, 0)

    def one_hot(r0, rb, c):
        c0 = pl.multiple_of(c * st, st)
        rank = rank_ref[pl.ds(e, 1), pl.ds(c0, st)]
        gate = gate_ref[pl.ds(e, 1), pl.ds(c0, st)]
        hit = (rank == r0 + lax.broadcasted_iota(jnp.int32, (rb, st), 0)) & (gate > 0.0)
        return hit, gate, c0

    def gather_block(r0, rb):
        rows = pl.ds(r0, rb)
        xs_ref[rows, :] = jnp.zeros((rb, D_MODEL), BF16)

        def sub(c):
            hit, _, c0 = one_hot(r0, rb, c)
            part = jnp.dot(jnp.where(hit, 1.0, 0.0).astype(BF16), xb_ref[pl.ds(c0, st), :],
                           preferred_element_type=F32)
            xs_ref[rows, :] = xs_ref[rows, :] + part.astype(BF16)

        for_subtiles(r0, rb, sub)

    def ffn_block(r0, rb):
        rows = pl.ds(r0, rb)
        xs = xs_ref[rows, :]
        a = jnp.dot(xs, wa_ref[0], preferred_element_type=F32)
        u = jnp.dot(xs, wu_ref[0], preferred_element_type=F32)
        part = jnp.dot((_silu(a) * u).astype(BF16), wo_ref[0], preferred_element_type=F32)

        @pl.when(j == 0)
        def _():
            acc_ref[rows, :] = part

        @pl.when(j > 0)
        def _():
            acc_ref[rows, :] = acc_ref[rows, :] + part

    def scatter_block(r0, rb):
        rows = pl.ds(r0, rb)

        def sub(c):
            hit, gate, c0 = one_hot(r0, rb, c)
            g_rows = jnp.sum(jnp.where(hit, gate, 0.0), axis=1, keepdims=True)
            out = (acc_ref[rows, :] * g_rows).astype(BF16)
            y_ref[pl.ds(c0, st), :] += _dot_tn(jnp.where(hit, 1.0, 0.0).astype(BF16), out)

        for_subtiles(r0, rb, sub)

    @pl.when((e == 0) & (j == 0))
    def _():
        y_ref[...] = jnp.zeros_like(y_ref)

    @pl.when(j == 0)
    def _():
        for_blocks(gather_block)

    for_blocks(ffn_block)

    @pl.when(j == nj - 1)
    def _():
        for_blocks(scatter_block)


def moe_routed(xb, gate_t, w_in, w_out, *, tm, tf):
    n, d = xb.shape
    ne, d_ff = w_out.shape[0], w_out.shape[1]
    nf = d_ff // tf
    st = min(MOE_ST, tm)
    n_tiles, n_sub = n // tm, tm // st
    assert n % tm == 0 and tm % st == 0 and d_ff % tf == 0 and tm % MOE_RB == 0
    mask = (gate_t > 0.0).astype(jnp.int32).reshape(ne, n_tiles, tm)
    rank = (jnp.cumsum(mask, axis=-1) - mask).reshape(ne, n)
    cnt = mask.reshape(ne, n_tiles, n_sub, st).sum(-1)
    cum = jnp.concatenate([jnp.zeros((ne, n_tiles, 1), jnp.int32), jnp.cumsum(cnt, axis=-1)], axis=-1)
    cum = cum.transpose(1, 0, 2).reshape(-1).astype(jnp.int32)
    w_in = w_in.astype(BF16)
    w_out = w_out.astype(BF16)
    kern = functools.partial(_moe_routed_kernel, n_sub=n_sub, st=st)
    return pl.pallas_call(
        kern,
        grid_spec=pltpu.PrefetchScalarGridSpec(
            num_scalar_prefetch=1,
            grid=(n_tiles, ne, nf),
            in_specs=[
                pl.BlockSpec((tm, d), lambda i, e, j, c: (i, 0)),
                pl.BlockSpec((ne, tm), lambda i, e, j, c: (0, i)),
                pl.BlockSpec((ne, tm), lambda i, e, j, c: (0, i)),
                pl.BlockSpec((1, d, tf), lambda i, e, j, c: (e, 0, j)),
                pl.BlockSpec((1, d, tf), lambda i, e, j, c: (e, 0, j + nf)),
                pl.BlockSpec((1, tf, d), lambda i, e, j, c: (e, j, 0)),
            ],
            out_specs=pl.BlockSpec((tm, d), lambda i, e, j, c: (i, 0)),
            scratch_shapes=[pltpu.VMEM((tm, d), BF16), pltpu.VMEM((tm, d), F32)]),
        out_shape=jax.ShapeDtypeStruct((n, d), F32),
        compiler_params=_params("parallel", "arbitrary", "arbitrary"),
        name="moe_routed",
    )(cum, xb, gate_t, rank, w_in, w_in, w_out)


def _add_ln_kernel(x_ref, y_ref, lng_ref, lnb_ref, o_ref):
    o_ref[...] = _layer_norm(ALPHA * x_ref[...] + y_ref[...], lng_ref[...], lnb_ref[...])


def add_ln(x, y, ln_g, ln_b, *, tm):
    n, d = x.shape
    row2 = lambda a: a.reshape(1, -1).astype(F32)
    blk = pl.BlockSpec((tm, d), lambda i: (i, 0))
    c2 = pl.BlockSpec((1, d), lambda i: (0, 0))
    return pl.pallas_call(
        _add_ln_kernel, grid=(n // tm,), in_specs=[blk, blk, c2, c2], out_specs=blk,
        out_shape=jax.ShapeDtypeStruct((n, d), F32), compiler_params=_params("parallel"), name="add_ln",
    )(x, y, row2(ln_g), row2(ln_b))


def moe_layer(x, xb, gate_t, w_in, w_out, ln_g, ln_b, *, tm, tf):
    y = moe_routed(xb, gate_t, w_in, w_out, tm=tm, tf=tf)
    return add_ln(x, y, ln_g, ln_b, tm=_tile(x.shape[0], 1024))


def _proj_kernel(x_ref, w_ref, o_ref):
    o_ref[...] = jnp.dot(x_ref[...].astype(BF16), w_ref[...], preferred_element_type=F32)


def proj(x, w, *, tm):
    n, d = x.shape
    m = w.shape[1]
    return pl.pallas_call(
        _proj_kernel,
        grid=(n // tm,),
        in_specs=[pl.BlockSpec((tm, d), lambda i: (i, 0)), pl.BlockSpec((d, m), lambda i: (0, 0))],
        out_specs=pl.BlockSpec((tm, m), lambda i: (i, 0)),
        out_shape=jax.ShapeDtypeStruct((n, m), F32),
        compiler_params=_params("parallel"),
        name="proj",
    )(x, w.astype(BF16))


def _proj_ln_router_kernel(o_ref, x_ref, w_ref, lng_ref, lnb_ref, wrt_ref, br_ref, h_ref, hb_ref, gate_ref):
    y = jnp.dot(o_ref[...].astype(BF16), w_ref[...], preferred_element_type=F32)
    h = _layer_norm(ALPHA * x_ref[...] + y, lng_ref[...], lnb_ref[...])
    h_ref[...] = h
    hb_ref[...] = h.astype(BF16)
    gate_ref[...] = _router_gates_t(h, wrt_ref[...], br_ref[...])


def proj_ln_router(o, x, w, ln_g, ln_b, w_router, b_router, *, tm):
    n, d = x.shape
    row2 = lambda a: a.reshape(1, -1).astype(F32)
    c2 = lambda i: (0, 0)
    blk = pl.BlockSpec((tm, d), lambda i: (i, 0))
    return pl.pallas_call(
        _proj_ln_router_kernel,
        grid=(n // tm,),
        in_specs=[pl.BlockSpec((tm, o.shape[1]), lambda i: (i, 0)), blk,
                  pl.BlockSpec(w.shape, c2), pl.BlockSpec((1, d), c2), pl.BlockSpec((1, d), c2),
                  pl.BlockSpec((N_EXPERTS, d), c2), pl.BlockSpec((N_EXPERTS, 1), c2)],
        out_specs=[blk, blk, pl.BlockSpec((N_EXPERTS, tm), lambda i: (0, i))],
        out_shape=[jax.ShapeDtypeStruct((n, d), F32), jax.ShapeDtypeStruct((n, d), BF16),
                   jax.ShapeDtypeStruct((N_EXPERTS, n), F32)],
        compiler_params=_params("parallel"),
        name="proj_ln_router",
    )(o, x, w.astype(BF16), row2(ln_g), row2(ln_b), w_router.T.astype(F32),
      b_router.reshape(N_EXPERTS, 1).astype(F32))


N_KV_COLS = 6 * B_KV * HEAD_DIM
KV_SET = 2 * B_KV * HEAD_DIM
N_Q_COLS = B_HEADS * HEAD_DIM
GATE_LANES = 128
LOG2_E = 1.4426950408889634
V_AUG = 2 * HEAD_DIM
ACC_ROWS = HEAD_DIM + 8


def _nsa_proj_kernel(x_ref, w_ref, cmp_ref, sel_ref, win_ref, q_ref, ks_ref, vs_ref, kw_ref, vw_ref, g_ref):
    z = jnp.dot(x_ref[0].astype(BF16), w_ref[...], preferred_element_type=F32)
    cmp_ref[0] = z[:, 0:KV_SET]
    sel_ref[0] = z[:, KV_SET:2 * KV_SET]
    win_ref[0] = z[:, 2 * KV_SET:3 * KV_SET]
    half = B_KV * HEAD_DIM
    lane = lax.broadcasted_iota(jnp.int32, (z.shape[0], V_AUG), 1)

    def v_aug(lo):
        return jnp.where(lane < HEAD_DIM, z[:, lo:lo + V_AUG], jnp.where(lane == HEAD_DIM, 1.0, 0.0)).astype(BF16)

    for g in range(B_KV):
        lo = KV_SET + g * HEAD_DIM
        ks_ref[0, g] = z[:, lo:lo + HEAD_DIM].astype(BF16)
        vs_ref[0, g] = v_aug(lo + half)
        lo = 2 * KV_SET + g * HEAD_DIM
        kw_ref[0, g] = z[:, lo:lo + HEAD_DIM].astype(BF16)
        vw_ref[0, g] = v_aug(lo + half)
    for h in range(B_HEADS):
        lo = N_KV_COLS + h * HEAD_DIM
        q_ref[0, h] = (z[:, lo:lo + HEAD_DIM] * (SCALE * LOG2_E)).astype(BF16)
    g_ref[0] = _sigmoid(z[:, N_KV_COLS + N_Q_COLS:N_KV_COLS + N_Q_COLS + GATE_LANES])


def nsa_proj(h, kv_w, w_qg, *, tm):
    bsz, t, d = h.shape
    w = jnp.concatenate([kv_w, w_qg], axis=1)
    w = jnp.pad(w, ((0, 0), (0, N_KV_COLS + N_Q_COLS + GATE_LANES - w.shape[1]))).astype(BF16)
    row = lambda: pl.BlockSpec((1, tm, KV_SET), lambda b, i: (b, i, 0))
    hm = lambda nh, w=HEAD_DIM: pl.BlockSpec((1, nh, tm, w), lambda b, i: (b, 0, i, 0))
    sds = jax.ShapeDtypeStruct
    kshape, vshape = sds((bsz, B_KV, t, HEAD_DIM), BF16), sds((bsz, B_KV, t, V_AUG), BF16)
    return pl.pallas_call(
        _nsa_proj_kernel,
        grid=(bsz, t // tm),
        in_specs=[pl.BlockSpec((1, tm, d), lambda b, i: (b, i, 0)), pl.BlockSpec(w.shape, lambda b, i: (0, 0))],
        out_specs=[row(), row(), row(), hm(B_HEADS), hm(B_KV), hm(B_KV, V_AUG), hm(B_KV), hm(B_KV, V_AUG),
                   pl.BlockSpec((1, tm, GATE_LANES), lambda b, i: (b, i, 0))],
        out_shape=[sds((bsz, t, KV_SET), F32)] * 3 + [sds((bsz, B_HEADS, t, HEAD_DIM), BF16)]
        + [kshape, vshape, kshape, vshape] + [sds((bsz, t, GATE_LANES), F32)],
        compiler_params=_params("parallel", "parallel"),
        name="nsa_proj",
    )(h, w)


PAIR = 2 * HEAD_DIM


def _compress_kernel(*refs, n_in, head_major, paged):
    if paged:
        refs = refs[1:]
    x_refs = refs[:n_in]
    wp_ref, u_ref, b1_ref, w2_ref, kc_ref, vc_ref, carry_ref = refs[n_in:]
    i = pl.program_id(1)

    @pl.when(i == 0)
    def _():
        carry_ref[...] = jnp.zeros_like(carry_ref)

    cpp = PAGE_SIZE // CMP_STRIDE
    pr = lax.broadcasted_iota(jnp.int32, (PAGE_SIZE, PAGE_SIZE), 0)
    pc = lax.broadcasted_iota(jnp.int32, (PAGE_SIZE, PAGE_SIZE), 1)
    perm = jnp.where(pc == CMP_STRIDE * (pr % cpp) + pr // cpp, 1.0, 0.0).astype(BF16)
    pages = []
    for r in x_refs:
        if paged:
            pages.append(_dot_nt(perm, r[0]))
        else:
            for p0 in range(0, r.shape[1], PAGE_SIZE):
                pages.append(jnp.dot(perm, r[0, p0:p0 + PAGE_SIZE, :].astype(BF16), preferred_element_type=F32))
    ch = len(pages) * cpp
    row0 = lax.broadcasted_iota(jnp.int32, (ch, PAIR), 0) == 0

    def chunk_rows(s, lo):
        parts = [pg[s * cpp:(s + 1) * cpp, lo:lo + PAIR] for pg in pages]
        return parts[0] if len(parts) == 1 else jnp.concatenate(parts, axis=0)
    for k in range(2):
        wp = wp_ref[k]
        pbm = jnp.dot(u_ref[k], wp, preferred_element_type=F32)
        pb = pbm[0:1, 0:PAIR] + pbm[1:2, PAIR:2 * PAIR] + b1_ref[k]
        for gp in range(B_KV // 2):
            base = k * B_KV * HEAD_DIM + gp * PAIR
            lhs = jnp.concatenate([chunk_rows(s, base) for s in range(CMP_STRIDE)], axis=1).astype(BF16)
            r = jnp.dot(lhs, wp, preferred_element_type=F32)
            first, second = r[:, 0:PAIR], r[:, PAIR:2 * PAIR]
            slot = k * (B_KV // 2) + gp
            prev = carry_ref[slot]
            shifted = jnp.where(row0, prev[7:8, :], pltpu.roll(first, 1, axis=0))
            carry_ref[slot] = first[ch - 8:ch, :]
            hid = jax.nn.gelu(shifted + second + pb)
            out = jnp.dot(hid.astype(BF16), w2_ref[k], preferred_element_type=F32).astype(BF16)
            dst = kc_ref if k == 0 else vc_ref
            if head_major:
                for g2 in range(2):
                    dst[0, 2 * gp + g2] = out[:, g2 * HEAD_DIM:(g2 + 1) * HEAD_DIM]
            else:
                dst[0, :, gp * PAIR:(gp + 1) * PAIR] = out


def _compress_weights(cmp_pe, cmp_w1, cmp_b1, cmp_w2):
    eye2 = jnp.eye(2, dtype=F32)
    w1 = cmp_w1.reshape(2, CMP_STRIDE, 2, HEAD_DIM, HEAD_DIM)
    wp = jnp.einsum('fskdh,ab->ksadfbh', w1, eye2).reshape(2, CMP_STRIDE * PAIR, 2 * PAIR)
    pe = cmp_pe.reshape(2, CMP_STRIDE, 2, HEAD_DIM)
    u = jnp.broadcast_to(pe.transpose(2, 0, 1, 3)[:, :, :, None, :], (2, 2, CMP_STRIDE, 2, HEAD_DIM))
    u = jnp.pad(u.reshape(2, 2, CMP_STRIDE * PAIR), ((0, 0), (0, 6), (0, 0)))
    b1 = jnp.tile(cmp_b1, (1, 2)).reshape(2, 1, PAIR)
    w2 = jnp.einsum('khd,ab->kahbd', cmp_w2, eye2).reshape(2, PAIR, PAIR)
    return wp.astype(BF16), u.astype(BF16), b1.astype(F32), w2.astype(BF16)


def compress(x, page_table, cmp_pe, cmp_w1, cmp_b1, cmp_w2, *, ch, head_major):
    wp, u, b1, w2 = _compress_weights(cmp_pe, cmp_w1, cmp_b1, cmp_w2)
    paged = page_table is not None
    if paged:
        bsz, n_pages = page_table.shape
        n_in = ch * CMP_STRIDE // PAGE_SIZE
        n_chunks = n_pages * PAGE_SIZE // CMP_STRIDE
        x_specs = [pl.BlockSpec((1, KV_SET, PAGE_SIZE), functools.partial(
            lambda b, i, pt, j: (pt[b, i * n_in + j], 0, 0), j=j)) for j in range(n_in)]
        cm = lambda f: (lambda b, i, pt: f(b, i))
    else:
        bsz, t, _ = x.shape
        n_chunks = t // CMP_STRIDE
        n_in = 1
        x_specs = [pl.BlockSpec((1, ch * CMP_STRIDE, KV_SET), lambda b, i: (b, i, 0))]
        cm = lambda f: f
    c3 = cm(lambda b, i: (0, 0, 0))
    if head_major:
        o_spec = pl.BlockSpec((1, B_KV, ch, HEAD_DIM), cm(lambda b, i: (b, 0, i, 0)))
        o_shape = jax.ShapeDtypeStruct((bsz, B_KV, n_chunks, HEAD_DIM), BF16)
    else:
        o_spec = pl.BlockSpec((1, ch, B_KV * HEAD_DIM), cm(lambda b, i: (b, i, 0)))
        o_shape = jax.ShapeDtypeStruct((bsz, n_chunks, B_KV * HEAD_DIM), BF16)
    in_specs = x_specs + [pl.BlockSpec(wp.shape, c3), pl.BlockSpec(u.shape, c3),
                          pl.BlockSpec(b1.shape, c3), pl.BlockSpec(w2.shape, c3)]
    grid = (bsz, n_chunks // ch)
    scratch = [pltpu.VMEM((2 * (B_KV // 2), 8, PAIR), F32)]
    kern = functools.partial(_compress_kernel, n_in=n_in, head_major=head_major, paged=paged)
    if paged:
        grid_spec = pltpu.PrefetchScalarGridSpec(num_scalar_prefetch=1, grid=grid, in_specs=in_specs,
                                                 out_specs=[o_spec, o_spec], scratch_shapes=scratch)
        args = (page_table,) + (x,) * n_in
    else:
        grid_spec = pl.GridSpec(grid=grid, in_specs=in_specs, out_specs=[o_spec, o_spec], scratch_shapes=scratch)
        args = (x,)
    return pl.pallas_call(
        kern, grid_spec=grid_spec, out_shape=[o_shape, o_shape],
        compiler_params=_params("parallel", "arbitrary"), name="compress",
    )(*args, wp, u, b1, w2)


def _select_blocks(imp, blk, cur, n_sel, axis):
    valid = blk <= cur
    forced = (blk == 0) | (valid & (blk > cur - N_LOCAL))
    score = jnp.where(forced, FORCE, jnp.where(valid, imp, -FORCE))
    score = jnp.where(blk < n_sel, score, -jnp.inf)
    out = jnp.full(score.shape, NEG_BIG, F32)
    big = jnp.int32(2 ** 30)
    for _ in range(min(TOP_N, n_sel)):
        m = jnp.max(score, axis=axis, keepdims=True)
        first = jnp.min(jnp.where(score == m, blk, big), axis=axis, keepdims=True)
        pick = blk == first
        out = jnp.where(pick, 0.0, out)
        score = jnp.where(pick, -jnp.inf, score)
    return jnp.where(valid, out, NEG_BIG)


def _overlap(c1, j, n_sel):
    c0 = (c1 - 1) * CMP_STRIDE
    j0 = j * SEL_BLOCK
    return ((c1 >= 1) & (j < n_sel) & (c0 <= j0 + SEL_BLOCK - 1) & (c0 + CMP_BLOCK - 1 >= j0)).astype(F32)


def _rank_select(imp, blk, cur, n_sel):
    valid = blk <= cur
    forced = (blk == 0) | (valid & (blk > cur - N_LOCAL))
    score = jnp.where(forced, FORCE, jnp.where(valid, imp, -FORCE))
    rank = jnp.zeros(score.shape, F32)
    for jp in range(n_sel):
        row = score[jp:jp + 1, :]
        gt = jnp.where(row > score, 1.0, 0.0)
        ge = jnp.where(row >= score, 1.0, 0.0)
        rank = rank + jnp.where(blk > jp, ge, gt)
    keep = jnp.where(rank < TOP_N, 0.0, NEG_BIG)
    return jnp.where(valid, jnp.where(blk < n_sel, keep, NEG_BIG), NEG_BIG)


def _online_softmax_t(s, m, acc, v):
    m_new = jnp.maximum(m, jnp.max(s, axis=0, keepdims=True))
    p = jnp.exp2((s - m_new).astype(BF16))
    acc = jnp.exp2(m - m_new) * acc + _dot_tn(v, p)[:ACC_ROWS]
    return m_new, acc


def _prompt_attn_kernel(q_ref, kc_ref, vc_ref, ks_ref, vs_ref, kw_ref, vw_ref, g_ref, o_ref, gt_ref,
                        *, tq, tk, n_sel, nb):
    g = pl.program_id(2)
    t0 = pl.program_id(1) * tq
    cols = B_GROUP * tq
    q = q_ref[0].reshape(cols, HEAD_DIM)
    tpos = t0 + lax.broadcasted_iota(jnp.int32, (1, tq), 1)
    nch = kc_ref.shape[2]
    rep = lambda a: jnp.concatenate([a] * B_GROUP, axis=1)

    c1 = lax.broadcasted_iota(jnp.int32, (nch, 1), 0)
    cmask = rep((c1 >= 1) & (c1 * CMP_STRIDE + (CMP_BLOCK - CMP_STRIDE - 1) <= tpos))
    s = jnp.where(cmask, _dot_nt(kc_ref[0, 0], q), NEG_BIG)
    m = jnp.max(s, axis=0, keepdims=True)
    e = jnp.where(cmask, jnp.exp2(s - m), 0.0)
    l = jnp.sum(e, axis=0, keepdims=True)
    p = e * (1.0 / jnp.where(l == 0.0, 1.0, l))
    o_cmp = _dot_tn(vc_ref[0, 0], p)

    p4 = p[:, 0:tq]
    for r in range(1, B_GROUP):
        p4 = p4 + p[:, r * tq:(r + 1) * tq]
    ov_t = _overlap(lax.broadcasted_iota(jnp.int32, (nb, nch), 1),
                    lax.broadcasted_iota(jnp.int32, (nb, nch), 0), n_sel)
    imp = _dot_01(ov_t, p4)
    blk = lax.broadcasted_iota(jnp.int32, (nb, tq), 0)
    bias = _rank_select(imp, blk, tpos // SEL_BLOCK, n_sel).astype(BF16)

    def sel_scores(kt):
        k0 = pl.multiple_of(kt * tk, tk)
        kpos = k0 + lax.broadcasted_iota(jnp.int32, (tk, 1), 0)
        blk_of_key = (k0 + lax.broadcasted_iota(jnp.int32, (tk, nb), 0)) // SEL_BLOCK
        onehot = jnp.where(blk_of_key == lax.broadcasted_iota(jnp.int32, (tk, nb), 1), 1.0, 0.0).astype(BF16)
        mb = jnp.dot(onehot, bias, preferred_element_type=F32)
        mb = jnp.where(kpos <= tpos, mb, NEG_BIG)
        return _dot_nt(ks_ref[0, 0, pl.ds(k0, tk), :], q) + rep(mb), vs_ref[0, 0, pl.ds(k0, tk), :]

    def win_scores(kt):
        k0 = pl.multiple_of(kt * tk, tk)
        kpos = k0 + lax.broadcasted_iota(jnp.int32, (tk, 1), 0)
        mb = jnp.where((kpos <= tpos) & (kpos >= tpos - WINDOW), 0.0, NEG_BIG)
        return _dot_nt(kw_ref[0, 0, pl.ds(k0, tk), :], q) + rep(mb), vw_ref[0, 0, pl.ds(k0, tk), :]

    def sel_only(kt, carry):
        s, v = sel_scores(kt)
        return _online_softmax_t(s, *carry, v)

    init = (jnp.full((1, cols), NEG_BIG, F32), jnp.zeros((ACC_ROWS, cols), F32))
    hi = (t0 + tq - 1) // tk + 1
    lo_w = jnp.maximum(t0 - WINDOW, 0) // tk

    def both(kt, carry):
        s, v = sel_scores(kt)
        sw, vw = win_scores(kt)
        return _online_softmax_t(s, *carry[:2], v) + _online_softmax_t(sw, *carry[2:], vw)

    sel_state = lax.fori_loop(0, lo_w, sel_only, init)
    _, acc_s, _, acc_w = lax.fori_loop(lo_w, hi, both, sel_state + init)

    gt_ref[...] = g_ref[0].T
    gate = lambda br: jnp.concatenate(
        [gt_ref[pl.ds(g * (3 * B_GROUP) + 3 * r + br, 1), :] for r in range(B_GROUP)], axis=1)
    norm = lambda acc: acc[:HEAD_DIM] * (1.0 / acc[HEAD_DIM:HEAD_DIM + 1])
    o = gate(0) * o_cmp + gate(1) * norm(acc_s) + gate(2) * norm(acc_w)
    for r in range(B_GROUP):
        o_ref[0, :, r * HEAD_DIM:(r + 1) * HEAD_DIM] = o[:, r * tq:(r + 1) * tq].T.astype(o_ref.dtype)


def nsa_prompt_attn(q, kc, vc, ks, vs, kw, vw, gates, *, tq, tk):
    bsz, _, t, _ = q.shape
    nch = kc.shape[2]
    n_sel = -(-t // SEL_BLOCK)
    nb = -(-n_sel // 16) * 16
    assert t % tq == 0 and t % tk == 0 and tq % 128 == 0
    seq = lambda n, w=HEAD_DIM: pl.BlockSpec((1, 1, n, w), lambda b, i, g: (b, g, 0, 0))
    kern = functools.partial(_prompt_attn_kernel, tq=tq, tk=tk, n_sel=n_sel, nb=nb)
    return pl.pallas_call(
        kern,
        grid=(bsz, t // tq, B_KV),
        in_specs=[pl.BlockSpec((1, B_GROUP, tq, HEAD_DIM), lambda b, i, g: (b, g, i, 0)),
                  seq(nch), seq(nch), seq(t), seq(t, V_AUG), seq(t), seq(t, V_AUG),
                  pl.BlockSpec((1, tq, GATE_LANES), lambda b, i, g: (b, i, 0))],
        out_specs=pl.BlockSpec((1, tq, B_GROUP * HEAD_DIM), lambda b, i, g: (b, i, g)),
        out_shape=jax.ShapeDtypeStruct((bsz, t, N_Q_COLS), BF16),
        scratch_shapes=[pltpu.VMEM((GATE_LANES, tq), F32)],
        compiler_params=_params("parallel", "parallel", "arbitrary"),
        name="nsa_prompt_attn",
    )(q, kc, vc, ks, vs, kw, vw, gates)


KV_HALF = B_KV * HEAD_DIM
Q_COLS = B_GROUP * B_KV * 8


def _sample_attn_kernel(*refs, n_pages_step, past, t_new, n_keep):
    pt_ref, qbd_ref, gl_ref, kc_ref, vc_ref, swin_ref, nsel_ref, nwin_ref = refs[:8]
    page_refs = refs[8:8 + n_pages_step]
    out_ref, bias_ref, m_ref, l_ref, acc_ref, oth_ref = refs[8 + n_pages_step:]
    del pt_ref
    i = pl.program_id(1)
    qbd = qbd_ref[0]
    ncol = qbd.shape[1]
    col = lax.broadcasted_iota(jnp.int32, (1, ncol), 1)
    tcol = col % t_new
    qpos = past + tcol
    sig = _sigmoid(gl_ref[0])
    n_sel = -(-(past + t_new) // SEL_BLOCK)
    blk_step = n_pages_step * PAGE_SIZE // SEL_BLOCK
    pad_rows = 8

    def scores(k):
        return jnp.dot(k.astype(BF16), qbd, preferred_element_type=F32) * SCALE

    @pl.when(i == 0)
    def _():
        nch = kc_ref.shape[1]
        c1 = lax.broadcasted_iota(jnp.int32, (nch, 1), 0)
        cmask = (c1 >= 1) & (c1 * CMP_STRIDE + (CMP_BLOCK - CMP_STRIDE - 1) <= qpos)
        s = jnp.where(cmask, scores(kc_ref[0]), NEG_BIG)
        m = jnp.max(s, axis=0, keepdims=True)
        e = jnp.where(cmask, jnp.exp(s - m), 0.0)
        l = jnp.sum(e, axis=0, keepdims=True)
        p = e / jnp.where(l == 0.0, 1.0, l)
        o_cmp = _dot_tn(vc_ref[0], p)
        nb = bias_ref.shape[0]
        ov_t = _overlap(lax.broadcasted_iota(jnp.int32, (nb, nch), 1),
                        lax.broadcasted_iota(jnp.int32, (nb, nch), 0), n_sel)
        imp = _dot_01(ov_t, p)
        per = ncol // B_GROUP
        imp = imp + pltpu.roll(imp, per, axis=1) + pltpu.roll(imp, 2 * per, axis=1) + pltpu.roll(imp, 3 * per, axis=1)
        blk = lax.broadcasted_iota(jnp.int32, (nb, ncol), 0)
        bias_ref[...] = _select_blocks(imp, blk, qpos // SEL_BLOCK, n_sel, axis=0)
        kv_w = jnp.concatenate([swin_ref[0], nwin_ref[0], jnp.zeros((pad_rows, 2 * KV_HALF), F32)], axis=0)
        nw = kv_w.shape[0]
        wi = lax.broadcasted_iota(jnp.int32, (nw, 1), 0)
        wpos = past - n_keep + wi
        wmask = (wi < n_keep + t_new) & (wpos <= qpos) & (wpos >= qpos - WINDOW) & (wpos >= 0)
        s = jnp.where(wmask, scores(kv_w[:, :KV_HALF]), NEG_BIG)
        m = jnp.max(s, axis=0, keepdims=True)
        e = jnp.where(wmask, jnp.exp(s - m), 0.0)
        l = jnp.sum(e, axis=0, keepdims=True)
        o_win = _dot_tn(kv_w[:, KV_HALF:], e / jnp.where(l == 0.0, 1.0, l))
        oth_ref[...] = sig[0:1] * o_cmp + sig[2:3] * o_win
        m_ref[...] = jnp.full(m_ref.shape, NEG_BIG, F32)
        l_ref[...] = jnp.zeros(l_ref.shape, F32)
        acc_ref[...] = jnp.zeros(acc_ref.shape, F32)

    def update(s, v=None, v_t=None):
        m_new = jnp.maximum(m_ref[...], jnp.max(s, axis=0, keepdims=True))
        alpha = jnp.exp(m_ref[...] - m_new)
        p = jnp.exp(s - m_new)
        l_ref[...] = alpha * l_ref[...] + jnp.sum(p, axis=0, keepdims=True)
        pv = _dot_tn(v, p) if v_t is None else _dot(v_t, p)
        acc_ref[...] = alpha * acc_ref[...] + pv
        m_ref[...] = m_new

    bias = bias_ref[pl.ds(pl.multiple_of(i * blk_step, blk_step), blk_step), :]
    per_page = PAGE_SIZE // SEL_BLOCK
    s_parts, v_parts = [], []
    for p in range(n_pages_step):
        page = page_refs[p][0]
        mb = jnp.concatenate([jnp.broadcast_to(bias[per_page * p + j:per_page * p + j + 1, :], (SEL_BLOCK, ncol))
                              for j in range(per_page)], axis=0)
        s_parts.append(_dot_tn(page[:KV_HALF], qbd) * SCALE + mb)
        v_parts.append(page[KV_HALF:].astype(BF16))
    update(jnp.concatenate(s_parts, axis=0), v_t=jnp.concatenate(v_parts, axis=1))

    @pl.when(i == pl.num_programs(1) - 1)
    def _():
        kv_n = jnp.concatenate([nsel_ref[0], jnp.zeros((pad_rows, 2 * KV_HALF), F32)], axis=0)
        u = lax.broadcasted_iota(jnp.int32, (kv_n.shape[0], 1), 0)
        nb_new = past // SEL_BLOCK
        s = scores(kv_n[:, :KV_HALF]) + bias_ref[nb_new:nb_new + 1, :]
        update(jnp.where((u < t_new) & (past + u <= qpos), s, NEG_BIG), kv_n[:, KV_HALF:])
        o = oth_ref[...] + sig[1:2] * acc_ref[...] / l_ref[...]
        g_row = lax.broadcasted_iota(jnp.int32, o.shape, 0) // HEAD_DIM
        g_col = (lax.broadcasted_iota(jnp.int32, o.shape, 1) // t_new) % B_KV
        o = jnp.where(g_row == g_col, o, 0.0)
        out_ref[0] = o[0:HEAD_DIM] + o[HEAD_DIM:2 * HEAD_DIM] + o[2 * HEAD_DIM:3 * HEAD_DIM] + o[3 * HEAD_DIM:]


def nsa_sample_attn(q, gate_logits, kc, vc, cache_kv_sel, state_kv_win, kv_sel_new, kv_win_new, page_table,
                    *, n_pages_step):
    bsz, t, _ = q.shape
    n_pages = page_table.shape[1]
    past = n_pages * PAGE_SIZE
    n_keep = state_kv_win.shape[1]
    assert t * B_GROUP * B_KV == Q_COLS and past % SEL_BLOCK == 0 and n_pages % n_pages_step == 0
    qt = q.reshape(bsz, t, B_KV, B_GROUP, HEAD_DIM).transpose(0, 2, 4, 3, 1)
    qbd = jnp.einsum('bgdrt,gh->bgdrht', qt, jnp.eye(B_KV, dtype=F32)).reshape(bsz, KV_HALF, Q_COLS).astype(BF16)
    gl = gate_logits.reshape(bsz, t, B_KV, B_GROUP, 3).transpose(0, 4, 3, 2, 1).reshape(bsz, 3, Q_COLS)
    gl = jnp.pad(gl, ((0, 0), (0, 5), (0, 0)))
    n_blk = past // SEL_BLOCK + 8
    per_b = lambda shape: pl.BlockSpec((1,) + shape, lambda b, i, pt: (b, 0, 0))
    page_specs = [pl.BlockSpec((1, 2 * KV_HALF, PAGE_SIZE), functools.partial(
        lambda b, i, pt, j: (pt[b, i * n_pages_step + j], 0, 0), j=j)) for j in range(n_pages_step)]
    kern = functools.partial(_sample_attn_kernel, n_pages_step=n_pages_step, past=past, t_new=t, n_keep=n_keep)
    out = pl.pallas_call(
        kern,
        grid_spec=pltpu.PrefetchScalarGridSpec(
            num_scalar_prefetch=1,
            grid=(bsz, n_pages // n_pages_step),
            in_specs=[per_b((KV_HALF, Q_COLS)), per_b((8, Q_COLS)), per_b(kc.shape[1:]), per_b(vc.shape[1:]),
                      per_b((n_keep, 2 * KV_HALF)), per_b((t, 2 * KV_HALF)), per_b((t, 2 * KV_HALF))] + page_specs,
            out_specs=pl.BlockSpec((1, HEAD_DIM, Q_COLS), lambda b, i, pt: (b, 0, 0)),
            scratch_shapes=[pltpu.VMEM((n_blk, Q_COLS), F32), pltpu.VMEM((1, Q_COLS), F32),
                            pltpu.VMEM((1, Q_COLS), F32), pltpu.VMEM((KV_HALF, Q_COLS), F32),
                            pltpu.VMEM((KV_HALF, Q_COLS), F32)]),
        out_shape=jax.ShapeDtypeStruct((bsz, HEAD_DIM, Q_COLS), F32),
        compiler_params=_params("parallel", "arbitrary"),
        name="nsa_sample_attn",
    )(page_table, qbd, gl, kc, vc, state_kv_win, kv_sel_new, kv_win_new, *([cache_kv_sel] * n_pages_step))
    return out.reshape(bsz, HEAD_DIM, B_GROUP, B_KV, t).transpose(0, 4, 3, 2, 1).reshape(bsz, t, N_Q_COLS)


def _tile(n, pref):
    return pref if n % pref == 0 else n


def _run_group(x, s0, is_prompt, caches, a_w_in, a_lb_logits, a_norm_g, a_w_out, b_w_qg, b_w_out, kv_w,
               cmp_pe, cmp_w1, cmp_b1, cmp_w2, ffn_w_in, ffn_w_out, moe_w_router, moe_b_router,
               moe_w_in, moe_w_out, ln_g, ln_b):
    bsz, t, d = x.shape
    n = bsz * t
    if is_prompt:
        h, s_out = hgrn_layer(x, s0, a_w_in[0], a_lb_logits, a_norm_g[0], a_w_out[0], ln_g[0, 0], ln_b[0, 0],
                              layer=0, seg=A_CHUNK, n_seg=8, carry=True)
    else:
        h, s_out = hgrn_layer(x, s0, a_w_in[0], a_lb_logits, a_norm_g[0], a_w_out[0], ln_g[0, 0], ln_b[0, 0],
                              layer=0, seg=t, n_seg=8, carry=False)
    h = h.reshape(n, d)
    h = ffn_layer(h, ffn_w_in[0], ffn_w_out[0], ln_g[0, 1], ln_b[0, 1], tm=_tile(n, 1024), tf=D_FF // 2)
    nq = B_HEADS * HEAD_DIM
    if is_prompt:
        kv_cmp, kv_sel, kv_win, qh, ksh, vsh, kwh, vwh, gts = nsa_proj(h.reshape(bsz, t, d), kv_w, b_w_qg[0], tm=512)
        n_ch = t // CMP_STRIDE
        kc, vc = compress(kv_cmp, None, cmp_pe, cmp_w1, cmp_b1, cmp_w2,
                          ch=n_ch, head_major=True)
        o = nsa_prompt_attn(qh, kc, vc, ksh, vsh, kwh, vwh, gts, tq=256, tk=256)
        h, hb, gate_t = proj_ln_router(o.reshape(n, nq), h, b_w_out[0], ln_g[1, 0], ln_b[1, 0],
                                       moe_w_router[0], moe_b_router[0], tm=_tile(n, 512))
        h = moe_layer(h, hb, gate_t, moe_w_in[0], moe_w_out[0], ln_g[1, 1], ln_b[1, 1], tm=_tile(n, 2048), tf=896)
        kvshape = (bsz, t, 2, B_KV, HEAD_DIM)
        return (h.reshape(bsz, t, d), s_out[None], kv_cmp.reshape(kvshape), kv_sel.reshape(kvshape),
                kv_win.reshape(kvshape)[:, t - min(WINDOW, t):])
    cache_kv_cmp, cache_kv_sel, state_kv_win, page_table = caches
    n_pool = cache_kv_cmp.shape[0]
    n_keep = state_kv_win.shape[1]
    w_cat = jnp.concatenate([kv_w, b_w_qg[0]], axis=1)
    w_cat = jnp.pad(w_cat, ((0, 0), (0, (-w_cat.shape[1]) % 128)))
    z = proj(h, w_cat, tm=_tile(n, 512))
    kv = z[:, :N_KV_COLS].reshape(bsz, t, 3, KV_SET)
    q = z[:, N_KV_COLS:N_KV_COLS + nq].reshape(bsz, t, nq)
    gate_logits = z[:, N_KV_COLS + nq:N_KV_COLS + nq + 3 * B_HEADS].reshape(bsz, t, 3 * B_HEADS)
    assert (page_table.shape[1] * PAGE_SIZE + t) // CMP_STRIDE == page_table.shape[1] * PAGE_SIZE // CMP_STRIDE
    token_minor = lambda c: c.transpose(0, 2, 3, 4, 1).reshape(n_pool, KV_SET, PAGE_SIZE)
    kc, vc = compress(token_minor(cache_kv_cmp), page_table,
                      cmp_pe, cmp_w1, cmp_b1, cmp_w2, ch=256, head_major=False)
    o = nsa_sample_attn(q, gate_logits, kc, vc, token_minor(cache_kv_sel),
                        state_kv_win.reshape(bsz, n_keep, KV_SET), kv[:, :, 1], kv[:, :, 2], page_table,
                        n_pages_step=32)
    h, hb, gate_t = proj_ln_router(o.reshape(n, nq), h, b_w_out[0], ln_g[1, 0], ln_b[1, 0],
                                   moe_w_router[0], moe_b_router[0], tm=_tile(n, 512))
    h = moe_layer(h, hb, gate_t, moe_w_in[0], moe_w_out[0], ln_g[1, 1], ln_b[1, 1], tm=_tile(n, 2048), tf=896)
    kvshape = (bsz, t, 2, B_KV, HEAD_DIM)
    win_all = jnp.concatenate([state_kv_win, kv[:, :, 2].reshape(kvshape)], axis=1)
    return (h.reshape(bsz, t, d), s_out[None], kv[:, :, 0].reshape(kvshape), kv[:, :, 1].reshape(kvshape),
            win_all[:, -n_keep:])


def kernel(x_prompt, x_sample, state_hgrn, cache_kv_cmp, cache_kv_sel, state_kv_win, page_table,
           a_w_in, a_lb_logits, a_norm_g, a_w_out, b_w_qg, b_w_out, kv_w,
           cmp_pe, cmp_w1, cmp_b1, cmp_w2, ffn_w_in, ffn_w_out,
           moe_w_router, moe_b_router, moe_w_in, moe_w_out, ln_g, ln_b):
    weights = (a_w_in, a_lb_logits, a_norm_g, a_w_out, b_w_qg, b_w_out, kv_w, cmp_pe, cmp_w1, cmp_b1, cmp_w2,
               ffn_w_in, ffn_w_out, moe_w_router, moe_b_router, moe_w_in, moe_w_out, ln_g, ln_b)
    hgrn0 = jnp.zeros((x_prompt.shape[0], A_HEADS, A_DK, A_DV), F32)
    y_p, hg_p, cmp_p, sel_p, win_p = _run_group(x_prompt, hgrn0, True, None, *weights)
    y_s, hg_s, cmp_s, sel_s, win_s = _run_group(
        x_sample, state_hgrn[0], False, (cache_kv_cmp, cache_kv_sel, state_kv_win, page_table), *weights)
    return (y_p, y_s, hg_p, cmp_p, sel_p, win_p, hg_s, cmp_s, sel_s, win_s)
```

```python
import functools

import jax
import jax.numpy as jnp
from jax import lax
from jax.experimental import pallas as pl
from jax.experimental.pallas import tpu as pltpu

F32 = jnp.float32
BF16 = jnp.bfloat16

D_MODEL = 1024
DEPTH = 2
ALPHA = (2.0 * DEPTH) ** 0.25
LN_EPS = 1e-5
RMS_EPS = 1e-6
NEG_BIG = -1e30
FORCE = 1e6
PAGE_SIZE = 128

A_DK = 128
A_HEADS = D_MODEL // A_DK
A_DV = D_MODEL // A_HEADS
A_WIDTH = A_HEADS * A_DK
A_CHUNK = 64

B_HEADS = 16
B_KV = 4
B_GROUP = B_HEADS // B_KV
HEAD_DIM = D_MODEL // B_HEADS
SCALE = HEAD_DIM ** -0.5
CMP_STRIDE = 16
CMP_BLOCK = 2 * CMP_STRIDE
SEL_BLOCK = 64
TOP_N = 8
N_LOCAL = 2
WINDOW = 512

D_FF = 256 * ((8 * D_MODEL // 3 + 255) // 256)
N_EXPERTS = 8
MOE_TOP_K = 2
D_FF_E = 7 * D_MODEL // 2

VMEM_LIMIT_BYTES = 62 * 1024 * 1024


def _params(*sem):
    return pltpu.CompilerParams(dimension_semantics=sem, vmem_limit_bytes=VMEM_LIMIT_BYTES)


def _silu(x):
    return x * (1.0 / (1.0 + jnp.exp(-x)))


def _sigmoid(x):
    return 1.0 / (1.0 + jnp.exp(-x))


def _layer_norm(x, g, b):
    xc = x - jnp.mean(x, -1, keepdims=True)
    var = jnp.mean(xc * xc, -1, keepdims=True)
    return xc * lax.rsqrt(var + LN_EPS) * g + b


def _dot(a, b):
    return jnp.dot(a.astype(BF16), b.astype(BF16), preferred_element_type=F32)


def _dot_nt(a, b):
    return lax.dot_general(a.astype(BF16), b.astype(BF16), (((1,), (1,)), ((), ())),
                           preferred_element_type=F32)


def _dot_01(a01, b):
    hi = b.astype(BF16)
    lo = (b - hi.astype(F32)).astype(BF16)
    a01 = a01.astype(BF16)
    return jnp.dot(a01, hi, preferred_element_type=F32) + jnp.dot(a01, lo, preferred_element_type=F32)


def _dot_tn(a, b):
    return lax.dot_general(a.astype(BF16), b.astype(BF16), (((0,), (0,)), ((), ())),
                           preferred_element_type=F32)


def _hgrn_kernel(x_ref, s0_ref, win_ref, lbl_ref, ng_ref, wout_ref, lng_ref, lnb_ref,
                 h_ref, sout_ref, st_ref, *, layer, seg, n_seg, carry):
    rows = seg * n_seg
    x = x_ref[...].reshape(rows, D_MODEL)
    z = jnp.dot(x.astype(BF16), win_ref[...], preferred_element_type=F32)
    zq = z[:, 0 * A_WIDTH:1 * A_WIDTH]
    zf = z[:, 1 * A_WIDTH:2 * A_WIDTH]
    v = z[:, 2 * A_WIDTH:3 * A_WIDTH]
    zg = z[:, 3 * A_WIDTH:4 * A_WIDTH]

    lbl = lbl_ref[...]
    e = jnp.exp(lbl - jnp.max(lbl, axis=0, keepdims=True))
    lb = jnp.sum(e[:layer + 1], axis=0, keepdims=True) / jnp.sum(e, axis=0, keepdims=True)

    q = _silu(zq)
    f = lb + (1.0 - lb) * _sigmoid(zf)
    logf = jnp.log(f)
    k = 1.0 - f

    r_i = lax.broadcasted_iota(jnp.int32, (seg, seg), 0)
    c_i = lax.broadcasted_iota(jnp.int32, (seg, seg), 1)
    causal = c_i <= r_i
    tri = jnp.where(causal, 1.0, 0.0).astype(BF16)

    if carry:
        @pl.when(pl.program_id(1) == 0)
        def _():
            for h in range(A_HEADS):
                st_ref[h] = s0_ref[0, h].T

    o_parts = []
    for s in range(n_seg):
        sl = slice(s * seg, (s + 1) * seg)
        g = _dot_01(tri, logf[sl])
        glast = g[seg - 1:seg, :]
        qg = q[sl] * jnp.exp(g)
        kg = k[sl] * jnp.exp(-g)
        kd = k[sl] * jnp.exp(glast - g)
        eg = jnp.exp(glast)
        vs = v[sl]
        heads = []
        for h in range(A_HEADS):
            cl = slice(h * A_DK, (h + 1) * A_DK)
            if carry:
                st = st_ref[h]
            else:
                st = s0_ref[s, h].T
            att = jnp.where(causal, _dot_nt(qg[:, cl], kg[:, cl]), 0.0)
            o = _dot_nt(qg[:, cl], st) + _dot(att, vs[:, cl])
            st_new = eg[:, cl] * st + _dot_tn(vs[:, cl], kd[:, cl])
            if carry:
                st_ref[h] = st_new
            else:
                sout_ref[s, h] = st_new.T
            o = o * lax.rsqrt(jnp.mean(o * o, -1, keepdims=True) + RMS_EPS)
            heads.append(o)
        o_parts.append(jnp.concatenate(heads, axis=1))
    o = o_parts[0] if n_seg == 1 else jnp.concatenate(o_parts, axis=0)
    o = o * ng_ref[...] * _silu(zg)
    y = jnp.dot(o.astype(BF16), wout_ref[...], preferred_element_type=F32)
    hh = _layer_norm(ALPHA * x + y, lng_ref[...], lnb_ref[...])
    h_ref[...] = hh.reshape(h_ref.shape)

    if carry:
        @pl.when(pl.program_id(1) == pl.num_programs(1) - 1)
        def _():
            for h in range(A_HEADS):
                sout_ref[0, h] = st_ref[h].T


def hgrn_layer(x, s0, w_in, lb_logits, norm_g, w_out, ln_g, ln_b, *, layer, seg, n_seg, carry):
    bsz, t, _ = x.shape
    row2 = lambda a: a.reshape(1, -1).astype(F32)
    w_in = w_in.astype(BF16)
    w_out = w_out.astype(BF16)
    const = lambda *_: (0, 0)
    if carry:
        tile = seg * n_seg
        grid = (bsz, t // tile)
        x_spec = pl.BlockSpec((1, tile, D_MODEL), lambda b, c: (b, c, 0))
        s_spec = pl.BlockSpec((1, A_HEADS, A_DK, A_DV), lambda b, c: (b, 0, 0, 0))
        sem = ("parallel", "arbitrary")
    else:
        assert t == seg
        grid = (bsz // n_seg, 1)
        x_spec = pl.BlockSpec((n_seg, seg, D_MODEL), lambda b, c: (b, 0, 0))
        s_spec = pl.BlockSpec((n_seg, A_HEADS, A_DK, A_DV), lambda b, c: (b, 0, 0, 0))
        sem = ("parallel", "arbitrary")
    kern = functools.partial(_hgrn_kernel, layer=layer, seg=seg, n_seg=n_seg, carry=carry)
    return pl.pallas_call(
        kern,
        grid=grid,
        in_specs=[
            x_spec, s_spec,
            pl.BlockSpec((D_MODEL, 4 * A_WIDTH), const),
            pl.BlockSpec(lb_logits.shape, const),
            pl.BlockSpec((1, A_WIDTH), const),
            pl.BlockSpec((A_WIDTH, D_MODEL), const),
            pl.BlockSpec((1, D_MODEL), const),
            pl.BlockSpec((1, D_MODEL), const),
        ],
        out_specs=[x_spec, s_spec],
        out_shape=[jax.ShapeDtypeStruct(x.shape, F32), jax.ShapeDtypeStruct(s0.shape, F32)],
        scratch_shapes=[pltpu.VMEM((A_HEADS, A_DV, A_DK), F32)],
        compiler_params=_params(*sem),
        name="hgrn_layer",
    )(x, s0, w_in, lb_logits.astype(F32), row2(norm_g), w_out, row2(ln_g), row2(ln_b))


def _ffn_kernel(x_ref, wa_ref, wu_ref, wo_ref, lng_ref, lnb_ref, o_ref, acc_ref, xb_ref, *, rb):
    j = pl.program_id(1)

    @pl.when(j == 0)
    def _():
        xb_ref[...] = x_ref[...].astype(BF16)

    def row_block(b, _):
        rows = pl.ds(pl.multiple_of(b * rb, rb), rb)
        xb = xb_ref[rows, :]
        a = jnp.dot(xb, wa_ref[...], preferred_element_type=F32)
        u = jnp.dot(xb, wu_ref[...], preferred_element_type=F32)
        part = jnp.dot((_silu(a) * u).astype(BF16), wo_ref[...], preferred_element_type=F32)

        @pl.when(j == 0)
        def _():
            acc_ref[rows, :] = part

        @pl.when(j > 0)
        def _():
            acc_ref[rows, :] = acc_ref[rows, :] + part
        return 0

    lax.fori_loop(0, x_ref.shape[0] // rb, row_block, 0)

    @pl.when(j == pl.num_programs(1) - 1)
    def _():
        o_ref[...] = _layer_norm(ALPHA * x_ref[...] + acc_ref[...], lng_ref[...], lnb_ref[...])


def ffn_layer(x, w_in, w_out, ln_g, ln_b, *, tm, tf):
    n, d = x.shape
    d_ff = w_out.shape[0]
    nf = d_ff // tf
    rb = min(tm, 256)
    assert n % tm == 0 and d_ff % tf == 0 and tm % rb == 0
    w_in = w_in.astype(BF16)
    w_out = w_out.astype(BF16)
    row2 = lambda a: a.reshape(1, -1).astype(F32)
    return pl.pallas_call(
        functools.partial(_ffn_kernel, rb=rb),
        grid=(n // tm, nf),
        in_specs=[
            pl.BlockSpec((tm, d), lambda i, j: (i, 0)),
            pl.BlockSpec((d, tf), lambda i, j: (0, j)),
            pl.BlockSpec((d, tf), lambda i, j: (0, j + nf)),
            pl.BlockSpec((tf, d), lambda i, j: (j, 0)),
            pl.BlockSpec((1, d), lambda i, j: (0, 0)),
            pl.BlockSpec((1, d), lambda i, j: (0, 0)),
        ],
        out_specs=pl.BlockSpec((tm, d), lambda i, j: (i, 0)),
        out_shape=jax.ShapeDtypeStruct((n, d), F32),
        scratch_shapes=[pltpu.VMEM((tm, d), F32), pltpu.VMEM((tm, d), BF16)],
        compiler_params=_params("parallel", "arbitrary"),
        name="ffn_layer",
    )(x, w_in, w_in, w_out, row2(ln_g), row2(ln_b))


MOE_RB = 256
MOE_TAIL = 128
MOE_ST = 512


def _router_gates_t(x, wrt, br):
    logits = lax.dot_general(wrt, x, (((1,), (1,)), ((), ())), preferred_element_type=F32,
                             precision=lax.Precision.HIGHEST) + br
    eidx = lax.broadcasted_iota(jnp.int32, logits.shape, 0)
    m1 = jnp.max(logits, axis=0, keepdims=True)
    i1 = jnp.min(jnp.where(logits == m1, eidx, N_EXPERTS), axis=0, keepdims=True)
    rest = jnp.where(eidx == i1, -jnp.inf, logits)
    m2 = jnp.max(rest, axis=0, keepdims=True)
    i2 = jnp.min(jnp.where(rest == m2, eidx, N_EXPERTS), axis=0, keepdims=True)
    e2 = jnp.exp(m2 - m1)
    den = 1.0 + e2
    return jnp.where(eidx == i1, 1.0 / den, 0.0) + jnp.where(eidx == i2, e2 / den, 0.0)


def _moe_routed_kernel(cum_ref, xb_ref, gate_ref, rank_ref, wa_ref, wu_ref, wo_ref, y_ref, xs_ref, acc_ref,
                       *, n_sub, st):
    i = pl.program_id(0)
    e = pl.program_id(1)
    j = pl.program_id(2)
    nj = pl.num_programs(2)
    base = (i * N_EXPERTS + e) * (n_sub + 1)
    count = cum_ref[base + n_sub]
    n_full = (count + (MOE_RB - MOE_TAIL - 1)) // MOE_RB

    def for_blocks(fn):
        def body(b, _):
            fn(pl.multiple_of(b * MOE_RB, MOE_RB), MOE_RB)
            return 0

        lax.fori_loop(0, n_full, body, 0)

        @pl.when(count > n_full * MOE_RB)
        def _():
            fn(pl.multiple_of(n_full * MOE_RB, MOE_RB), MOE_TAIL)

    def for_subtiles(r0, rb, fn):
        def body(c, _):
            @pl.when((cum_ref[base + c] < r0 + rb) & (cum_ref[base + c + 1] > r0))
            def _():
                fn(c)
            return 0

        lax.fori_loop(0, n_sub, body, 0)

    def one_hot(r0, rb, c):
        c0 = pl.multiple_of(c * st, st)
        rank = rank_ref[pl.ds(e, 1), pl.ds(c0, st)]
        gate = gate_ref[pl.ds(e, 1), pl.ds(c0, st)]
        hit = (rank == r0 + lax.broadcasted_iota(jnp.int32, (rb, st), 0)) & (gate > 0.0)
        return hit, gate, c0

    def gather_block(r0, rb):
        rows = pl.ds(r0, rb)
        xs_ref[rows, :] = jnp.zeros((rb, D_MODEL), BF16)

        def sub(c):
            hit, _, c0 = one_hot(r0, rb, c)
            part = jnp.dot(jnp.where(hit, 1.0, 0.0).astype(BF16), xb_ref[pl.ds(c0, st), :],
                           preferred_element_type=F32)
            xs_ref[rows, :] = xs_ref[rows, :] + part.astype(BF16)

        for_subtiles(r0, rb, sub)

    def ffn_block(r0, rb):
        rows = pl.ds(r0, rb)
        xs = xs_ref[rows, :]
        a = jnp.dot(xs, wa_ref[0], preferred_element_type=F32)
        u = jnp.dot(xs, wu_ref[0], preferred_element_type=F32)
        part = jnp.dot((_silu(a) * u).astype(BF16), wo_ref[0], preferred_element_type=F32)

        @pl.when(j == 0)
        def _():
            acc_ref[rows, :] = part

        @pl.when(j > 0)
        def _():
            acc_ref[rows, :] = acc_ref[rows, :] + part

    def scatter_block(r0, rb):
        rows = pl.ds(r0, rb)

        def sub(c):
            hit, gate, c0 = one_hot(r0, rb, c)
            g_rows = jnp.sum(jnp.where(hit, gate, 0.0), axis=1, keepdims=True)
            out = (acc_ref[rows, :] * g_rows).astype(BF16)
            y_ref[pl.ds(c0, st), :] += _dot_tn(jnp.where(hit, 1.0, 0.0).astype(BF16), out)

        for_subtiles(r0, rb, sub)

    @pl.when((e == 0) & (j == 0))
    def _():
        y_ref[...] = jnp.zeros_like(y_ref)

    @pl.when(j == 0)
    def _():
        for_blocks(gather_block)

    for_blocks(ffn_block)

    @pl.when(j == nj - 1)
    def _():
        for_blocks(scatter_block)


def moe_routed(xb, gate_t, w_in, w_out, *, tm, tf):
    n, d = xb.shape
    ne, d_ff = w_out.shape[0], w_out.shape[1]
    nf = d_ff // tf
    st = min(MOE_ST, tm)
    n_tiles, n_sub = n // tm, tm // st
    assert n % tm == 0 and tm % st == 0 and d_ff % tf == 0 and tm % MOE_RB == 0
    mask = (gate_t > 0.0).astype(jnp.int32).reshape(ne, n_tiles, tm)
    rank = (jnp.cumsum(mask, axis=-1) - mask).reshape(ne, n)
    cnt = mask.reshape(ne, n_tiles, n_sub, st).sum(-1)
    cum = jnp.concatenate([jnp.zeros((ne, n_tiles, 1), jnp.int32), jnp.cumsum(cnt, axis=-1)], axis=-1)
    cum = cum.transpose(1, 0, 2).reshape(-1).astype(jnp.int32)
    w_in = w_in.astype(BF16)
    w_out = w_out.astype(BF16)
    kern = functools.partial(_moe_routed_kernel, n_sub=n_sub, st=st)
    return pl.pallas_call(
        kern,
        grid_spec=pltpu.PrefetchScalarGridSpec(
            num_scalar_prefetch=1,
            grid=(n_tiles, ne, nf),
            in_specs=[
                pl.BlockSpec((tm, d), lambda i, e, j, c: (i, 0)),
                pl.BlockSpec((ne, tm), lambda i, e, j, c: (0, i)),
                pl.BlockSpec((ne, tm), lambda i, e, j, c: (0, i)),
                pl.BlockSpec((1, d, tf), lambda i, e, j, c: (e, 0, j)),
                pl.BlockSpec((1, d, tf), lambda i, e, j, c: (e, 0, j + nf)),
                pl.BlockSpec((1, tf, d), lambda i, e, j, c: (e, j, 0)),
            ],
            out_specs=pl.BlockSpec((tm, d), lambda i, e, j, c: (i, 0)),
            scratch_shapes=[pltpu.VMEM((tm, d), BF16), pltpu.VMEM((tm, d), F32)]),
        out_shape=jax.ShapeDtypeStruct((n, d), F32),
        compiler_params=_params("parallel", "arbitrary", "arbitrary"),
        name="moe_routed",
    )(cum, xb, gate_t, rank, w_in, w_in, w_out)


def _add_ln_kernel(x_ref, y_ref, lng_ref, lnb_ref, o_ref):
    o_ref[...] = _layer_norm(ALPHA * x_ref[...] + y_ref[...], lng_ref[...], lnb_ref[...])


def add_ln(x, y, ln_g, ln_b, *, tm):
    n, d = x.shape
    row2 = lambda a: a.reshape(1, -1).astype(F32)
    blk = pl.BlockSpec((tm, d), lambda i: (i, 0))
    c2 = pl.BlockSpec((1, d), lambda i: (0, 0))
    return pl.pallas_call(
        _add_ln_kernel, grid=(n // tm,), in_specs=[blk, blk, c2, c2], out_specs=blk,
        out_shape=jax.ShapeDtypeStruct((n, d), F32), compiler_params=_params("parallel"), name="add_ln",
    )(x, y, row2(ln_g), row2(ln_b))


def moe_layer(x, xb, gate_t, w_in, w_out, ln_g, ln_b, *, tm, tf):
    y = moe_routed(xb, gate_t, w_in, w_out, tm=tm, tf=tf)
    return add_ln(x, y, ln_g, ln_b, tm=_tile(x.shape[0], 1024))


def _proj_kernel(x_ref, w_ref, o_ref):
    o_ref[...] = jnp.dot(x_ref[...].astype(BF16), w_ref[...], preferred_element_type=F32)


def proj(x, w, *, tm):
    n, d = x.shape
    m = w.shape[1]
    return pl.pallas_call(
        _proj_kernel,
        grid=(n // tm,),
        in_specs=[pl.BlockSpec((tm, d), lambda i: (i, 0)), pl.BlockSpec((d, m), lambda i: (0, 0))],
        out_specs=pl.BlockSpec((tm, m), lambda i: (i, 0)),
        out_shape=jax.ShapeDtypeStruct((n, m), F32),
        compiler_params=_params("parallel"),
        name="proj",
    )(x, w.astype(BF16))


def _proj_ln_router_kernel(o_ref, x_ref, w_ref, lng_ref, lnb_ref, wrt_ref, br_ref, h_ref, hb_ref, gate_ref):
    y = jnp.dot(o_ref[...].astype(BF16), w_ref[...], preferred_element_type=F32)
    h = _layer_norm(ALPHA * x_ref[...] + y, lng_ref[...], lnb_ref[...])
    h_ref[...] = h
    hb_ref[...] = h.astype(BF16)
    gate_ref[...] = _router_gates_t(h, wrt_ref[...], br_ref[...])


def proj_ln_router(o, x, w, ln_g, ln_b, w_router, b_router, *, tm):
    n, d = x.shape
    row2 = lambda a: a.reshape(1, -1).astype(F32)
    c2 = lambda i: (0, 0)
    blk = pl.BlockSpec((tm, d), lambda i: (i, 0))
    return pl.pallas_call(
        _proj_ln_router_kernel,
        grid=(n // tm,),
        in_specs=[pl.BlockSpec((tm, o.shape[1]), lambda i: (i, 0)), blk,
                  pl.BlockSpec(w.shape, c2), pl.BlockSpec((1, d), c2), pl.BlockSpec((1, d), c2),
                  pl.BlockSpec((N_EXPERTS, d), c2), pl.BlockSpec((N_EXPERTS, 1), c2)],
        out_specs=[blk, blk, pl.BlockSpec((N_EXPERTS, tm), lambda i: (0, i))],
        out_shape=[jax.ShapeDtypeStruct((n, d), F32), jax.ShapeDtypeStruct((n, d), BF16),
                   jax.ShapeDtypeStruct((N_EXPERTS, n), F32)],
        compiler_params=_params("parallel"),
        name="proj_ln_router",
    )(o, x, w.astype(BF16), row2(ln_g), row2(ln_b), w_router.T.astype(F32),
      b_router.reshape(N_EXPERTS, 1).astype(F32))


N_KV_COLS = 6 * B_KV * HEAD_DIM
KV_SET = 2 * B_KV * HEAD_DIM
N_Q_COLS = B_HEADS * HEAD_DIM
GATE_LANES = 128
LOG2_E = 1.4426950408889634
V_AUG = 2 * HEAD_DIM
ACC_ROWS = HEAD_DIM + 8


def _nsa_proj_kernel(x_ref, w_ref, cmp_ref, sel_ref, win_ref, q_ref, ks_ref, vs_ref, kw_ref, vw_ref, g_ref):
    z = jnp.dot(x_ref[0].astype(BF16), w_ref[...], preferred_element_type=F32)
    cmp_ref[0] = z[:, 0:KV_SET]
    sel_ref[0] = z[:, KV_SET:2 * KV_SET]
    win_ref[0] = z[:, 2 * KV_SET:3 * KV_SET]
    half = B_KV * HEAD_DIM
    lane = lax.broadcasted_iota(jnp.int32, (z.shape[0], V_AUG), 1)

    def v_aug(lo):
        return jnp.where(lane < HEAD_DIM, z[:, lo:lo + V_AUG], jnp.where(lane == HEAD_DIM, 1.0, 0.0)).astype(BF16)

    for g in range(B_KV):
        lo = KV_SET + g * HEAD_DIM
        ks_ref[0, g] = z[:, lo:lo + HEAD_DIM].astype(BF16)
        vs_ref[0, g] = v_aug(lo + half)
        lo = 2 * KV_SET + g * HEAD_DIM
        kw_ref[0, g] = z[:, lo:lo + HEAD_DIM].astype(BF16)
        vw_ref[0, g] = v_aug(lo + half)
    for h in range(B_HEADS):
        lo = N_KV_COLS + h * HEAD_DIM
        q_ref[0, h] = (z[:, lo:lo + HEAD_DIM] * (SCALE * LOG2_E)).astype(BF16)
    g_ref[0] = _sigmoid(z[:, N_KV_COLS + N_Q_COLS:N_KV_COLS + N_Q_COLS + GATE_LANES])


def nsa_proj(h, kv_w, w_qg, *, tm):
    bsz, t, d = h.shape
    w = jnp.concatenate([kv_w, w_qg], axis=1)
    w = jnp.pad(w, ((0, 0), (0, N_KV_COLS + N_Q_COLS + GATE_LANES - w.shape[1]))).astype(BF16)
    row = lambda: pl.BlockSpec((1, tm, KV_SET), lambda b, i: (b, i, 0))
    hm = lambda nh, w=HEAD_DIM: pl.BlockSpec((1, nh, tm, w), lambda b, i: (b, 0, i, 0))
    sds = jax.ShapeDtypeStruct
    kshape, vshape = sds((bsz, B_KV, t, HEAD_DIM), BF16), sds((bsz, B_KV, t, V_AUG), BF16)
    return pl.pallas_call(
        _nsa_proj_kernel,
        grid=(bsz, t // tm),
        in_specs=[pl.BlockSpec((1, tm, d), lambda b, i: (b, i, 0)), pl.BlockSpec(w.shape, lambda b, i: (0, 0))],
        out_specs=[row(), row(), row(), hm(B_HEADS), hm(B_KV), hm(B_KV, V_AUG), hm(B_KV), hm(B_KV, V_AUG),
                   pl.BlockSpec((1, tm, GATE_LANES), lambda b, i: (b, i, 0))],
        out_shape=[sds((bsz, t, KV_SET), F32)] * 3 + [sds((bsz, B_HEADS, t, HEAD_DIM), BF16)]
        + [kshape, vshape, kshape, vshape] + [sds((bsz, t, GATE_LANES), F32)],
        compiler_params=_params("parallel", "parallel"),
        name="nsa_proj",
    )(h, w)


PAIR = 2 * HEAD_DIM


def _compress_kernel(*refs, n_in, head_major, paged):
    if paged:
        refs = refs[1:]
    x_refs = refs[:n_in]
    wp_ref, u_ref, b1_ref, w2_ref, kc_ref, vc_ref, carry_ref = refs[n_in:]
    i = pl.program_id(1)

    @pl.when(i == 0)
    def _():
        carry_ref[...] = jnp.zeros_like(carry_ref)

    cpp = PAGE_SIZE // CMP_STRIDE
    pr = lax.broadcasted_iota(jnp.int32, (PAGE_SIZE, PAGE_SIZE), 0)
    pc = lax.broadcasted_iota(jnp.int32, (PAGE_SIZE, PAGE_SIZE), 1)
    perm = jnp.where(pc == CMP_STRIDE * (pr % cpp) + pr // cpp, 1.0, 0.0).astype(BF16)
    pages = []
    for r in x_refs:
        if paged:
            pages.append(_dot_nt(perm, r[0]))
        else:
            for p0 in range(0, r.shape[1], PAGE_SIZE):
                pages.append(jnp.dot(perm, r[0, p0:p0 + PAGE_SIZE, :].astype(BF16), preferred_element_type=F32))
    ch = len(pages) * cpp
    row0 = lax.broadcasted_iota(jnp.int32, (ch, PAIR), 0) == 0

    def chunk_rows(s, lo):
        parts = [pg[s * cpp:(s + 1) * cpp, lo:lo + PAIR] for pg in pages]
        return parts[0] if len(parts) == 1 else jnp.concatenate(parts, axis=0)
    for k in range(2):
        wp = wp_ref[k]
        pbm = jnp.dot(u_ref[k], wp, preferred_element_type=F32)
        pb = pbm[0:1, 0:PAIR] + pbm[1:2, PAIR:2 * PAIR] + b1_ref[k]
        for gp in range(B_KV // 2):
            base = k * B_KV * HEAD_DIM + gp * PAIR
            lhs = jnp.concatenate([chunk_rows(s, base) for s in range(CMP_STRIDE)], axis=1).astype(BF16)
            r = jnp.dot(lhs, wp, preferred_element_type=F32)
            first, second = r[:, 0:PAIR], r[:, PAIR:2 * PAIR]
            slot = k * (B_KV // 2) + gp
            prev = carry_ref[slot]
            shifted = jnp.where(row0, prev[7:8, :], pltpu.roll(first, 1, axis=0))
            carry_ref[slot] = first[ch - 8:ch, :]
            hid = jax.nn.gelu(shifted + second + pb)
            out = jnp.dot(hid.astype(BF16), w2_ref[k], preferred_element_type=F32).astype(BF16)
            dst = kc_ref if k == 0 else vc_ref
            if head_major:
                for g2 in range(2):
                    dst[0, 2 * gp + g2] = out[:, g2 * HEAD_DIM:(g2 + 1) * HEAD_DIM]
            else:
                dst[0, :, gp * PAIR:(gp + 1) * PAIR] = out


def _compress_weights(cmp_pe, cmp_w1, cmp_b1, cmp_w2):
    eye2 = jnp.eye(2, dtype=F32)
    w1 = cmp_w1.reshape(2, CMP_STRIDE, 2, HEAD_DIM, HEAD_DIM)
    wp = jnp.einsum('fskdh,ab->ksadfbh', w1, eye2).reshape(2, CMP_STRIDE * PAIR, 2 * PAIR)
    pe = cmp_pe.reshape(2, CMP_STRIDE, 2, HEAD_DIM)
    u = jnp.broadcast_to(pe.transpose(2, 0, 1, 3)[:, :, :, None, :], (2, 2, CMP_STRIDE, 2, HEAD_DIM))
    u = jnp.pad(u.reshape(2, 2, CMP_STRIDE * PAIR), ((0, 0), (0, 6), (0, 0)))
    b1 = jnp.tile(cmp_b1, (1, 2)).reshape(2, 1, PAIR)
    w2 = jnp.einsum('khd,ab->kahbd', cmp_w2, eye2).reshape(2, PAIR, PAIR)
    return wp.astype(BF16), u.astype(BF16), b1.astype(F32), w2.astype(BF16)


def compress(x, page_table, cmp_pe, cmp_w1, cmp_b1, cmp_w2, *, ch, head_major):
    wp, u, b1, w2 = _compress_weights(cmp_pe, cmp_w1, cmp_b1, cmp_w2)
    paged = page_table is not None
    if paged:
        bsz, n_pages = page_table.shape
        n_in = ch * CMP_STRIDE // PAGE_SIZE
        n_chunks = n_pages * PAGE_SIZE // CMP_STRIDE
        x_specs = [pl.BlockSpec((1, KV_SET, PAGE_SIZE), functools.partial(
            lambda b, i, pt, j: (pt[b, i * n_in + j], 0, 0), j=j)) for j in range(n_in)]
        cm = lambda f: (lambda b, i, pt: f(b, i))
    else:
        bsz, t, _ = x.shape
        n_chunks = t // CMP_STRIDE
        n_in = 1
        x_specs = [pl.BlockSpec((1, ch * CMP_STRIDE, KV_SET), lambda b, i: (b, i, 0))]
        cm = lambda f: f
    c3 = cm(lambda b, i: (0, 0, 0))
    if head_major:
        o_spec = pl.BlockSpec((1, B_KV, ch, HEAD_DIM), cm(lambda b, i: (b, 0, i, 0)))
        o_shape = jax.ShapeDtypeStruct((bsz, B_KV, n_chunks, HEAD_DIM), BF16)
    else:
        o_spec = pl.BlockSpec((1, ch, B_KV * HEAD_DIM), cm(lambda b, i: (b, i, 0)))
        o_shape = jax.ShapeDtypeStruct((bsz, n_chunks, B_KV * HEAD_DIM), BF16)
    in_specs = x_specs + [pl.BlockSpec(wp.shape, c3), pl.BlockSpec(u.shape, c3),
                          pl.BlockSpec(b1.shape, c3), pl.BlockSpec(w2.shape, c3)]
    grid = (bsz, n_chunks // ch)
    scratch = [pltpu.VMEM((2 * (B_KV // 2), 8, PAIR), F32)]
    kern = functools.partial(_compress_kernel, n_in=n_in, head_major=head_major, paged=paged)
    if paged:
        grid_spec = pltpu.PrefetchScalarGridSpec(num_scalar_prefetch=1, grid=grid, in_specs=in_specs,
                                                 out_specs=[o_spec, o_spec], scratch_shapes=scratch)
        args = (page_table,) + (x,) * n_in
    else:
        grid_spec = pl.GridSpec(grid=grid, in_specs=in_specs, out_specs=[o_spec, o_spec], scratch_shapes=scratch)
        args = (x,)
    return pl.pallas_call(
        kern, grid_spec=grid_spec, out_shape=[o_shape, o_shape],
        compiler_params=_params("parallel", "arbitrary"), name="compress",
    )(*args, wp, u, b1, w2)


def _select_blocks(imp, blk, cur, n_sel, axis):
    valid = blk <= cur
    forced = (blk == 0) | (valid & (blk > cur - N_LOCAL))
    score = jnp.where(forced, FORCE, jnp.where(valid, imp, -FORCE))
    score = jnp.where(blk < n_sel, score, -jnp.inf)
    out = jnp.full(score.shape, NEG_BIG, F32)
    big = jnp.int32(2 ** 30)
    for _ in range(min(TOP_N, n_sel)):
        m = jnp.max(score, axis=axis, keepdims=True)
        first = jnp.min(jnp.where(score == m, blk, big), axis=axis, keepdims=True)
        pick = blk == first
        out = jnp.where(pick, 0.0, out)
        score = jnp.where(pick, -jnp.inf, score)
    return jnp.where(valid, out, NEG_BIG)


def _overlap(c1, j, n_sel):
    c0 = (c1 - 1) * CMP_STRIDE
    j0 = j * SEL_BLOCK
    return ((c1 >= 1) & (j < n_sel) & (c0 <= j0 + SEL_BLOCK - 1) & (c0 + CMP_BLOCK - 1 >= j0)).astype(F32)


def _rank_select(imp, blk, cur, n_sel):
    valid = blk <= cur
    forced = (blk == 0) | (valid & (blk > cur - N_LOCAL))
    score = jnp.where(forced, FORCE, jnp.where(valid, imp, -FORCE))
    rank = jnp.zeros(score.shape, F32)
    for jp in range(n_sel):
        row = score[jp:jp + 1, :]
        gt = jnp.where(row > score, 1.0, 0.0)
        ge = jnp.where(row >= score, 1.0, 0.0)
        rank = rank + jnp.where(blk > jp, ge, gt)
    keep = jnp.where(rank < TOP_N, 0.0, NEG_BIG)
    return jnp.where(valid, jnp.where(blk < n_sel, keep, NEG_BIG), NEG_BIG)


def _online_softmax_t(s, m, acc, v):
    m_new = jnp.maximum(m, jnp.max(s, axis=0, keepdims=True))
    p = jnp.exp2((s - m_new).astype(BF16))
    acc = jnp.exp2(m - m_new) * acc + _dot_tn(v, p)[:ACC_ROWS]
    return m_new, acc


def _prompt_attn_kernel(q_ref, kc_ref, vc_ref, ks_ref, vs_ref, kw_ref, vw_ref, g_ref, o_ref, gt_ref,
                        *, tq, tk, n_sel, nb):
    g = pl.program_id(2)
    t0 = pl.program_id(1) * tq
    cols = B_GROUP * tq
    q = q_ref[0].reshape(cols, HEAD_DIM)
    tpos = t0 + lax.broadcasted_iota(jnp.int32, (1, tq), 1)
    nch = kc_ref.shape[2]
    rep = lambda a: jnp.concatenate([a] * B_GROUP, axis=1)

    c1 = lax.broadcasted_iota(jnp.int32, (nch, 1), 0)
    cmask = rep((c1 >= 1) & (c1 * CMP_STRIDE + (CMP_BLOCK - CMP_STRIDE - 1) <= tpos))
    s = jnp.where(cmask, _dot_nt(kc_ref[0, 0], q), NEG_BIG)
    m = jnp.max(s, axis=0, keepdims=True)
    e = jnp.where(cmask, jnp.exp2(s - m), 0.0)
    l = jnp.sum(e, axis=0, keepdims=True)
    p = e * (1.0 / jnp.where(l == 0.0, 1.0, l))
    o_cmp = _dot_tn(vc_ref[0, 0], p)

    p4 = p[:, 0:tq]
    for r in range(1, B_GROUP):
        p4 = p4 + p[:, r * tq:(r + 1) * tq]
    ov_t = _overlap(lax.broadcasted_iota(jnp.int32, (nb, nch), 1),
                    lax.broadcasted_iota(jnp.int32, (nb, nch), 0), n_sel)
    imp = _dot_01(ov_t, p4)
    blk = lax.broadcasted_iota(jnp.int32, (nb, tq), 0)
    bias = _rank_select(imp, blk, tpos // SEL_BLOCK, n_sel).astype(BF16)

    def sel_scores(kt):
        k0 = pl.multiple_of(kt * tk, tk)
        kpos = k0 + lax.broadcasted_iota(jnp.int32, (tk, 1), 0)
        blk_of_key = (k0 + lax.broadcasted_iota(jnp.int32, (tk, nb), 0)) // SEL_BLOCK
        onehot = jnp.where(blk_of_key == lax.broadcasted_iota(jnp.int32, (tk, nb), 1), 1.0, 0.0).astype(BF16)
        mb = jnp.dot(onehot, bias, preferred_element_type=F32)
        mb = jnp.where(kpos <= tpos, mb, NEG_BIG)
        return _dot_nt(ks_ref[0, 0, pl.ds(k0, tk), :], q) + rep(mb), vs_ref[0, 0, pl.ds(k0, tk), :]

    def win_scores(kt):
        k0 = pl.multiple_of(kt * tk, tk)
        kpos = k0 + lax.broadcasted_iota(jnp.int32, (tk, 1), 0)
        mb = jnp.where((kpos <= tpos) & (kpos >= tpos - WINDOW), 0.0, NEG_BIG)
        return _dot_nt(kw_ref[0, 0, pl.ds(k0, tk), :], q) + rep(mb), vw_ref[0, 0, pl.ds(k0, tk), :]

    def sel_only(kt, carry):
        s, v = sel_scores(kt)
        return _online_softmax_t(s, *carry, v)

    init = (jnp.full((1, cols), NEG_BIG, F32), jnp.zeros((ACC_ROWS, cols), F32))
    hi = (t0 + tq - 1) // tk + 1
    lo_w = jnp.maximum(t0 - WINDOW, 0) // tk

    def both(kt, carry):
        s, v = sel_scores(kt)
        sw, vw = win_scores(kt)
        return _online_softmax_t(s, *carry[:2], v) + _online_softmax_t(sw, *carry[2:], vw)

    sel_state = lax.fori_loop(0, lo_w, sel_only, init)
    _, acc_s, _, acc_w = lax.fori_loop(lo_w, hi, both, sel_state + init)

    gt_ref[...] = g_ref[0].T
    gate = lambda br: jnp.concatenate(
        [gt_ref[pl.ds(g * (3 * B_GROUP) + 3 * r + br, 1), :] for r in range(B_GROUP)], axis=1)
    norm = lambda acc: acc[:HEAD_DIM] * (1.0 / acc[HEAD_DIM:HEAD_DIM + 1])
    o = gate(0) * o_cmp + gate(1) * norm(acc_s) + gate(2) * norm(acc_w)
    for r in range(B_GROUP):
        o_ref[0, :, r * HEAD_DIM:(r + 1) * HEAD_DIM] = o[:, r * tq:(r + 1) * tq].T.astype(o_ref.dtype)


def nsa_prompt_attn(q, kc, vc, ks, vs, kw, vw, gates, *, tq, tk):
    bsz, _, t, _ = q.shape
    nch = kc.shape[2]
    n_sel = -(-t // SEL_BLOCK)
    nb = -(-n_sel // 16) * 16
    assert t % tq == 0 and t % tk == 0 and tq % 128 == 0
    seq = lambda n, w=HEAD_DIM: pl.BlockSpec((1, 1, n, w), lambda b, i, g: (b, g, 0, 0))
    kern = functools.partial(_prompt_attn_kernel, tq=tq, tk=tk, n_sel=n_sel, nb=nb)
    return pl.pallas_call(
        kern,
        grid=(bsz, t // tq, B_KV),
        in_specs=[pl.BlockSpec((1, B_GROUP, tq, HEAD_DIM), lambda b, i, g: (b, g, i, 0)),
                  seq(nch), seq(nch), seq(t), seq(t, V_AUG), seq(t), seq(t, V_AUG),
                  pl.BlockSpec((1, tq, GATE_LANES), lambda b, i, g: (b, i, 0))],
        out_specs=pl.BlockSpec((1, tq, B_GROUP * HEAD_DIM), lambda b, i, g: (b, i, g)),
        out_shape=jax.ShapeDtypeStruct((bsz, t, N_Q_COLS), BF16),
        scratch_shapes=[pltpu.VMEM((GATE_LANES, tq), F32)],
        compiler_params=_params("parallel", "parallel", "arbitrary"),
        name="nsa_prompt_attn",
    )(q, kc, vc, ks, vs, kw, vw, gates)


KV_HALF = B_KV * HEAD_DIM
Q_COLS = B_GROUP * B_KV * 8


def _sample_attn_kernel(*refs, n_pages_step, past, t_new, n_keep):
    pt_ref, qbd_ref, gl_ref, kc_ref, vc_ref, swin_ref, nsel_ref, nwin_ref = refs[:8]
    page_refs = refs[8:8 + n_pages_step]
    out_ref, bias_ref, m_ref, l_ref, acc_ref, oth_ref = refs[8 + n_pages_step:]
    del pt_ref
    i = pl.program_id(1)
    qbd = qbd_ref[0]
    ncol = qbd.shape[1]
    col = lax.broadcasted_iota(jnp.int32, (1, ncol), 1)
    tcol = col % t_new
    qpos = past + tcol
    sig = _sigmoid(gl_ref[0])
    n_sel = -(-(past + t_new) // SEL_BLOCK)
    blk_step = n_pages_step * PAGE_SIZE // SEL_BLOCK
    pad_rows = 8

    def scores(k):
        return jnp.dot(k.astype(BF16), qbd, preferred_element_type=F32) * SCALE

    @pl.when(i == 0)
    def _():
        nch = kc_ref.shape[1]
        c1 = lax.broadcasted_iota(jnp.int32, (nch, 1), 0)
        cmask = (c1 >= 1) & (c1 * CMP_STRIDE + (CMP_BLOCK - CMP_STRIDE - 1) <= qpos)
        s = jnp.where(cmask, scores(kc_ref[0]), NEG_BIG)
        m = jnp.max(s, axis=0, keepdims=True)
        e = jnp.where(cmask, jnp.exp(s - m), 0.0)
        l = jnp.sum(e, axis=0, keepdims=True)
        p = e / jnp.where(l == 0.0, 1.0, l)
        o_cmp = _dot_tn(vc_ref[0], p)
        nb = bias_ref.shape[0]
        ov_t = _overlap(lax.broadcasted_iota(jnp.int32, (nb, nch), 1),
                        lax.broadcasted_iota(jnp.int32, (nb, nch), 0), n_sel)
        imp = _dot_01(ov_t, p)
        per = ncol // B_GROUP
        imp = imp + pltpu.roll(imp, per, axis=1) + pltpu.roll(imp, 2 * per, axis=1) + pltpu.roll(imp, 3 * per, axis=1)
        blk = lax.broadcasted_iota(jnp.int32, (nb, ncol), 0)
        bias_ref[...] = _select_blocks(imp, blk, qpos // SEL_BLOCK, n_sel, axis=0)
        kv_w = jnp.concatenate([swin_ref[0], nwin_ref[0], jnp.zeros((pad_rows, 2 * KV_HALF), F32)], axis=0)
        nw = kv_w.shape[0]
        wi = lax.broadcasted_iota(jnp.int32, (nw, 1), 0)
        wpos = past - n_keep + wi
        wmask = (wi < n_keep + t_new) & (wpos <= qpos) & (wpos >= qpos - WINDOW) & (wpos >= 0)
        s = jnp.where(wmask, scores(kv_w[:, :KV_HALF]), NEG_BIG)
        m = jnp.max(s, axis=0, keepdims=True)
        e = jnp.where(wmask, jnp.exp(s - m), 0.0)
        l = jnp.sum(e, axis=0, keepdims=True)
        o_win = _dot_tn(kv_w[:, KV_HALF:], e / jnp.where(l == 0.0, 1.0, l))
        oth_ref[...] = sig[0:1] * o_cmp + sig[2:3] * o_win
        m_ref[...] = jnp.full(m_ref.shape, NEG_BIG, F32)
        l_ref[...] = jnp.zeros(l_ref.shape, F32)
        acc_ref[...] = jnp.zeros(acc_ref.shape, F32)

    def update(s, v=None, v_t=None):
        m_new = jnp.maximum(m_ref[...], jnp.max(s, axis=0, keepdims=True))
        alpha = jnp.exp(m_ref[...] - m_new)
        p = jnp.exp(s - m_new)
        l_ref[...] = alpha * l_ref[...] + jnp.sum(p, axis=0, keepdims=True)
        pv = _dot_tn(v, p) if v_t is None else _dot(v_t, p)
        acc_ref[...] = alpha * acc_ref[...] + pv
        m_ref[...] = m_new

    bias = bias_ref[pl.ds(pl.multiple_of(i * blk_step, blk_step), blk_step), :]
    per_page = PAGE_SIZE // SEL_BLOCK
    s_parts, v_parts = [], []
    for p in range(n_pages_step):
        page = page_refs[p][0]
        mb = jnp.concatenate([jnp.broadcast_to(bias[per_page * p + j:per_page * p + j + 1, :], (SEL_BLOCK, ncol))
                              for j in range(per_page)], axis=0)
        s_parts.append(_dot_tn(page[:KV_HALF], qbd) * SCALE + mb)
        v_parts.append(page[KV_HALF:].astype(BF16))
    update(jnp.concatenate(s_parts, axis=0), v_t=jnp.concatenate(v_parts, axis=1))

    @pl.when(i == pl.num_programs(1) - 1)
    def _():
        kv_n = jnp.concatenate([nsel_ref[0], jnp.zeros((pad_rows, 2 * KV_HALF), F32)], axis=0)
        u = lax.broadcasted_iota(jnp.int32, (kv_n.shape[0], 1), 0)
        nb_new = past // SEL_BLOCK
        s = scores(kv_n[:, :KV_HALF]) + bias_ref[nb_new:nb_new + 1, :]
        update(jnp.where((u < t_new) & (past + u <= qpos), s, NEG_BIG), kv_n[:, KV_HALF:])
        o = oth_ref[...] + sig[1:2] * acc_ref[...] / l_ref[...]
        g_row = lax.broadcasted_iota(jnp.int32, o.shape, 0) // HEAD_DIM
        g_col = (lax.broadcasted_iota(jnp.int32, o.shape, 1) // t_new) % B_KV
        o = jnp.where(g_row == g_col, o, 0.0)
        out_ref[0] = o[0:HEAD_DIM] + o[HEAD_DIM:2 * HEAD_DIM] + o[2 * HEAD_DIM:3 * HEAD_DIM] + o[3 * HEAD_DIM:]


def nsa_sample_attn(q, gate_logits, kc, vc, cache_kv_sel, state_kv_win, kv_sel_new, kv_win_new, page_table,
                    *, n_pages_step):
    bsz, t, _ = q.shape
    n_pages = page_table.shape[1]
    past = n_pages * PAGE_SIZE
    n_keep = state_kv_win.shape[1]
    assert t * B_GROUP * B_KV == Q_COLS and past % SEL_BLOCK == 0 and n_pages % n_pages_step == 0
    qt = q.reshape(bsz, t, B_KV, B_GROUP, HEAD_DIM).transpose(0, 2, 4, 3, 1)
    qbd = jnp.einsum('bgdrt,gh->bgdrht', qt, jnp.eye(B_KV, dtype=F32)).reshape(bsz, KV_HALF, Q_COLS).astype(BF16)
    gl = gate_logits.reshape(bsz, t, B_KV, B_GROUP, 3).transpose(0, 4, 3, 2, 1).reshape(bsz, 3, Q_COLS)
    gl = jnp.pad(gl, ((0, 0), (0, 5), (0, 0)))
    n_blk = past // SEL_BLOCK + 8
    per_b = lambda shape: pl.BlockSpec((1,) + shape, lambda b, i, pt: (b, 0, 0))
    page_specs = [pl.BlockSpec((1, 2 * KV_HALF, PAGE_SIZE), functools.partial(
        lambda b, i, pt, j: (pt[b, i * n_pages_step + j], 0, 0), j=j)) for j in range(n_pages_step)]
    kern = functools.partial(_sample_attn_kernel, n_pages_step=n_pages_step, past=past, t_new=t, n_keep=n_keep)
    out = pl.pallas_call(
        kern,
        grid_spec=pltpu.PrefetchScalarGridSpec(
            num_scalar_prefetch=1,
            grid=(bsz, n_pages // n_pages_step),
            in_specs=[per_b((KV_HALF, Q_COLS)), per_b((8, Q_COLS)), per_b(kc.shape[1:]), per_b(vc.shape[1:]),
                      per_b((n_keep, 2 * KV_HALF)), per_b((t, 2 * KV_HALF)), per_b((t, 2 * KV_HALF))] + page_specs,
            out_specs=pl.BlockSpec((1, HEAD_DIM, Q_COLS), lambda b, i, pt: (b, 0, 0)),
            scratch_shapes=[pltpu.VMEM((n_blk, Q_COLS), F32), pltpu.VMEM((1, Q_COLS), F32),
                            pltpu.VMEM((1, Q_COLS), F32), pltpu.VMEM((KV_HALF, Q_COLS), F32),
                            pltpu.VMEM((KV_HALF, Q_COLS), F32)]),
        out_shape=jax.ShapeDtypeStruct((bsz, HEAD_DIM, Q_COLS), F32),
        compiler_params=_params("parallel", "arbitrary"),
        name="nsa_sample_attn",
    )(page_table, qbd, gl, kc, vc, state_kv_win, kv_sel_new, kv_win_new, *([cache_kv_sel] * n_pages_step))
    return out.reshape(bsz, HEAD_DIM, B_GROUP, B_KV, t).transpose(0, 4, 3, 2, 1).reshape(bsz, t, N_Q_COLS)


def _tile(n, pref):
    return pref if n % pref == 0 else n


def _run_group(x, s0, is_prompt, caches, a_w_in, a_lb_logits, a_norm_g, a_w_out, b_w_qg, b_w_out, kv_w,
               cmp_pe, cmp_w1, cmp_b1, cmp_w2, ffn_w_in, ffn_w_out, moe_w_router, moe_b_router,
               moe_w_in, moe_w_out, ln_g, ln_b):
    bsz, t, d = x.shape
    n = bsz * t
    if is_prompt:
        h, s_out = hgrn_layer(x, s0, a_w_in[0], a_lb_logits, a_norm_g[0], a_w_out[0], ln_g[0, 0], ln_b[0, 0],
                              layer=0, seg=A_CHUNK, n_seg=8, carry=True)
    else:
        h, s_out = hgrn_layer(x, s0, a_w_in[0], a_lb_logits, a_norm_g[0], a_w_out[0], ln_g[0, 0], ln_b[0, 0],
                              layer=0, seg=t, n_seg=8, carry=False)
    h = h.reshape(n, d)
    h = ffn_layer(h, ffn_w_in[0], ffn_w_out[0], ln_g[0, 1], ln_b[0, 1], tm=_tile(n, 1024), tf=D_FF // 2)
    nq = B_HEADS * HEAD_DIM
    if is_prompt:
        kv_cmp, kv_sel, kv_win, qh, ksh, vsh, kwh, vwh, gts = nsa_proj(h.reshape(bsz, t, d), kv_w, b_w_qg[0], tm=512)
        n_ch = t // CMP_STRIDE
        kc, vc = compress(kv_cmp, None, cmp_pe, cmp_w1, cmp_b1, cmp_w2,
                          ch=n_ch, head_major=True)
        o = nsa_prompt_attn(qh, kc, vc, ksh, vsh, kwh, vwh, gts, tq=256, tk=256)
        h, hb, gate_t = proj_ln_router(o.reshape(n, nq), h, b_w_out[0], ln_g[1, 0], ln_b[1, 0],
                                       moe_w_router[0], moe_b_router[0], tm=_tile(n, 512))
        h = moe_layer(h, hb, gate_t, moe_w_in[0], moe_w_out[0], ln_g[1, 1], ln_b[1, 1], tm=_tile(n, 2048), tf=1792)
        kvshape = (bsz, t, 2, B_KV, HEAD_DIM)
        return (h.reshape(bsz, t, d), s_out[None], kv_cmp.reshape(kvshape), kv_sel.reshape(kvshape),
                kv_win.reshape(kvshape)[:, t - min(WINDOW, t):])
    cache_kv_cmp, cache_kv_sel, state_kv_win, page_table = caches
    n_pool = cache_kv_cmp.shape[0]
    n_keep = state_kv_win.shape[1]
    w_cat = jnp.concatenate([kv_w, b_w_qg[0]], axis=1)
    w_cat = jnp.pad(w_cat, ((0, 0), (0, (-w_cat.shape[1]) % 128)))
    z = proj(h, w_cat, tm=_tile(n, 512))
    kv = z[:, :N_KV_COLS].reshape(bsz, t, 3, KV_SET)
    q = z[:, N_KV_COLS:N_KV_COLS + nq].reshape(bsz, t, nq)
    gate_logits = z[:, N_KV_COLS + nq:N_KV_COLS + nq + 3 * B_HEADS].reshape(bsz, t, 3 * B_HEADS)
    assert (page_table.shape[1] * PAGE_SIZE + t) // CMP_STRIDE == page_table.shape[1] * PAGE_SIZE // CMP_STRIDE
    token_minor = lambda c: c.transpose(0, 2, 3, 4, 1).reshape(n_pool, KV_SET, PAGE_SIZE)
    kc, vc = compress(token_minor(cache_kv_cmp), page_table,
                      cmp_pe, cmp_w1, cmp_b1, cmp_w2, ch=256, head_major=False)
    o = nsa_sample_attn(q, gate_logits, kc, vc, token_minor(cache_kv_sel),
                        state_kv_win.reshape(bsz, n_keep, KV_SET), kv[:, :, 1], kv[:, :, 2], page_table,
                        n_pages_step=32)
    h, hb, gate_t = proj_ln_router(o.reshape(n, nq), h, b_w_out[0], ln_g[1, 0], ln_b[1, 0],
                                   moe_w_router[0], moe_b_router[0], tm=_tile(n, 512))
    h = moe_layer(h, hb, gate_t, moe_w_in[0], moe_w_out[0], ln_g[1, 1], ln_b[1, 1], tm=_tile(n, 2048), tf=896)
    kvshape = (bsz, t, 2, B_KV, HEAD_DIM)
    win_all = jnp.concatenate([state_kv_win, kv[:, :, 2].reshape(kvshape)], axis=1)
    return (h.reshape(bsz, t, d), s_out[None], kv[:, :, 0].reshape(kvshape), kv[:, :, 1].reshape(kvshape),
            win_all[:, -n_keep:])


def kernel(x_prompt, x_sample, state_hgrn, cache_kv_cmp, cache_kv_sel, state_kv_win, page_table,
           a_w_in, a_lb_logits, a_norm_g, a_w_out, b_w_qg, b_w_out, kv_w,
           cmp_pe, cmp_w1, cmp_b1, cmp_w2, ffn_w_in, ffn_w_out,
           moe_w_router, moe_b_router, moe_w_in, moe_w_out, ln_g, ln_b):
    weights = (a_w_in, a_lb_logits, a_norm_g, a_w_out, b_w_qg, b_w_out, kv_w, cmp_pe, cmp_w1, cmp_b1, cmp_w2,
               ffn_w_in, ffn_w_out, moe_w_router, moe_b_router, moe_w_in, moe_w_out, ln_g, ln_b)
    hgrn0 = jnp.zeros((x_prompt.shape[0], A_HEADS, A_DK, A_DV), F32)
    y_p, hg_p, cmp_p, sel_p, win_p = _run_group(x_prompt, hgrn0, True, None, *weights)
    y_s, hg_s, cmp_s, sel_s, win_s = _run_group(
        x_sample, state_hgrn[0], False, (cache_kv_cmp, cache_kv_sel, state_kv_win, page_table), *weights)
    return (y_p, y_s, hg_p, cmp_p, sel_p, win_p, hg_s, cmp_s, sel_s, win_s)
```
